```python
import math
import jax, jax.numpy as jnp
from jax import lax
import numpy as np

D_MODEL = 1024
BATCH = 8
SEQ = 4096
DEPTH = 2

HEAD_DIM = 64
N_HEADS_SB = 8
N_HEADS_FOX = 8
W_SB = N_HEADS_SB * HEAD_DIM
W_FOX = N_HEADS_FOX * HEAD_DIM
Q_BLOCK = 128
D_FF = 2816
N_EXPERTS = 8
TOP_K = 2
D_FF_EXPERT = 1408
N_DENSE = (DEPTH + 1) // 2
N_MOE = DEPTH // 2
RMS_EPS = 1e-6
PROJ_IN = 3 * W_SB + 3 * W_FOX + N_HEADS_FOX + 2 * D_MODEL

kernel_name = "hybrid_stickbreaking_fox_gated_moe"


def rmsnorm(x, g):
    x32 = x.astype(jnp.float32)
    y = x32 * lax.rsqrt(jnp.mean(x32 * x32, axis=-1, keepdims=True) + RMS_EPS)
    return (y * g.astype(jnp.float32)).astype(x.dtype)


def split_heads(t, n_heads):
    b, s, _ = t.shape
    return t.reshape(b, s, n_heads, HEAD_DIM).transpose(0, 2, 1, 3)


def merge_heads(t):
    b, h, s, d = t.shape
    return t.transpose(0, 2, 1, 3).reshape(b, s, h * d)


def stick_breaking_attention(q, k, v):
    seq = q.shape[2]
    scale = 1.0 / math.sqrt(HEAD_DIM)
    outs = []
    for i in range(seq // Q_BLOCK):
        t0 = i * Q_BLOCK
        kv_len = t0 + Q_BLOCK
        qb = q[:, :, t0:kv_len]
        kb = k[:, :, :kv_len]
        vb = v[:, :, :kv_len]
        z = jnp.einsum('bhqd,bhkd->bhqk', qb, kb).astype(jnp.float32) * scale
        q_pos = t0 + jnp.arange(Q_BLOCK)[:, None]
        k_pos = jnp.arange(kv_len)[None, :]
        mask = k_pos < q_pos
        log_beta = jax.nn.log_sigmoid(z)
        log_keep = jnp.where(mask, jax.nn.log_sigmoid(-z), 0.0)
        tail = lax.cumsum(log_keep, axis=log_keep.ndim - 1, reverse=True) - log_keep
        w = jnp.where(mask, jnp.exp(log_beta + tail), 0.0)
        outs.append(jnp.einsum('bhqk,bhkd->bhqd', w.astype(vb.dtype), vb))
    return jnp.concatenate(outs, axis=2)


def forgetting_attention(q, k, v, log_f):
    seq = q.shape[2]
    scale = 1.0 / math.sqrt(HEAD_DIM)
    cum_f = lax.cumsum(log_f, axis=log_f.ndim - 1)
    outs = []
    for i in range(seq // Q_BLOCK):
        t0 = i * Q_BLOCK
        kv_len = t0 + Q_BLOCK
        qb = q[:, :, t0:kv_len]
        kb = k[:, :, :kv_len]
        vb = v[:, :, :kv_len]
        z = jnp.einsum('bhqd,bhkd->bhqk', qb, kb).astype(jnp.float32) * scale
        decay = cum_f[:, :, t0:kv_len, None] - cum_f[:, :, None, :kv_len]
        q_pos = t0 + jnp.arange(Q_BLOCK)[:, None]
        k_pos = jnp.arange(kv_len)[None, :]
        logits = jnp.where(k_pos <= q_pos, z + decay, -jnp.inf)
        p = jax.nn.softmax(logits, axis=-1)
        outs.append(jnp.einsum('bhqk,bhkd->bhqd', p.astype(vb.dtype), vb))
    return jnp.concatenate(outs, axis=2)


def swiglu(h, w_gu, w_dn):
    gu = h @ w_gu
    g, u = jnp.split(gu, 2, axis=-1)
    return (jax.nn.silu(g) * u) @ w_dn


def moe_swiglu(h, w_router, w_gu_e, w_dn_e):
    logits = (h @ w_router).astype(jnp.float32)
    top_val, top_idx = lax.top_k(logits, TOP_K)
    top_w = jax.nn.softmax(top_val, axis=-1)
    combine = jnp.sum(jax.nn.one_hot(top_idx, N_EXPERTS, dtype=jnp.float32) * top_w[..., None], axis=-2)
    combine = combine.astype(h.dtype)
    out = jnp.zeros_like(h)
    for e in range(N_EXPERTS):
        out = out + combine[..., e:e + 1] * swiglu(h, w_gu_e[e], w_dn_e[e])
    return out


def setup_inputs(seed: int = 0) -> dict:
    key = jax.random.key(seed)
    ks = jax.random.split(key, 16)
    nrm = lambda k, shape, fan_in: jax.random.normal(k, shape, jnp.float32) * fan_in ** -0.5
    return {
        "x": jax.random.normal(ks[0], (BATCH, SEQ, D_MODEL), jnp.float32),
        "g_mix": 1.0 + 0.02 * jax.random.normal(ks[1], (DEPTH, D_MODEL), jnp.float32),
        "w_in": nrm(ks[2], (DEPTH, D_MODEL, PROJ_IN), D_MODEL),
        "b_f": 2.0 + 0.1 * jax.random.normal(ks[3], (DEPTH, N_HEADS_FOX), jnp.float32),
        "b_gate": 0.02 * jax.random.normal(ks[4], (DEPTH, 2 * D_MODEL), jnp.float32),
        "g_q": 1.0 + 0.02 * jax.random.normal(ks[5], (DEPTH, HEAD_DIM), jnp.float32),
        "g_k": 1.0 + 0.02 * jax.random.normal(ks[6], (DEPTH, HEAD_DIM), jnp.float32),
        "w_o_sb": nrm(ks[7], (DEPTH, W_SB, D_MODEL), W_SB),
        "w_o_fox": nrm(ks[8], (DEPTH, W_FOX, D_MODEL), W_FOX),
        "w_out": nrm(ks[9], (DEPTH, D_MODEL, D_MODEL), D_MODEL),
        "g_ffn": 1.0 + 0.02 * jax.random.normal(ks[10], (DEPTH, D_MODEL), jnp.float32),
        "w_gu_dense": nrm(ks[11], (N_DENSE, D_MODEL, 2 * D_FF), D_MODEL),
        "w_dn_dense": nrm(ks[12], (N_DENSE, D_FF, D_MODEL), D_FF),
        "w_router": nrm(ks[13], (N_MOE, D_MODEL, N_EXPERTS), D_MODEL),
        "w_gu_exp": nrm(ks[14], (N_MOE, N_EXPERTS, D_MODEL, 2 * D_FF_EXPERT), D_MODEL),
        "w_dn_exp": nrm(ks[15], (N_MOE, N_EXPERTS, D_FF_EXPERT, D_MODEL), D_FF_EXPERT),
    }


def reference(x, g_mix, w_in, b_f, b_gate, g_q, g_k, w_o_sb, w_o_fox, w_out,
              g_ffn, w_gu_dense, w_dn_dense, w_router, w_gu_exp, w_dn_exp):
    splits = [int(c) for c in np.cumsum([W_SB, W_SB, W_SB, W_FOX, W_FOX, W_FOX, N_HEADS_FOX])]
    for l in range(DEPTH):
        h = rmsnorm(x, g_mix[l])
        p = h @ w_in[l]
        q_sb, k_sb, v_sb, q_fx, k_fx, v_fx, f_pre, gate_pre = jnp.split(p, splits, axis=-1)

        y_sb = stick_breaking_attention(split_heads(q_sb, N_HEADS_SB),
                                        split_heads(k_sb, N_HEADS_SB),
                                        split_heads(v_sb, N_HEADS_SB))
        y_sb = merge_heads(y_sb) @ w_o_sb[l]

        qf = rmsnorm(split_heads(q_fx, N_HEADS_FOX), g_q[l])
        kf = rmsnorm(split_heads(k_fx, N_HEADS_FOX), g_k[l])
        log_f = jax.nn.log_sigmoid(f_pre.astype(jnp.float32) + b_f[l].astype(jnp.float32))
        y_fx = forgetting_attention(qf, kf, split_heads(v_fx, N_HEADS_FOX), log_f.transpose(0, 2, 1))
        y_fx = merge_heads(y_fx) @ w_o_fox[l]

        g_sb, g_fx = jnp.split(jax.nn.sigmoid(gate_pre + b_gate[l]), 2, axis=-1)
        x = x + (g_sb * y_sb + g_fx * y_fx) @ w_out[l]

        h = rmsnorm(x, g_ffn[l])
        if l % 2 == 0:
            x = x + swiglu(h, w_gu_dense[l // 2], w_dn_dense[l // 2])
        else:
            x = x + moe_swiglu(h, w_router[l // 2], w_gu_exp[l // 2], w_dn_exp[l // 2])
    return x
```

```python
import functools

import jax
import jax.numpy as jnp
from jax import lax
from jax.experimental import pallas as pl
from jax.experimental.pallas import tpu as pltpu

F32 = jnp.float32
BF16 = jnp.bfloat16

HEAD_DIM = 64
N_HEADS = 8
W_BRANCH = N_HEADS * HEAD_DIM
N_EXPERTS = 8
RMS_EPS = 1e-6
QK_SCALE = 1.0 / 8.0
LANES = 128
HEADS_PER_BLOCK = LANES // HEAD_DIM
V7X_VMEM_LIMIT_BYTES = 56 * 1024 * 1024

ATT_BLOCK = 256
CUM_BLOCK = 512


def _split_bf16(x, parts):
    out = []
    for _ in range(parts - 1):
        hi = x.astype(BF16)
        out.append(hi)
        x = x - hi.astype(F32)
    out.append(x.astype(BF16))
    return out


def _dot(a, b):
    return jnp.dot(a, b, preferred_element_type=F32)


def _dot_nt(a, b):
    return lax.dot_general(a, b, (((1,), (1,)), ((), ())), preferred_element_type=F32)


def _rmsnorm(x, g):
    ms = jnp.mean(x * x, axis=-1, keepdims=True)
    return x * lax.rsqrt(ms + RMS_EPS) * g


def _sigmoid(x):
    return 1.0 / (1.0 + jnp.exp(-x))


def _compiler_params(semantics):
    return pltpu.CompilerParams(dimension_semantics=semantics,
                                vmem_limit_bytes=V7X_VMEM_LIMIT_BYTES)


def _proj_kernel(x_ref, g_ref, w_ref, gq_ref, gk_ref, sb_ref, fx_ref, f_ref):
    h = _rmsnorm(x_ref[...], g_ref[...]).astype(BF16)
    wb = W_BRANCH
    r = lax.broadcasted_iota(jnp.int32, (wb, wb), 0) // HEAD_DIM
    c = lax.broadcasted_iota(jnp.int32, (wb, wb), 1) // HEAD_DIM
    head_mean = jnp.where(r == c, 1.0 / HEAD_DIM, 0.0).astype(BF16)
    for i in range(3):
        acc = _dot(h, w_ref[:, i * wb:(i + 1) * wb])
        if i == 0:
            acc = acc * QK_SCALE
        sb_ref[:, i * wb:(i + 1) * wb] = acc.astype(BF16)
    for i, gain_ref in enumerate((gq_ref, gk_ref, None)):
        acc = _dot(h, w_ref[:, (3 + i) * wb:(4 + i) * wb])
        if gain_ref is not None:
            sq_hi, sq_lo = _split_bf16(acc * acc, 2)
            ms = _dot(sq_hi, head_mean) + _dot(sq_lo, head_mean)
            acc = acc * lax.rsqrt(ms + RMS_EPS) * gain_ref[...]
        if i == 0:
            acc = acc * QK_SCALE
        fx_ref[:, i * wb:(i + 1) * wb] = acc.astype(BF16)
    f_ref[...] = _dot(h, w_ref[:, 6 * wb:6 * wb + LANES])


def _proj(x, g, w, gq, gk, tm):
    t, d = x.shape
    n = w.shape[1]
    full = lambda i: (0, 0)
    return pl.pallas_call(
        _proj_kernel,
        grid=(t // tm,),
        in_specs=[
            pl.BlockSpec((tm, d), lambda i: (i, 0)),
            pl.BlockSpec((1, d), full),
            pl.BlockSpec((d, n), full),
            pl.BlockSpec((1, W_BRANCH), full),
            pl.BlockSpec((1, W_BRANCH), full),
        ],
        out_specs=[
            pl.BlockSpec((tm, 3 * W_BRANCH), lambda i: (i, 0)),
            pl.BlockSpec((tm, 3 * W_BRANCH), lambda i: (i, 0)),
            pl.BlockSpec((tm, LANES), lambda i: (i, 0)),
        ],
        out_shape=[
            jax.ShapeDtypeStruct((t, 3 * W_BRANCH), BF16),
            jax.ShapeDtypeStruct((t, 3 * W_BRANCH), BF16),
            jax.ShapeDtypeStruct((t, LANES), F32),
        ],
        compiler_params=_compiler_params(("parallel",)),
        name="proj",
    )(x, g, w, gq, gk)


def _cumf_kernel(f_ref, b_ref, o_ref):
    seq = f_ref.shape[1]
    cb = min(CUM_BLOCK, seq)
    r = lax.broadcasted_iota(jnp.int32, (cb, cb), 0)
    c = lax.broadcasted_iota(jnp.int32, (cb, cb), 1)
    prefix = jnp.where(r <= c, 1.0, 0.0).astype(BF16)
    carry = jnp.zeros((N_HEADS, 1), F32)
    for i in range(seq // cb):
        v = f_ref[0, i * cb:(i + 1) * cb, :] + b_ref[...]
        log_f = jnp.minimum(v, 0.0) - jnp.log1p(jnp.exp(-jnp.abs(v)))
        log_f = log_f.T[:N_HEADS, :]
        cum = carry
        for part in _split_bf16(log_f, 3):
            cum = cum + _dot(part, prefix)
        o_ref[0, :, i * cb:(i + 1) * cb] = cum
        carry = cum[:, cb - 1:cb]


def _cumf(f_pre, b_f):
    b, s, _ = f_pre.shape
    return pl.pallas_call(
        _cumf_kernel,
        grid=(b,),
        in_specs=[
            pl.BlockSpec((1, s, LANES), lambda i: (i, 0, 0)),
            pl.BlockSpec((1, LANES), lambda i: (0, 0)),
        ],
        out_specs=pl.BlockSpec((1, N_HEADS, s), lambda i: (i, 0, 0)),
        out_shape=jax.ShapeDtypeStruct((b, N_HEADS, s), F32),
        compiler_params=_compiler_params(("parallel",)),
        name="cumf",
    )(f_pre, b_f)


def _head_queries(q_ref):
    q = q_ref[0].astype(F32)
    lane = lax.broadcasted_iota(jnp.int32, (1, LANES), 1)
    return [jnp.where((lane >= hh * HEAD_DIM) & (lane < (hh + 1) * HEAD_DIM), q, 0.0).astype(BF16)
            for hh in range(HEADS_PER_BLOCK)]


def _merge_head_outputs(outs):
    lane = lax.broadcasted_iota(jnp.int32, (1, LANES), 1)
    return jnp.where(lane < HEAD_DIM, outs[0], outs[1])


def _sb_kernel(q_ref, k_ref, v_ref, o_ref):
    bq = q_ref.shape[1]
    qi = pl.program_id(2)
    row = lax.broadcasted_iota(jnp.int32, (bq, bq), 0)
    col = lax.broadcasted_iota(jnp.int32, (bq, bq), 1)
    suffix = jnp.where(row >= col, 1.0, 0.0).astype(BF16)
    below_diag = col < row
    outs = []
    for qh in _head_queries(q_ref):

        def block(kb, carry, acc, masked):
            start = pl.multiple_of(kb * bq, bq)
            k = k_ref[0, pl.ds(start, bq), :]
            v = v_ref[0, pl.ds(start, bq), :]
            z = _dot_nt(qh, k)
            sp = jnp.maximum(z, 0.0) + jnp.log(1.0 + jnp.exp(-jnp.abs(z)))
            if masked:
                sp = jnp.where(below_diag, sp, 0.0)
            sp_hi, sp_lo = _split_bf16(sp, 2)
            incl = _dot(sp_hi, suffix) + _dot(sp_lo, suffix)
            w = jnp.exp(z - incl - carry)
            if masked:
                w = jnp.where(below_diag, w, 0.0)
            acc = acc + _dot(w.astype(BF16), v)
            return carry + incl[:, 0:1], acc

        carry, acc = block(qi, jnp.zeros((bq, 1), F32), jnp.zeros((bq, LANES), F32), True)
        carry, acc = lax.fori_loop(
            0, qi, lambda i, c: block(qi - 1 - i, c[0], c[1], False), (carry, acc))
        outs.append(acc)
    o_ref[0] = _merge_head_outputs(outs).astype(BF16)


def _fox_kernel(q_ref, k_ref, v_ref, f_ref, o_ref):
    bq = q_ref.shape[1]
    qi = pl.program_id(2)
    row = lax.broadcasted_iota(jnp.int32, (bq, bq), 0)
    col = lax.broadcasted_iota(jnp.int32, (bq, bq), 1)
    causal = col <= row
    outs = []
    for hh, qh in enumerate(_head_queries(q_ref)):

        def logits(kb):
            start = pl.multiple_of(kb * bq, bq)
            k = k_ref[0, pl.ds(start, bq), :]
            v = v_ref[0, pl.ds(start, bq), :]
            return _dot_nt(qh, k) - f_ref[0, 0, hh:hh + 1, pl.ds(start, bq)], v

        s, v = logits(qi)
        s = jnp.where(causal, s, -jnp.inf)
        m = jnp.max(s, axis=1, keepdims=True)
        p = jnp.exp(s - m)
        l = jnp.sum(p, axis=1, keepdims=True)
        acc = _dot(p.astype(BF16), v)

        def body(kb, c):
            m, l, acc = c
            s, v = logits(kb)
            m_new = jnp.maximum(m, jnp.max(s, axis=1, keepdims=True))
            alpha = jnp.exp(m - m_new)
            p = jnp.exp(s - m_new)
            l = alpha * l + jnp.sum(p, axis=1, keepdims=True)
            acc = alpha * acc + _dot(p.astype(BF16), v)
            return m_new, l, acc

        m, l, acc = lax.fori_loop(0, qi, body, (m, l, acc))
        outs.append(acc / l)
    o_ref[0] = _merge_head_outputs(outs).astype(BF16)


def _attention(kernel, qkv, cum_f=None):
    b, s, _ = qkv.shape
    bq = min(ATT_BLOCK, s)
    n_pairs = W_BRANCH // LANES
    in_specs = [
        pl.BlockSpec((1, bq, LANES), lambda bi, hp, qi: (bi, qi, hp)),
        pl.BlockSpec((1, s, LANES), lambda bi, hp, qi: (bi, 0, n_pairs + hp)),
        pl.BlockSpec((1, s, LANES), lambda bi, hp, qi: (bi, 0, 2 * n_pairs + hp)),
    ]
    args = [qkv, qkv, qkv]
    if cum_f is not None:
        in_specs.append(pl.BlockSpec((1, 1, HEADS_PER_BLOCK, s), lambda bi, hp, qi: (bi, hp, 0, 0)))
        args.append(cum_f.reshape(b, n_pairs, HEADS_PER_BLOCK, s))
    return pl.pallas_call(
        kernel,
        grid=(b, n_pairs, s // bq),
        in_specs=in_specs,
        out_specs=pl.BlockSpec((1, bq, LANES), lambda bi, hp, qi: (bi, qi, hp)),
        out_shape=jax.ShapeDtypeStruct((b, s, W_BRANCH), BF16),
        compiler_params=_compiler_params(("parallel", "parallel", "arbitrary")),
        name=kernel.__name__.strip("_"),
    )(*args)


def _top2_combine(logits):
    lane = lax.broadcasted_iota(jnp.int32, logits.shape, 1).astype(F32)
    lg = jnp.where(lane < N_EXPERTS, logits, -jnp.inf)
    m1 = jnp.max(lg, axis=1, keepdims=True)
    i1 = jnp.min(jnp.where(lg == m1, lane, float(LANES)), axis=1, keepdims=True)
    lg2 = jnp.where(lane == i1, -jnp.inf, lg)
    m2 = jnp.max(lg2, axis=1, keepdims=True)
    i2 = jnp.min(jnp.where(lg2 == m2, lane, float(LANES)), axis=1, keepdims=True)
    e2 = jnp.exp(m2 - m1)
    w1 = 1.0 / (1.0 + e2)
    return jnp.where(lane == i1, w1, 0.0) + jnp.where(lane == i2, e2 * w1, 0.0)


def _merge_kernel(x_ref, ysb_ref, yfx_ref, gmix_ref, wgate_ref, bgate_ref, wosb_ref, wofx_ref,
                  wout_ref, gffn_ref, *rest, with_router):
    if with_router:
        wr_hi_ref, wr_lo_ref, x1_ref, h2_ref, comb_ref = rest
    else:
        x1_ref, h2_ref = rest
    d = x_ref.shape[1]
    x = x_ref[...]
    h = _rmsnorm(x, gmix_ref[...]).astype(BF16)
    merged = None
    for i, (y_ref, wo_ref) in enumerate(((ysb_ref, wosb_ref), (yfx_ref, wofx_ref))):
        gate = _sigmoid(_dot(h, wgate_ref[:, i * d:(i + 1) * d]) + bgate_ref[:, i * d:(i + 1) * d])
        branch = gate * _dot(y_ref[...], wo_ref[...])
        merged = branch if merged is None else merged + branch
    x1 = x + _dot(merged.astype(BF16), wout_ref[...])
    x1_ref[...] = x1
    h2 = _rmsnorm(x1, gffn_ref[...])
    h2_ref[...] = h2.astype(BF16)
    if with_router:
        h_hi, h_lo = _split_bf16(h2, 2)
        logits = _dot(h_hi, wr_hi_ref[...]) + _dot(h_lo, wr_hi_ref[...]) + _dot(h_hi, wr_lo_ref[...])
        comb_ref[...] = _top2_combine(logits)


def _merge(x, y_sb, y_fx, g_mix, w_gate, b_gate, w_o_sb, w_o_fox, w_out, g_ffn, w_router, tm):
    t, d = x.shape
    with_router = w_router is not None
    full = lambda i: (0, 0)
    tile = lambda i: (i, 0)
    in_specs = [
        pl.BlockSpec((tm, d), tile),
        pl.BlockSpec((tm, W_BRANCH), tile),
        pl.BlockSpec((tm, W_BRANCH), tile),
        pl.BlockSpec((1, d), full),
        pl.BlockSpec((d, 2 * d), full),
        pl.BlockSpec((1, 2 * d), full),
        pl.BlockSpec((W_BRANCH, d), full),
        pl.BlockSpec((W_BRANCH, d), full),
        pl.BlockSpec((d, d), full),
        pl.BlockSpec((1, d), full),
    ]
    args = [x, y_sb, y_fx, g_mix, w_gate, b_gate, w_o_sb, w_o_fox, w_out, g_ffn]
    out_specs = [pl.BlockSpec((tm, d), tile), pl.BlockSpec((tm, d), tile)]
    out_shape = [jax.ShapeDtypeStruct((t, d), F32), jax.ShapeDtypeStruct((t, d), BF16)]
    if with_router:
        in_specs += [pl.BlockSpec((d, LANES), full)] * 2
        args += list(w_router)
        out_specs.append(pl.BlockSpec((tm, LANES), tile))
        out_shape.append(jax.ShapeDtypeStruct((t, LANES), F32))
    return pl.pallas_call(
        functools.partial(_merge_kernel, with_router=with_router),
        grid=(t // tm,),
        in_specs=in_specs,
        out_specs=out_specs,
        out_shape=out_shape,
        compiler_params=_compiler_params(("parallel",)),
        name="merge_router" if with_router else "merge",
    )(*args)


def _swiglu_chunk(h, wg, wu, wd):
    g = _dot(h, wg)
    u = _dot(h, wu)
    return _dot((g * _sigmoid(g) * u).astype(BF16), wd)


def _ffn_kernel(h_ref, x_ref, wg_ref, wu_ref, wd_ref, o_ref):
    @pl.when(pl.program_id(1) == 0)
    def _():
        o_ref[...] = x_ref[...]

    o_ref[...] += _swiglu_chunk(h_ref[...], wg_ref[...], wu_ref[...], wd_ref[...])


def _ffn(h, x, w_gu, w_dn, tm, tf):
    t, d = x.shape
    d_ff = w_dn.shape[0]
    nf = d_ff // tf
    return pl.pallas_call(
        _ffn_kernel,
        grid=(t // tm, nf),
        in_specs=[
            pl.BlockSpec((tm, d), lambda i, j: (i, 0)),
            pl.BlockSpec((tm, d), lambda i, j: (i, 0)),
            pl.BlockSpec((d, tf), lambda i, j: (0, j)),
            pl.BlockSpec((d, tf), lambda i, j: (0, nf + j)),
            pl.BlockSpec((tf, d), lambda i, j: (j, 0)),
        ],
        out_specs=pl.BlockSpec((tm, d), lambda i, j: (i, 0)),
        out_shape=jax.ShapeDtypeStruct((t, d), F32),
        compiler_params=_compiler_params(("parallel", "arbitrary")),
        name="ffn",
    )(h, x, w_gu, w_gu, w_dn)


def _moe_kernel(h_ref, x_ref, comb_ref, wg_ref, wu_ref, wd_ref, o_ref):
    e = pl.program_id(1)

    @pl.when(e == 0)
    def _():
        o_ref[...] = x_ref[...]

    comb = comb_ref[...]
    lane = lax.broadcasted_iota(jnp.int32, comb.shape, 1)
    weight = jnp.sum(jnp.where(lane == e, comb, 0.0), axis=1, keepdims=True)
    o_ref[...] += weight * _swiglu_chunk(h_ref[...], wg_ref[0], wu_ref[0], wd_ref[0])


def _moe(h, x, comb, w_gu, w_dn, tm):
    t, d = x.shape
    n_exp, d_ff, _ = w_dn.shape
    return pl.pallas_call(
        _moe_kernel,
        grid=(t // tm, n_exp),
        in_specs=[
            pl.BlockSpec((tm, d), lambda i, e: (i, 0)),
            pl.BlockSpec((tm, d), lambda i, e: (i, 0)),
            pl.BlockSpec((tm, LANES), lambda i, e: (i, 0)),
            pl.BlockSpec((1, d, d_ff), lambda i, e: (e, 0, 0)),
            pl.BlockSpec((1, d, d_ff), lambda i, e: (e, 0, 1)),
            pl.BlockSpec((1, d_ff, d), lambda i, e: (e, 0, 0)),
        ],
        out_specs=pl.BlockSpec((tm, d), lambda i, e: (i, 0)),
        out_shape=jax.ShapeDtypeStruct((t, d), F32),
        compiler_params=_compiler_params(("parallel", "arbitrary")),
        name="moe",
    )(h, x, comb, w_gu, w_gu, w_dn)


def _pad_lanes(w):
    return jnp.pad(w, ((0, 0), (0, LANES - w.shape[1])))


def kernel(x, g_mix, w_in, b_f, b_gate, g_q, g_k, w_o_sb, w_o_fox, w_out, g_ffn, w_gu_dense,
           w_dn_dense, w_router, w_gu_exp, w_dn_exp):
    batch, seq, d = x.shape
    depth = w_in.shape[0]
    t = batch * seq
    tm_proj = min(512, t)
    tm_ffn = min(1024, t)
    n_qkv = 6 * W_BRANCH
    xt = x.reshape(t, d)
    for l in range(depth):
        w_l = w_in[l]
        w_proj = jnp.concatenate(
            [w_l[:, :n_qkv], _pad_lanes(w_l[:, n_qkv:n_qkv + N_HEADS])], axis=1).astype(BF16)
        w_gate = w_l[:, n_qkv + N_HEADS:].astype(BF16)
        row = lambda v: v.reshape(1, -1)
        gq = jnp.tile(g_q[l], N_HEADS).reshape(1, W_BRANCH)
        gk = jnp.tile(g_k[l], N_HEADS).reshape(1, W_BRANCH)
        qkv_sb, qkv_fx, f_pre = _proj(xt, row(g_mix[l]), w_proj, gq, gk, tm_proj)
        cum_f = _cumf(f_pre.reshape(batch, seq, LANES), _pad_lanes(row(b_f[l])))
        y_sb = _attention(_sb_kernel, qkv_sb.reshape(batch, seq, 3 * W_BRANCH))
        y_fx = _attention(_fox_kernel, qkv_fx.reshape(batch, seq, 3 * W_BRANCH), cum_f)
        moe_layer = l % 2 == 1
        router = None
        if moe_layer:
            wr = _pad_lanes(w_router[l // 2])
            router = _split_bf16(wr, 2)
        outs = _merge(xt, y_sb.reshape(t, W_BRANCH), y_fx.reshape(t, W_BRANCH), row(g_mix[l]),
                      w_gate, row(b_gate[l]), w_o_sb[l].astype(BF16), w_o_fox[l].astype(BF16),
                      w_out[l].astype(BF16), row(g_ffn[l]), router, tm_proj)
        if moe_layer:
            x1, h2, comb = outs
            xt = _moe(h2, x1, comb, w_gu_exp[l // 2].astype(BF16), w_dn_exp[l // 2].astype(BF16),
                      tm_proj)
        else:
            x1, h2 = outs
            xt = _ffn(h2, x1, w_gu_dense[l // 2].astype(BF16), w_dn_dense[l // 2].astype(BF16),
                      tm_ffn, 256)
    return xt.reshape(batch, seq, d)
```

```python
import functools

import jax
import jax.numpy as jnp
from jax import lax
from jax.experimental import pallas as pl
from jax.experimental.pallas import tpu as pltpu

F32 = jnp.float32
BF16 = jnp.bfloat16

HEAD_DIM = 64
N_HEADS = 8
W_BRANCH = N_HEADS * HEAD_DIM
N_EXPERTS = 8
RMS_EPS = 1e-6
QK_SCALE = 1.0 / 8.0
LANES = 128
HEADS_PER_BLOCK = LANES // HEAD_DIM
V7X_VMEM_LIMIT_BYTES = 56 * 1024 * 1024

ATT_BLOCK = 256
ATT_QUERY_BLOCK = 2 * ATT_BLOCK
CUM_BLOCK = 512


def _split_bf16(x, parts):
    out = []
    for _ in range(parts - 1):
        hi = x.astype(BF16)
        out.append(hi)
        x = x - hi.astype(F32)
    out.append(x.astype(BF16))
    return out


def _dot(a, b):
    return jnp.dot(a, b, preferred_element_type=F32)


def _dot_nt(a, b):
    return lax.dot_general(a, b, (((1,), (1,)), ((), ())), preferred_element_type=F32)


def _rmsnorm(x, g):
    ms = jnp.mean(x * x, axis=-1, keepdims=True)
    return x * lax.rsqrt(ms + RMS_EPS) * g


def _sigmoid(x):
    return 1.0 / (1.0 + jnp.exp(-x))


def _compiler_params(semantics):
    return pltpu.CompilerParams(dimension_semantics=semantics,
                                vmem_limit_bytes=V7X_VMEM_LIMIT_BYTES)


def _proj_kernel(x_ref, g_ref, w_ref, gq_ref, gk_ref, sb_ref, fx_ref, f_ref):
    h = _rmsnorm(x_ref[...], g_ref[...]).astype(BF16)
    wb = W_BRANCH
    r = lax.broadcasted_iota(jnp.int32, (wb, wb), 0) // HEAD_DIM
    c = lax.broadcasted_iota(jnp.int32, (wb, wb), 1) // HEAD_DIM
    head_mean = jnp.where(r == c, 1.0 / HEAD_DIM, 0.0).astype(BF16)
    for i in range(3):
        acc = _dot(h, w_ref[:, i * wb:(i + 1) * wb])
        if i == 0:
            acc = acc * QK_SCALE
        sb_ref[:, i * wb:(i + 1) * wb] = acc.astype(BF16)
    for i, gain_ref in enumerate((gq_ref, gk_ref, None)):
        acc = _dot(h, w_ref[:, (3 + i) * wb:(4 + i) * wb])
        if gain_ref is not None:
            sq_hi, sq_lo = _split_bf16(acc * acc, 2)
            ms = _dot(sq_hi, head_mean) + _dot(sq_lo, head_mean)
            acc = acc * lax.rsqrt(ms + RMS_EPS) * gain_ref[...]
        if i == 0:
            acc = acc * QK_SCALE
        fx_ref[:, i * wb:(i + 1) * wb] = acc.astype(BF16)
    f_ref[...] = _dot(h, w_ref[:, 6 * wb:6 * wb + LANES])


def _proj(x, g, w, gq, gk, tm):
    t, d = x.shape
    n = w.shape[1]
    full = lambda i: (0, 0)
    return pl.pallas_call(
        _proj_kernel,
        grid=(t // tm,),
        in_specs=[
            pl.BlockSpec((tm, d), lambda i: (i, 0)),
            pl.BlockSpec((1, d), full),
            pl.BlockSpec((d, n), full),
            pl.BlockSpec((1, W_BRANCH), full),
            pl.BlockSpec((1, W_BRANCH), full),
        ],
        out_specs=[
            pl.BlockSpec((tm, 3 * W_BRANCH), lambda i: (i, 0)),
            pl.BlockSpec((tm, 3 * W_BRANCH), lambda i: (i, 0)),
            pl.BlockSpec((tm, LANES), lambda i: (i, 0)),
        ],
        out_shape=[
            jax.ShapeDtypeStruct((t, 3 * W_BRANCH), BF16),
            jax.ShapeDtypeStruct((t, 3 * W_BRANCH), BF16),
            jax.ShapeDtypeStruct((t, LANES), F32),
        ],
        compiler_params=_compiler_params(("parallel",)),
        name="proj",
    )(x, g, w, gq, gk)


def _cumf_kernel(f_ref, b_ref, o_ref):
    seq = f_ref.shape[1]
    cb = min(CUM_BLOCK, seq)
    r = lax.broadcasted_iota(jnp.int32, (cb, cb), 0)
    c = lax.broadcasted_iota(jnp.int32, (cb, cb), 1)
    prefix = jnp.where(r <= c, 1.0, 0.0).astype(BF16)
    carry = jnp.zeros((N_HEADS, 1), F32)
    for i in range(seq // cb):
        v = f_ref[0, i * cb:(i + 1) * cb, :] + b_ref[...]
        log_f = jnp.minimum(v, 0.0) - jnp.log1p(jnp.exp(-jnp.abs(v)))
        log_f = log_f.T[:N_HEADS, :]
        cum = carry
        for part in _split_bf16(log_f, 3):
            cum = cum + _dot(part, prefix)
        o_ref[0, :, i * cb:(i + 1) * cb] = cum
        carry = cum[:, cb - 1:cb]


def _cumf(f_pre, b_f):
    b, s, _ = f_pre.shape
    return pl.pallas_call(
        _cumf_kernel,
        grid=(b,),
        in_specs=[
            pl.BlockSpec((1, s, LANES), lambda i: (i, 0, 0)),
            pl.BlockSpec((1, LANES), lambda i: (0, 0)),
        ],
        out_specs=pl.BlockSpec((1, N_HEADS, s), lambda i: (i, 0, 0)),
        out_shape=jax.ShapeDtypeStruct((b, N_HEADS, s), F32),
        compiler_params=_compiler_params(("parallel",)),
        name="cumf",
    )(f_pre, b_f)


def _head_queries(q_ref, rows):
    q = q_ref[0, rows, :].astype(F32)
    lane = lax.broadcasted_iota(jnp.int32, (1, LANES), 1)
    return [jnp.where((lane >= hh * HEAD_DIM) & (lane < (hh + 1) * HEAD_DIM), q, 0.0).astype(BF16)
            for hh in range(HEADS_PER_BLOCK)]


def _merge_head_outputs(outs):
    lane = lax.broadcasted_iota(jnp.int32, (1, LANES), 1)
    return jnp.where(lane < HEAD_DIM, outs[0], outs[1])


def _run_chains(q_ref, k_ref, v_ref, o_ref, stages, finish):
    bk = ATT_BLOCK
    n_sub = q_ref.shape[1] // bk
    first = pl.program_id(2) * n_sub

    def load_kv(kb):
        start = pl.multiple_of(kb * bk, bk)
        return k_ref[0, pl.ds(start, bk), :], v_ref[0, pl.ds(start, bk), :], start

    chains = [(hh, r, q) for r in range(n_sub)
              for hh, q in enumerate(_head_queries(q_ref, slice(r * bk, (r + 1) * bk)))]

    def advance(jobs, states):
        args = [(chains[c][0], chains[c][2], *kv, masked) for c, kv, masked in jobs]
        tmps = [None] * len(jobs)
        for stage in stages[:-1]:
            tmps = [stage(*a, tmp) for a, tmp in zip(args, tmps)]
        states = list(states)
        for (c, _, _), a, tmp in zip(jobs, args, tmps):
            states[c] = stages[-1](*a, tmp, states[c])
        return tuple(states)

    diag = [load_kv(first + r) for r in range(n_sub)]
    states = advance([(c, diag[r - j], j == 0) for j in range(n_sub)
                      for c, (_, r, _) in enumerate(chains) if r >= j], [None] * len(chains))

    def body(i, states):
        blocks = [load_kv(first - 1 - i * n_sub - u) for u in range(n_sub)]
        return advance([(c, kv, False) for kv in blocks for c in range(len(chains))], states)

    states = lax.fori_loop(0, pl.program_id(2), body, states)
    outs = {(hh, r): finish(hh, state) for (hh, r, _), state in zip(chains, states)}
    for r in range(n_sub):
        merged = _merge_head_outputs([outs[(hh, r)] for hh in range(HEADS_PER_BLOCK)])
        o_ref[0, r * bk:(r + 1) * bk, :] = merged.astype(BF16)


def _sb_kernel(q_ref, k_ref, v_ref, o_ref):
    bk = ATT_BLOCK
    row = lax.broadcasted_iota(jnp.int32, (2 * bk, bk), 0)
    col = lax.broadcasted_iota(jnp.int32, (2 * bk, bk), 1)
    suffix2 = jnp.where((row >= col) & ((row < bk) | (row - bk >= col)), 1.0, 0.0).astype(BF16)
    below_diag = (lax.broadcasted_iota(jnp.int32, (bk, bk), 1)
                  < lax.broadcasted_iota(jnp.int32, (bk, bk), 0))

    def scores(hh, q, k, v, start, masked, tmp):
        z = _dot_nt(q, k)
        sp = jnp.maximum(z, 0.0) + jnp.log(1.0 + jnp.exp(-jnp.abs(z)))
        if masked:
            sp = jnp.where(below_diag, sp, 0.0)
        return z, jnp.concatenate(_split_bf16(sp, 2), axis=1)

    def suffix_sums(hh, q, k, v, start, masked, tmp):
        z, sp_parts = tmp
        return z, _dot(sp_parts, suffix2)

    def weights(hh, q, k, v, start, masked, tmp, state):
        z, incl = tmp
        arg = z - incl
        if state is not None:
            arg = arg - state[0]
        w = jnp.exp(arg)
        if masked:
            w = jnp.where(below_diag, w, 0.0)
        pv = _dot(w.astype(BF16), v)
        if state is None:
            return incl[:, 0:1], pv
        return state[0] + incl[:, 0:1], state[1] + pv

    _run_chains(q_ref, k_ref, v_ref, o_ref, (scores, suffix_sums, weights), lambda hh, state: state[1])


def _fox_kernel(q_ref, k_ref, v_ref, f_ref, o_ref):
    bk = ATT_BLOCK
    row = lax.broadcasted_iota(jnp.int32, (bk, bk), 0)
    col = lax.broadcasted_iota(jnp.int32, (bk, bk), 1)
    causal = col <= row
    lane = lax.broadcasted_iota(jnp.int32, (1, LANES), 1)
    values = {}

    def values_with_ones(hh, v):
        if (hh, id(v)) not in values:
            own = (lane >= hh * HEAD_DIM) & (lane < (hh + 1) * HEAD_DIM)
            values[(hh, id(v))] = (v, jnp.where(own, v.astype(F32), 1.0).astype(BF16))
        return values[(hh, id(v))][1]

    def logits(hh, q, k, v, start, masked, tmp):
        s = _dot_nt(q, k) - f_ref[0, 0, hh:hh + 1, pl.ds(start, bk)]
        if masked:
            s = jnp.where(causal, s, -jnp.inf)
        return s, jnp.max(s, axis=1, keepdims=True)

    def accumulate(hh, q, k, v, start, masked, tmp, state):
        s, m_blk = tmp
        v_ones = values_with_ones(hh, v)
        if state is None:
            return m_blk, _dot(jnp.exp(s - m_blk).astype(BF16), v_ones)
        m, acc = state
        m_new = jnp.maximum(m, m_blk)
        pv = _dot(jnp.exp(s - m_new).astype(BF16), v_ones)
        return m_new, jnp.exp(m - m_new) * acc + pv

    def finish(hh, state):
        acc = state[1]
        ones_lane = (1 - hh) * HEAD_DIM
        return acc / acc[:, ones_lane:ones_lane + 1]

    _run_chains(q_ref, k_ref, v_ref, o_ref, (logits, accumulate), finish)


def _attention(kernel, qkv, cum_f=None):
    b, s, _ = qkv.shape
    bq = min(ATT_QUERY_BLOCK, s)
    n_pairs = W_BRANCH // LANES
    in_specs = [
        pl.BlockSpec((1, bq, LANES), lambda bi, hp, qi: (bi, qi, hp)),
        pl.BlockSpec((1, s, LANES), lambda bi, hp, qi: (bi, 0, n_pairs + hp)),
        pl.BlockSpec((1, s, LANES), lambda bi, hp, qi: (bi, 0, 2 * n_pairs + hp)),
    ]
    args = [qkv, qkv, qkv]
    if cum_f is not None:
        in_specs.append(pl.BlockSpec((1, 1, HEADS_PER_BLOCK, s), lambda bi, hp, qi: (bi, hp, 0, 0)))
        args.append(cum_f.reshape(b, n_pairs, HEADS_PER_BLOCK, s))
    return pl.pallas_call(
        kernel,
        grid=(b, n_pairs, s // bq),
        in_specs=in_specs,
        out_specs=pl.BlockSpec((1, bq, LANES), lambda bi, hp, qi: (bi, qi, hp)),
        out_shape=jax.ShapeDtypeStruct((b, s, W_BRANCH), BF16),
        compiler_params=_compiler_params(("parallel", "parallel", "arbitrary")),
        name=kernel.__name__.strip("_"),
    )(*args)


def _top2_combine(logits):
    lane = lax.broadcasted_iota(jnp.int32, logits.shape, 1).astype(F32)
    lg = jnp.where(lane < N_EXPERTS, logits, -jnp.inf)
    m1 = jnp.max(lg, axis=1, keepdims=True)
    i1 = jnp.min(jnp.where(lg == m1, lane, float(LANES)), axis=1, keepdims=True)
    lg2 = jnp.where(lane == i1, -jnp.inf, lg)
    m2 = jnp.max(lg2, axis=1, keepdims=True)
    i2 = jnp.min(jnp.where(lg2 == m2, lane, float(LANES)), axis=1, keepdims=True)
    e2 = jnp.exp(m2 - m1)
    w1 = 1.0 / (1.0 + e2)
    return jnp.where(lane == i1, w1, 0.0) + jnp.where(lane == i2, e2 * w1, 0.0)


def _merge_kernel(x_ref, ysb_ref, yfx_ref, gmix_ref, wgate_ref, bgate_ref, wosb_ref, wofx_ref,
                  wout_ref, gffn_ref, *rest, with_router):
    if with_router:
        wr_hi_ref, wr_lo_ref, x1_ref, h2_ref, comb_ref = rest
    else:
        x1_ref, h2_ref = rest
    d = x_ref.shape[1]
    x = x_ref[...]
    h = _rmsnorm(x, gmix_ref[...]).astype(BF16)
    merged = None
    for i, (y_ref, wo_ref) in enumerate(((ysb_ref, wosb_ref), (yfx_ref, wofx_ref))):
        gate = _sigmoid(_dot(h, wgate_ref[:, i * d:(i + 1) * d]) + bgate_ref[:, i * d:(i + 1) * d])
        branch = gate * _dot(y_ref[...], wo_ref[...])
        merged = branch if merged is None else merged + branch
    x1 = x + _dot(merged.astype(BF16), wout_ref[...])
    x1_ref[...] = x1
    h2 = _rmsnorm(x1, gffn_ref[...])
    h2_ref[...] = h2.astype(BF16)
    if with_router:
        h_hi, h_lo = _split_bf16(h2, 2)
        logits = _dot(h_hi, wr_hi_ref[...]) + _dot(h_lo, wr_hi_ref[...]) + _dot(h_hi, wr_lo_ref[...])
        comb_ref[...] = _top2_combine(logits)


def _merge(x, y_sb, y_fx, g_mix, w_gate, b_gate, w_o_sb, w_o_fox, w_out, g_ffn, w_router, tm):
    t, d = x.shape
    with_router = w_router is not None
    full = lambda i: (0, 0)
    tile = lambda i: (i, 0)
    in_specs = [
        pl.BlockSpec((tm, d), tile),
        pl.BlockSpec((tm, W_BRANCH), tile),
        pl.BlockSpec((tm, W_BRANCH), tile),
        pl.BlockSpec((1, d), full),
        pl.BlockSpec((d, 2 * d), full),
        pl.BlockSpec((1, 2 * d), full),
        pl.BlockSpec((W_BRANCH, d), full),
        pl.BlockSpec((W_BRANCH, d), full),
        pl.BlockSpec((d, d), full),
        pl.BlockSpec((1, d), full),
    ]
    args = [x, y_sb, y_fx, g_mix, w_gate, b_gate, w_o_sb, w_o_fox, w_out, g_ffn]
    out_specs = [pl.BlockSpec((tm, d), tile), pl.BlockSpec((tm, d), tile)]
    out_shape = [jax.ShapeDtypeStruct((t, d), F32), jax.ShapeDtypeStruct((t, d), BF16)]
    if with_router:
        in_specs += [pl.BlockSpec((d, LANES), full)] * 2
        args += list(w_router)
        out_specs.append(pl.BlockSpec((tm, LANES), tile))
        out_shape.append(jax.ShapeDtypeStruct((t, LANES), F32))
    return pl.pallas_call(
        functools.partial(_merge_kernel, with_router=with_router),
        grid=(t // tm,),
        in_specs=in_specs,
        out_specs=out_specs,
        out_shape=out_shape,
        compiler_params=_compiler_params(("parallel",)),
        name="merge_router" if with_router else "merge",
    )(*args)


def _swiglu_chunk(h, wg, wu, wd):
    g = _dot(h, wg)
    u = _dot(h, wu)
    return _dot((g * _sigmoid(g) * u).astype(BF16), wd)


def _ffn_kernel(h_ref, x_ref, wg_ref, wu_ref, wd_ref, o_ref):
    @pl.when(pl.program_id(1) == 0)
    def _():
        o_ref[...] = x_ref[...]

    o_ref[...] += _swiglu_chunk(h_ref[...], wg_ref[...], wu_ref[...], wd_ref[...])


def _ffn(h, x, w_gu, w_dn, tm, tf):
    t, d = x.shape
    d_ff = w_dn.shape[0]
    nf = d_ff // tf
    return pl.pallas_call(
        _ffn_kernel,
        grid=(t // tm, nf),
        in_specs=[
            pl.BlockSpec((tm, d), lambda i, j: (i, 0)),
            pl.BlockSpec((tm, d), lambda i, j: (i, 0)),
            pl.BlockSpec((d, tf), lambda i, j: (0, j)),
            pl.BlockSpec((d, tf), lambda i, j: (0, nf + j)),
            pl.BlockSpec((tf, d), lambda i, j: (j, 0)),
        ],
        out_specs=pl.BlockSpec((tm, d), lambda i, j: (i, 0)),
        out_shape=jax.ShapeDtypeStruct((t, d), F32),
        compiler_params=_compiler_params(("parallel", "arbitrary")),
        name="ffn",
    )(h, x, w_gu, w_gu, w_dn)


def _moe_kernel(h_ref, x_ref, comb_ref, wg_ref, wu_ref, wd_ref, o_ref):
    e = pl.program_id(1)

    @pl.when(e == 0)
    def _():
        o_ref[...] = x_ref[...]

    comb = comb_ref[...]
    lane = lax.broadcasted_iota(jnp.int32, comb.shape, 1)
    weight = jnp.sum(jnp.where(lane == e, comb, 0.0), axis=1, keepdims=True)
    o_ref[...] += weight * _swiglu_chunk(h_ref[...], wg_ref[0], wu_ref[0], wd_ref[0])


def _moe(h, x, comb, w_gu, w_dn, tm):
    t, d = x.shape
    n_exp, d_ff, _ = w_dn.shape
    return pl.pallas_call(
        _moe_kernel,
        grid=(t // tm, n_exp),
        in_specs=[
            pl.BlockSpec((tm, d), lambda i, e: (i, 0)),
            pl.BlockSpec((tm, d), lambda i, e: (i, 0)),
            pl.BlockSpec((tm, LANES), lambda i, e: (i, 0)),
            pl.BlockSpec((1, d, d_ff), lambda i, e: (e, 0, 0)),
            pl.BlockSpec((1, d, d_ff), lambda i, e: (e, 0, 1)),
            pl.BlockSpec((1, d_ff, d), lambda i, e: (e, 0, 0)),
        ],
        out_specs=pl.BlockSpec((tm, d), lambda i, e: (i, 0)),
        out_shape=jax.ShapeDtypeStruct((t, d), F32),
        compiler_params=_compiler_params(("parallel", "arbitrary")),
        name="moe",
    )(h, x, comb, w_gu, w_gu, w_dn)


def _pad_lanes(w):
    return jnp.pad(w, ((0, 0), (0, LANES - w.shape[1])))


def kernel(x, g_mix, w_in, b_f, b_gate, g_q, g_k, w_o_sb, w_o_fox, w_out, g_ffn, w_gu_dense,
           w_dn_dense, w_router, w_gu_exp, w_dn_exp):
    batch, seq, d = x.shape
    depth = w_in.shape[0]
    t = batch * seq
    tm_proj = min(512, t)
    tm_ffn = min(1024, t)
    n_qkv = 6 * W_BRANCH
    xt = x.reshape(t, d)
    for l in range(depth):
        w_l = w_in[l]
        w_proj = jnp.concatenate(
            [w_l[:, :n_qkv], _pad_lanes(w_l[:, n_qkv:n_qkv + N_HEADS])], axis=1).astype(BF16)
        w_gate = w_l[:, n_qkv + N_HEADS:].astype(BF16)
        row = lambda v: v.reshape(1, -1)
        gq = jnp.tile(g_q[l], N_HEADS).reshape(1, W_BRANCH)
        gk = jnp.tile(g_k[l], N_HEADS).reshape(1, W_BRANCH)
        qkv_sb, qkv_fx, f_pre = _proj(xt, row(g_mix[l]), w_proj, gq, gk, tm_proj)
        cum_f = _cumf(f_pre.reshape(batch, seq, LANES), _pad_lanes(row(b_f[l])))
        y_sb = _attention(_sb_kernel, qkv_sb.reshape(batch, seq, 3 * W_BRANCH))
        y_fx = _attention(_fox_kernel, qkv_fx.reshape(batch, seq, 3 * W_BRANCH), cum_f)
        moe_layer = l % 2 == 1
        router = None
        if moe_layer:
            wr = _pad_lanes(w_router[l // 2])
            router = _split_bf16(wr, 2)
        outs = _merge(xt, y_sb.reshape(t, W_BRANCH), y_fx.reshape(t, W_BRANCH), row(g_mix[l]),
                      w_gate, row(b_gate[l]), w_o_sb[l].astype(BF16), w_o_fox[l].astype(BF16),
                      w_out[l].astype(BF16), row(g_ffn[l]), router, tm_proj)
        if moe_layer:
            x1, h2, comb = outs
            xt = _moe(h2, x1, comb, w_gu_exp[l // 2].astype(BF16), w_dn_exp[l // 2].astype(BF16),
                      tm_proj)
        else:
            x1, h2 = outs
            xt = _ffn(h2, x1, w_gu_dense[l // 2].astype(BF16), w_dn_dense[l // 2].astype(BF16),
                      tm_ffn, 256)
    return xt.reshape(batch, seq, d)
```

```python
import functools

import jax
import jax.numpy as jnp
from jax import lax
from jax.experimental import pallas as pl
from jax.experimental.pallas import tpu as pltpu

F32 = jnp.float32
BF16 = jnp.bfloat16

HEAD_DIM = 64
N_HEADS = 8
W_BRANCH = N_HEADS * HEAD_DIM
N_EXPERTS = 8
RMS_EPS = 1e-6
QK_SCALE = 1.0 / 8.0
LANES = 128
HEADS_PER_BLOCK = LANES // HEAD_DIM
V7X_VMEM_LIMIT_BYTES = 56 * 1024 * 1024

ATT_BLOCK = 256
ATT_LANE_BLOCKS = 2
F32_EXP_UNDERFLOW = 106.0
QK_BOUND_MARGIN = 1.05
CUM_BLOCK = 512


def _split_bf16(x, parts):
    out = []
    for _ in range(parts - 1):
        hi = x.astype(BF16)
        out.append(hi)
        x = x - hi.astype(F32)
    out.append(x.astype(BF16))
    return out


def _dot(a, b):
    return jnp.dot(a, b, preferred_element_type=F32)


def _dot_nt(a, b):
    return lax.dot_general(a, b, (((1,), (1,)), ((), ())), preferred_element_type=F32)


def _rmsnorm(x, g):
    ms = jnp.mean(x * x, axis=-1, keepdims=True)
    return x * lax.rsqrt(ms + RMS_EPS) * g


def _sigmoid(x):
    return 1.0 / (1.0 + jnp.exp(-x))


def _compiler_params(semantics):
    return pltpu.CompilerParams(dimension_semantics=semantics,
                                vmem_limit_bytes=V7X_VMEM_LIMIT_BYTES)


def _proj_kernel(x_ref, g_ref, w_ref, gq_ref, gk_ref, sb_ref, fx_ref, f_ref):
    h = _rmsnorm(x_ref[...], g_ref[...]).astype(BF16)
    wb = W_BRANCH
    r = lax.broadcasted_iota(jnp.int32, (wb, wb), 0) // HEAD_DIM
    c = lax.broadcasted_iota(jnp.int32, (wb, wb), 1) // HEAD_DIM
    head_mean = jnp.where(r == c, 1.0 / HEAD_DIM, 0.0).astype(BF16)
    for i in range(3):
        acc = _dot(h, w_ref[:, i * wb:(i + 1) * wb])
        if i == 0:
            acc = acc * QK_SCALE
        sb_ref[:, i * wb:(i + 1) * wb] = acc.astype(BF16)
    for i, gain_ref in enumerate((gq_ref, gk_ref, None)):
        acc = _dot(h, w_ref[:, (3 + i) * wb:(4 + i) * wb])
        if gain_ref is not None:
            sq_hi, sq_lo = _split_bf16(acc * acc, 2)
            ms = _dot(sq_hi, head_mean) + _dot(sq_lo, head_mean)
            acc = acc * lax.rsqrt(ms + RMS_EPS) * gain_ref[...]
        if i == 0:
            acc = acc * QK_SCALE
        fx_ref[:, i * wb:(i + 1) * wb] = acc.astype(BF16)
    f_ref[...] = _dot(h, w_ref[:, 6 * wb:6 * wb + LANES])


def _proj(x, g, w, gq, gk, tm):
    t, d = x.shape
    n = w.shape[1]
    full = lambda i: (0, 0)
    return pl.pallas_call(
        _proj_kernel,
        grid=(t // tm,),
        in_specs=[
            pl.BlockSpec((tm, d), lambda i: (i, 0)),
            pl.BlockSpec((1, d), full),
            pl.BlockSpec((d, n), full),
            pl.BlockSpec((1, W_BRANCH), full),
            pl.BlockSpec((1, W_BRANCH), full),
        ],
        out_specs=[
            pl.BlockSpec((tm, 3 * W_BRANCH), lambda i: (i, 0)),
            pl.BlockSpec((tm, 3 * W_BRANCH), lambda i: (i, 0)),
            pl.BlockSpec((tm, LANES), lambda i: (i, 0)),
        ],
        out_shape=[
            jax.ShapeDtypeStruct((t, 3 * W_BRANCH), BF16),
            jax.ShapeDtypeStruct((t, 3 * W_BRANCH), BF16),
            jax.ShapeDtypeStruct((t, LANES), F32),
        ],
        compiler_params=_compiler_params(("parallel",)),
        name="proj",
    )(x, g, w, gq, gk)


def _cumf_kernel(f_ref, b_ref, o_ref):
    seq = f_ref.shape[1]
    cb = min(CUM_BLOCK, seq)
    r = lax.broadcasted_iota(jnp.int32, (cb, cb), 0)
    c = lax.broadcasted_iota(jnp.int32, (cb, cb), 1)
    prefix = jnp.where(r <= c, 1.0, 0.0).astype(BF16)
    carry = jnp.zeros((N_HEADS, 1), F32)
    for i in range(seq // cb):
        v = f_ref[0, i * cb:(i + 1) * cb, :] + b_ref[...]
        log_f = jnp.minimum(v, 0.0) - jnp.log1p(jnp.exp(-jnp.abs(v)))
        log_f = log_f.T[:N_HEADS, :]
        cum = carry
        for part in _split_bf16(log_f, 3):
            cum = cum + _dot(part, prefix)
        o_ref[0, :, i * cb:(i + 1) * cb] = cum
        carry = cum[:, cb - 1:cb]


def _cumf(f_pre, b_f):
    b, s, _ = f_pre.shape
    return pl.pallas_call(
        _cumf_kernel,
        grid=(b,),
        in_specs=[
            pl.BlockSpec((1, s, LANES), lambda i: (i, 0, 0)),
            pl.BlockSpec((1, LANES), lambda i: (0, 0)),
        ],
        out_specs=pl.BlockSpec((1, N_HEADS, s), lambda i: (i, 0, 0)),
        out_shape=jax.ShapeDtypeStruct((b, N_HEADS, s), F32),
        compiler_params=_compiler_params(("parallel",)),
        name="cumf",
    )(f_pre, b_f)


def _head_queries(q_ref, half):
    q = q_ref[0, :, half * LANES:(half + 1) * LANES].astype(F32)
    lane = lax.broadcasted_iota(jnp.int32, (1, LANES), 1)
    return [jnp.where((lane >= hh * HEAD_DIM) & (lane < (hh + 1) * HEAD_DIM), q, 0.0).astype(BF16)
            for hh in range(HEADS_PER_BLOCK)]


def _run_chains(q_ref, k_ref, v_ref, o_ref, stages, finish, stop):
    bk = ATT_BLOCK
    halves = range(q_ref.shape[2] // LANES)
    qi = pl.program_id(2)
    queries = [q for half in halves for q in _head_queries(q_ref, half)]

    def advance(kb, masked, states):
        start = pl.multiple_of(kb * bk, bk)
        kv = [(k_ref[0, pl.ds(start, bk), half * LANES:(half + 1) * LANES],
               v_ref[0, pl.ds(start, bk), half * LANES:(half + 1) * LANES]) for half in halves]
        args = [(head, q, *kv[head // HEADS_PER_BLOCK], start, masked)
                for head, q in enumerate(queries)]
        tmps = [None] * len(args)
        for stage in stages[:-1]:
            tmps = [stage(*a, tmp) for a, tmp in zip(args, tmps)]
        return tuple(stages[-1](*a, tmp, state) for a, tmp, state in zip(args, tmps, states))

    def body(carry):
        i, _, states = carry
        kb = qi - 1 - i
        states = advance(kb, False, states)
        return i + 1, stop(states, jnp.maximum(kb - 1, 0)).astype(jnp.int32), states

    states = advance(qi, True, [None] * len(queries))
    _, _, states = lax.while_loop(lambda c: jnp.logical_and(c[0] < qi, c[1] == 0), body,
                                  (jnp.int32(0), jnp.int32(0), states))
    lane = lax.broadcasted_iota(jnp.int32, (1, LANES), 1)
    for half in halves:
        outs = [finish(head, states[head])
                for head in range(half * HEADS_PER_BLOCK, (half + 1) * HEADS_PER_BLOCK)]
        o_ref[0, :, half * LANES:(half + 1) * LANES] = jnp.where(
            lane < HEAD_DIM, outs[0], outs[1]).astype(BF16)


def _sb_kernel(q_ref, k_ref, v_ref, o_ref):
    bk = ATT_BLOCK
    row = lax.broadcasted_iota(jnp.int32, (2 * bk, bk), 0)
    col = lax.broadcasted_iota(jnp.int32, (2 * bk, bk), 1)
    suffix2 = jnp.where((row >= col) & ((row < bk) | (row - bk >= col)), 1.0, 0.0).astype(BF16)
    below_diag = (lax.broadcasted_iota(jnp.int32, (bk, bk), 1)
                  < lax.broadcasted_iota(jnp.int32, (bk, bk), 0))

    def scores(head, q, k, v, start, masked, tmp):
        z = _dot_nt(q, k)
        sp = jnp.maximum(z, 0.0) + jnp.log(1.0 + jnp.exp(-jnp.abs(z)))
        if masked:
            sp = jnp.where(below_diag, sp, 0.0)
        return z, jnp.concatenate(_split_bf16(sp, 2), axis=1)

    def suffix_sums(head, q, k, v, start, masked, tmp):
        z, sp_parts = tmp
        return z, _dot(sp_parts, suffix2)

    def weights(head, q, k, v, start, masked, tmp, state):
        z, incl = tmp
        arg = z - incl
        if state is not None:
            arg = arg - state[0]
        w = jnp.exp(arg)
        if masked:
            w = jnp.where(below_diag, w, 0.0)
        pv = _dot(w.astype(BF16), v)
        if state is None:
            return incl[:, 0:1], pv
        return state[0] + incl[:, 0:1], state[1] + pv

    def stop(states, next_block):
        carry = functools.reduce(jnp.minimum, [state[0] for state in states])
        return jnp.min(carry) > F32_EXP_UNDERFLOW

    _run_chains(q_ref, k_ref, v_ref, o_ref, (scores, suffix_sums, weights),
                lambda head, state: state[1], stop)


def _fox_kernel(zmax_ref, fend_ref, q_ref, k_ref, v_ref, f_ref, o_ref):
    bk = ATT_BLOCK
    n_blocks = k_ref.shape[1] // bk
    n_heads = f_ref.shape[2]
    row = lax.broadcasted_iota(jnp.int32, (bk, bk), 0)
    col = lax.broadcasted_iota(jnp.int32, (bk, bk), 1)
    causal = col <= row
    lane = lax.broadcasted_iota(jnp.int32, (1, LANES), 1)
    values = {}

    def values_with_ones(hh, v):
        if (hh, id(v)) not in values:
            own = (lane >= hh * HEAD_DIM) & (lane < (hh + 1) * HEAD_DIM)
            values[(hh, id(v))] = (v, jnp.where(own, v.astype(F32), 1.0).astype(BF16))
        return values[(hh, id(v))][1]

    def logits(head, q, k, v, start, masked, tmp):
        s = _dot_nt(q, k) - f_ref[0, 0, head:head + 1, pl.ds(start, bk)]
        if masked:
            s = jnp.where(causal, s, -jnp.inf)
        return s, jnp.max(s, axis=1, keepdims=True)

    def accumulate(head, q, k, v, start, masked, tmp, state):
        s, m_blk = tmp
        v_ones = values_with_ones(head % HEADS_PER_BLOCK, v)
        if state is None:
            return m_blk, _dot(jnp.exp(s - m_blk).astype(BF16), v_ones)
        m, acc = state
        m_new = jnp.maximum(m, m_blk)
        pv = _dot(jnp.exp(s - m_new).astype(BF16), v_ones)
        return m_new, jnp.exp(m - m_new) * acc + pv

    def finish(head, state):
        acc = state[1]
        ones_lane = (1 - head % HEADS_PER_BLOCK) * HEAD_DIM
        return acc / acc[:, ones_lane:ones_lane + 1]

    def stop(states, next_block):
        base = ((pl.program_id(0) * (W_BRANCH // HEAD_DIM) + pl.program_id(1) * n_heads) * n_blocks
                + next_block)
        slack = functools.reduce(jnp.minimum, [state[0] - fend_ref[base + head * n_blocks]
                                               for head, state in enumerate(states)])
        return jnp.min(slack) > zmax_ref[0] + F32_EXP_UNDERFLOW

    _run_chains(q_ref, k_ref, v_ref, o_ref, (logits, accumulate), finish, stop)


def _attention(kernel, qkv, cum_f=None, zmax=None):
    b, s, _ = qkv.shape
    bq = min(ATT_BLOCK, s)
    width = ATT_LANE_BLOCKS * LANES
    n_groups = W_BRANCH // width
    in_specs = [
        pl.BlockSpec((1, bq, width), lambda bi, g, qi: (bi, qi, g)),
        pl.BlockSpec((1, s, width), lambda bi, g, qi: (bi, 0, n_groups + g)),
        pl.BlockSpec((1, s, width), lambda bi, g, qi: (bi, 0, 2 * n_groups + g)),
    ]
    args = [qkv, qkv, qkv]
    if cum_f is not None:
        heads = ATT_LANE_BLOCKS * HEADS_PER_BLOCK
        smem = pl.BlockSpec(memory_space=pltpu.SMEM)
        in_specs = [smem, smem] + in_specs + [
            pl.BlockSpec((1, 1, heads, s), lambda bi, g, qi: (bi, g, 0, 0))]
        f_end = -cum_f[:, :, bq - 1::bq]
        args = [zmax, f_end.reshape(-1)] + args + [cum_f.reshape(b, n_groups, heads, s)]
    return pl.pallas_call(
        kernel,
        grid=(b, n_groups, s // bq),
        in_specs=in_specs,
        out_specs=pl.BlockSpec((1, bq, width), lambda bi, g, qi: (bi, qi, g)),
        out_shape=jax.ShapeDtypeStruct((b, s, W_BRANCH), BF16),
        compiler_params=_compiler_params(("parallel", "parallel", "arbitrary")),
        name=kernel.__name__.strip("_"),
    )(*args)


def _top2_combine(logits):
    lane = lax.broadcasted_iota(jnp.int32, logits.shape, 1).astype(F32)
    lg = jnp.where(lane < N_EXPERTS, logits, -jnp.inf)
    m1 = jnp.max(lg, axis=1, keepdims=True)
    i1 = jnp.min(jnp.where(lg == m1, lane, float(LANES)), axis=1, keepdims=True)
    lg2 = jnp.where(lane == i1, -jnp.inf, lg)
    m2 = jnp.max(lg2, axis=1, keepdims=True)
    i2 = jnp.min(jnp.where(lg2 == m2, lane, float(LANES)), axis=1, keepdims=True)
    e2 = jnp.exp(m2 - m1)
    w1 = 1.0 / (1.0 + e2)
    return jnp.where(lane == i1, w1, 0.0) + jnp.where(lane == i2, e2 * w1, 0.0)


def _merge_kernel(x_ref, ysb_ref, yfx_ref, gmix_ref, wgate_ref, bgate_ref, wosb_ref, wofx_ref,
                  wout_ref, gffn_ref, *rest, with_router):
    if with_router:
        wr_hi_ref, wr_lo_ref, x1_ref, h2_ref, comb_ref = rest
    else:
        x1_ref, h2_ref = rest
    d = x_ref.shape[1]
    x = x_ref[...]
    h = _rmsnorm(x, gmix_ref[...]).astype(BF16)
    merged = None
    for i, (y_ref, wo_ref) in enumerate(((ysb_ref, wosb_ref), (yfx_ref, wofx_ref))):
        gate = _sigmoid(_dot(h, wgate_ref[:, i * d:(i + 1) * d]) + bgate_ref[:, i * d:(i + 1) * d])
        branch = gate * _dot(y_ref[...], wo_ref[...])
        merged = branch if merged is None else merged + branch
    x1 = x + _dot(merged.astype(BF16), wout_ref[...])
    x1_ref[...] = x1
    h2 = _rmsnorm(x1, gffn_ref[...])
    h2_ref[...] = h2.astype(BF16)
    if with_router:
        h_hi, h_lo = _split_bf16(h2, 2)
        logits = _dot(h_hi, wr_hi_ref[...]) + _dot(h_lo, wr_hi_ref[...]) + _dot(h_hi, wr_lo_ref[...])
        comb_ref[...] = _top2_combine(logits)


def _merge(x, y_sb, y_fx, g_mix, w_gate, b_gate, w_o_sb, w_o_fox, w_out, g_ffn, w_router, tm):
    t, d = x.shape
    with_router = w_router is not None
    full = lambda i: (0, 0)
    tile = lambda i: (i, 0)
    in_specs = [
        pl.BlockSpec((tm, d), tile),
        pl.BlockSpec((tm, W_BRANCH), tile),
        pl.BlockSpec((tm, W_BRANCH), tile),
        pl.BlockSpec((1, d), full),
        pl.BlockSpec((d, 2 * d), full),
        pl.BlockSpec((1, 2 * d), full),
        pl.BlockSpec((W_BRANCH, d), full),
        pl.BlockSpec((W_BRANCH, d), full),
        pl.BlockSpec((d, d), full),
        pl.BlockSpec((1, d), full),
    ]
    args = [x, y_sb, y_fx, g_mix, w_gate, b_gate, w_o_sb, w_o_fox, w_out, g_ffn]
    out_specs = [pl.BlockSpec((tm, d), tile), pl.BlockSpec((tm, d), tile)]
    out_shape = [jax.ShapeDtypeStruct((t, d), F32), jax.ShapeDtypeStruct((t, d), BF16)]
    if with_router:
        in_specs += [pl.BlockSpec((d, LANES), full)] * 2
        args += list(w_router)
        out_specs.append(pl.BlockSpec((tm, LANES), tile))
        out_shape.append(jax.ShapeDtypeStruct((t, LANES), F32))
    return pl.pallas_call(
        functools.partial(_merge_kernel, with_router=with_router),
        grid=(t // tm,),
        in_specs=in_specs,
        out_specs=out_specs,
        out_shape=out_shape,
        compiler_params=_compiler_params(("parallel",)),
        name="merge_router" if with_router else "merge",
    )(*args)


def _swiglu_chunk(h, wg, wu, wd):
    g = _dot(h, wg)
    u = _dot(h, wu)
    return _dot((g * _sigmoid(g) * u).astype(BF16), wd)


def _ffn_kernel(h_ref, x_ref, wg_ref, wu_ref, wd_ref, o_ref):
    @pl.when(pl.program_id(1) == 0)
    def _():
        o_ref[...] = x_ref[...]

    o_ref[...] += _swiglu_chunk(h_ref[...], wg_ref[...], wu_ref[...], wd_ref[...])


def _ffn(h, x, w_gu, w_dn, tm, tf):
    t, d = x.shape
    d_ff = w_dn.shape[0]
    nf = d_ff // tf
    return pl.pallas_call(
        _ffn_kernel,
        grid=(t // tm, nf),
        in_specs=[
            pl.BlockSpec((tm, d), lambda i, j: (i, 0)),
            pl.BlockSpec((tm, d), lambda i, j: (i, 0)),
            pl.BlockSpec((d, tf), lambda i, j: (0, j)),
            pl.BlockSpec((d, tf), lambda i, j: (0, nf + j)),
            pl.BlockSpec((tf, d), lambda i, j: (j, 0)),
        ],
        out_specs=pl.BlockSpec((tm, d), lambda i, j: (i, 0)),
        out_shape=jax.ShapeDtypeStruct((t, d), F32),
        compiler_params=_compiler_params(("parallel", "arbitrary")),
        name="ffn",
    )(h, x, w_gu, w_gu, w_dn)


def _moe_kernel(h_ref, x_ref, comb_ref, wg_ref, wu_ref, wd_ref, o_ref):
    e = pl.program_id(1)

    @pl.when(e == 0)
    def _():
        o_ref[...] = x_ref[...]

    comb = comb_ref[...]
    lane = lax.broadcasted_iota(jnp.int32, comb.shape, 1)
    weight = jnp.sum(jnp.where(lane == e, comb, 0.0), axis=1, keepdims=True)
    o_ref[...] += weight * _swiglu_chunk(h_ref[...], wg_ref[0], wu_ref[0], wd_ref[0])


def _moe(h, x, comb, w_gu, w_dn, tm):
    t, d = x.shape
    n_exp, d_ff, _ = w_dn.shape
    return pl.pallas_call(
        _moe_kernel,
        grid=(t // tm, n_exp),
        in_specs=[
            pl.BlockSpec((tm, d), lambda i, e: (i, 0)),
            pl.BlockSpec((tm, d), lambda i, e: (i, 0)),
            pl.BlockSpec((tm, LANES), lambda i, e: (i, 0)),
            pl.BlockSpec((1, d, d_ff), lambda i, e: (e, 0, 0)),
            pl.BlockSpec((1, d, d_ff), lambda i, e: (e, 0, 1)),
            pl.BlockSpec((1, d_ff, d), lambda i, e: (e, 0, 0)),
        ],
        out_specs=pl.BlockSpec((tm, d), lambda i, e: (i, 0)),
        out_shape=jax.ShapeDtypeStruct((t, d), F32),
        compiler_params=_compiler_params(("parallel", "arbitrary")),
        name="moe",
    )(h, x, comb, w_gu, w_gu, w_dn)


def _pad_lanes(w):
    return jnp.pad(w, ((0, 0), (0, LANES - w.shape[1])))


def kernel(x, g_mix, w_in, b_f, b_gate, g_q, g_k, w_o_sb, w_o_fox, w_out, g_ffn, w_gu_dense,
           w_dn_dense, w_router, w_gu_exp, w_dn_exp):
    batch, seq, d = x.shape
    depth = w_in.shape[0]
    t = batch * seq
    tm_proj = min(512, t)
    tm_ffn = min(1024, t)
    n_qkv = 6 * W_BRANCH
    xt = x.reshape(t, d)
    for l in range(depth):
        w_l = w_in[l]
        w_proj = jnp.concatenate(
            [w_l[:, :n_qkv], _pad_lanes(w_l[:, n_qkv:n_qkv + N_HEADS])], axis=1).astype(BF16)
        w_gate = w_l[:, n_qkv + N_HEADS:].astype(BF16)
        row = lambda v: v.reshape(1, -1)
        gq = jnp.tile(g_q[l], N_HEADS).reshape(1, W_BRANCH)
        gk = jnp.tile(g_k[l], N_HEADS).reshape(1, W_BRANCH)
        qkv_sb, qkv_fx, f_pre = _proj(xt, row(g_mix[l]), w_proj, gq, gk, tm_proj)
        cum_f = _cumf(f_pre.reshape(batch, seq, LANES), _pad_lanes(row(b_f[l])))
        y_sb = _attention(_sb_kernel, qkv_sb.reshape(batch, seq, 3 * W_BRANCH))
        zmax = (QK_BOUND_MARGIN * 8.0 * jnp.max(jnp.abs(g_q[l])) * jnp.max(jnp.abs(g_k[l]))).reshape(1)
        y_fx = _attention(_fox_kernel, qkv_fx.reshape(batch, seq, 3 * W_BRANCH), cum_f, zmax)
        moe_layer = l % 2 == 1
        router = None
        if moe_layer:
            wr = _pad_lanes(w_router[l // 2])
            router = _split_bf16(wr, 2)
        outs = _merge(xt, y_sb.reshape(t, W_BRANCH), y_fx.reshape(t, W_BRANCH), row(g_mix[l]),
                      w_gate, row(b_gate[l]), w_o_sb[l].astype(BF16), w_o_fox[l].astype(BF16),
                      w_out[l].astype(BF16), row(g_ffn[l]), router, tm_proj)
        if moe_layer:
            x1, h2, comb = outs
            xt = _moe(h2, x1, comb, w_gu_exp[l // 2].astype(BF16), w_dn_exp[l // 2].astype(BF16),
                      tm_proj)
        else:
            x1, h2 = outs
            xt = _ffn(h2, x1, w_gu_dense[l // 2].astype(BF16), w_dn_dense[l // 2].astype(BF16),
                      tm_ffn, 256)
    return xt.reshape(batch, seq, d)
```

```python
import functools

import jax
import jax.numpy as jnp
from jax import lax
from jax.experimental import pallas as pl
from jax.experimental.pallas import tpu as pltpu

F32 = jnp.float32
BF16 = jnp.bfloat16

HEAD_DIM = 64
N_HEADS = 8
W_BRANCH = N_HEADS * HEAD_DIM
N_EXPERTS = 8
RMS_EPS = 1e-6
QK_SCALE = 1.0 / 8.0
LANES = 128
HEADS_PER_BLOCK = LANES // HEAD_DIM
V7X_VMEM_LIMIT_BYTES = 56 * 1024 * 1024

ATT_BLOCK = 256
ATT_LANE_BLOCKS = 4
F32_EXP_UNDERFLOW = 106.0
QK_BOUND_MARGIN = 1.05
CUM_BLOCK = 512


def _split_bf16(x, parts):
    out = []
    for _ in range(parts - 1):
        hi = x.astype(BF16)
        out.append(hi)
        x = x - hi.astype(F32)
    out.append(x.astype(BF16))
    return out


def _dot(a, b):
    return jnp.dot(a, b, preferred_element_type=F32)


def _dot_nt(a, b):
    return lax.dot_general(a, b, (((1,), (1,)), ((), ())), preferred_element_type=F32)


def _rmsnorm(x, g):
    ms = jnp.mean(x * x, axis=-1, keepdims=True)
    return x * lax.rsqrt(ms + RMS_EPS) * g


def _sigmoid(x):
    return 1.0 / (1.0 + jnp.exp(-x))


def _compiler_params(semantics):
    return pltpu.CompilerParams(dimension_semantics=semantics,
                                vmem_limit_bytes=V7X_VMEM_LIMIT_BYTES)


def _proj_kernel(x_ref, g_ref, w_ref, gq_ref, gk_ref, sb_ref, fx_ref, f_ref):
    h = _rmsnorm(x_ref[...], g_ref[...]).astype(BF16)
    wb = W_BRANCH
    r = lax.broadcasted_iota(jnp.int32, (wb, wb), 0) // HEAD_DIM
    c = lax.broadcasted_iota(jnp.int32, (wb, wb), 1) // HEAD_DIM
    head_mean = jnp.where(r == c, 1.0 / HEAD_DIM, 0.0).astype(BF16)
    for i in range(3):
        acc = _dot(h, w_ref[:, i * wb:(i + 1) * wb])
        if i == 0:
            acc = acc * QK_SCALE
        sb_ref[:, i * wb:(i + 1) * wb] = acc.astype(BF16)
    for i, gain_ref in enumerate((gq_ref, gk_ref, None)):
        acc = _dot(h, w_ref[:, (3 + i) * wb:(4 + i) * wb])
        if gain_ref is not None:
            sq_hi, sq_lo = _split_bf16(acc * acc, 2)
            ms = _dot(sq_hi, head_mean) + _dot(sq_lo, head_mean)
            acc = acc * lax.rsqrt(ms + RMS_EPS) * gain_ref[...]
        if i == 0:
            acc = acc * QK_SCALE
        fx_ref[:, i * wb:(i + 1) * wb] = acc.astype(BF16)
    f_ref[...] = _dot(h, w_ref[:, 6 * wb:6 * wb + LANES])


def _proj(x, g, w, gq, gk, tm):
    t, d = x.shape
    n = w.shape[1]
    full = lambda i: (0, 0)
    return pl.pallas_call(
        _proj_kernel,
        grid=(t // tm,),
        in_specs=[
            pl.BlockSpec((tm, d), lambda i: (i, 0)),
            pl.BlockSpec((1, d), full),
            pl.BlockSpec((d, n), full),
            pl.BlockSpec((1, W_BRANCH), full),
            pl.BlockSpec((1, W_BRANCH), full),
        ],
        out_specs=[
            pl.BlockSpec((tm, 3 * W_BRANCH), lambda i: (i, 0)),
            pl.BlockSpec((tm, 3 * W_BRANCH), lambda i: (i, 0)),
            pl.BlockSpec((tm, LANES), lambda i: (i, 0)),
        ],
        out_shape=[
            jax.ShapeDtypeStruct((t, 3 * W_BRANCH), BF16),
            jax.ShapeDtypeStruct((t, 3 * W_BRANCH), BF16),
            jax.ShapeDtypeStruct((t, LANES), F32),
        ],
        compiler_params=_compiler_params(("parallel",)),
        name="proj",
    )(x, g, w, gq, gk)


def _cumf_kernel(f_ref, b_ref, o_ref):
    seq = f_ref.shape[1]
    cb = min(CUM_BLOCK, seq)
    r = lax.broadcasted_iota(jnp.int32, (cb, cb), 0)
    c = lax.broadcasted_iota(jnp.int32, (cb, cb), 1)
    prefix = jnp.where(r <= c, 1.0, 0.0).astype(BF16)
    carry = jnp.zeros((N_HEADS, 1), F32)
    for i in range(seq // cb):
        v = f_ref[0, i * cb:(i + 1) * cb, :] + b_ref[...]
        log_f = jnp.minimum(v, 0.0) - jnp.log1p(jnp.exp(-jnp.abs(v)))
        log_f = log_f.T[:N_HEADS, :]
        cum = carry
        for part in _split_bf16(log_f, 3):
            cum = cum + _dot(part, prefix)
        o_ref[0, :, i * cb:(i + 1) * cb] = cum
        carry = cum[:, cb - 1:cb]


def _cumf(f_pre, b_f):
    b, s, _ = f_pre.shape
    return pl.pallas_call(
        _cumf_kernel,
        grid=(b,),
        in_specs=[
            pl.BlockSpec((1, s, LANES), lambda i: (i, 0, 0)),
            pl.BlockSpec((1, LANES), lambda i: (0, 0)),
        ],
        out_specs=pl.BlockSpec((1, N_HEADS, s), lambda i: (i, 0, 0)),
        out_shape=jax.ShapeDtypeStruct((b, N_HEADS, s), F32),
        compiler_params=_compiler_params(("parallel",)),
        name="cumf",
    )(f_pre, b_f)


def _head_queries(q_ref, half):
    q = q_ref[0, :, half * LANES:(half + 1) * LANES].astype(F32)
    lane = lax.broadcasted_iota(jnp.int32, (1, LANES), 1)
    return [jnp.where((lane >= hh * HEAD_DIM) & (lane < (hh + 1) * HEAD_DIM), q, 0.0).astype(BF16)
            for hh in range(HEADS_PER_BLOCK)]


def _run_chains(q_ref, k_ref, v_ref, o_ref, stages, finish, stop):
    bk = ATT_BLOCK
    halves = range(q_ref.shape[2] // LANES)
    qi = pl.program_id(2)
    queries = [q for half in halves for q in _head_queries(q_ref, half)]

    def advance(kb, masked, states):
        start = pl.multiple_of(kb * bk, bk)
        kv = [(k_ref[0, pl.ds(start, bk), half * LANES:(half + 1) * LANES],
               v_ref[0, pl.ds(start, bk), half * LANES:(half + 1) * LANES]) for half in halves]
        args = [(head, q, *kv[head // HEADS_PER_BLOCK], start, masked)
                for head, q in enumerate(queries)]
        tmps = [None] * len(args)
        for stage in stages[:-1]:
            tmps = [stage(*a, tmp) for a, tmp in zip(args, tmps)]
        return tuple(stages[-1](*a, tmp, state) for a, tmp, state in zip(args, tmps, states))

    def body(carry):
        i, _, states = carry
        kb = qi - 1 - i
        states = advance(kb, False, states)
        return i + 1, stop(states, jnp.maximum(kb - 1, 0)).astype(jnp.int32), states

    states = advance(qi, True, [None] * len(queries))
    _, _, states = lax.while_loop(lambda c: jnp.logical_and(c[0] < qi, c[1] == 0), body,
                                  (jnp.int32(0), jnp.int32(0), states))
    lane = lax.broadcasted_iota(jnp.int32, (1, LANES), 1)
    for half in halves:
        outs = [finish(head, states[head])
                for head in range(half * HEADS_PER_BLOCK, (half + 1) * HEADS_PER_BLOCK)]
        o_ref[0, :, half * LANES:(half + 1) * LANES] = jnp.where(
            lane < HEAD_DIM, outs[0], outs[1]).astype(BF16)


def _sb_kernel(q_ref, k_ref, v_ref, o_ref):
    bk = ATT_BLOCK
    row = lax.broadcasted_iota(jnp.int32, (2 * bk, bk), 0)
    col = lax.broadcasted_iota(jnp.int32, (2 * bk, bk), 1)
    suffix2 = jnp.where((row >= col) & ((row < bk) | (row - bk >= col)), 1.0, 0.0).astype(BF16)
    below_diag = (lax.broadcasted_iota(jnp.int32, (bk, bk), 1)
                  < lax.broadcasted_iota(jnp.int32, (bk, bk), 0))

    def scores(head, q, k, v, start, masked, tmp):
        z = _dot_nt(q, k)
        neg_abs = lax.bitcast_convert_type(
            lax.bitcast_convert_type(z, jnp.uint32) | jnp.uint32(0x80000000), F32)
        sp = jnp.maximum(z, 0.0) + jnp.log(1.0 + jnp.exp(neg_abs))
        if masked:
            sp = jnp.where(below_diag, sp, 0.0)
        return z, jnp.concatenate(_split_bf16(sp, 2), axis=1)

    def suffix_sums(head, q, k, v, start, masked, tmp):
        z, sp_parts = tmp
        return z, _dot(sp_parts, suffix2)

    def weights(head, q, k, v, start, masked, tmp, state):
        z, incl = tmp
        arg = z - incl
        if state is not None:
            arg = arg - state[0]
        w = jnp.exp(arg)
        if masked:
            w = jnp.where(below_diag, w, 0.0)
        pv = _dot(w.astype(BF16), v)
        if state is None:
            return incl[:, 0:1], pv
        return state[0] + incl[:, 0:1], state[1] + pv

    def stop(states, next_block):
        carry = functools.reduce(jnp.minimum, [state[0] for state in states])
        return jnp.min(carry) > F32_EXP_UNDERFLOW

    _run_chains(q_ref, k_ref, v_ref, o_ref, (scores, suffix_sums, weights),
                lambda head, state: state[1], stop)


def _fox_kernel(zmax_ref, fend_ref, q_ref, k_ref, v_ref, f_ref, o_ref):
    bk = ATT_BLOCK
    n_blocks = k_ref.shape[1] // bk
    n_heads = f_ref.shape[2]
    row = lax.broadcasted_iota(jnp.int32, (bk, bk), 0)
    col = lax.broadcasted_iota(jnp.int32, (bk, bk), 1)
    causal = col <= row
    lane = lax.broadcasted_iota(jnp.int32, (1, LANES), 1)
    values = {}

    def values_with_ones(hh, v):
        if (hh, id(v)) not in values:
            own = (lane >= hh * HEAD_DIM) & (lane < (hh + 1) * HEAD_DIM)
            values[(hh, id(v))] = (v, jnp.where(own, v.astype(F32), 1.0).astype(BF16))
        return values[(hh, id(v))][1]

    def logits(head, q, k, v, start, masked, tmp):
        s = _dot_nt(q, k) - f_ref[0, 0, head:head + 1, pl.ds(start, bk)]
        if masked:
            s = jnp.where(causal, s, -jnp.inf)
        return s, jnp.max(s, axis=1, keepdims=True)

    def accumulate(head, q, k, v, start, masked, tmp, state):
        s, m_blk = tmp
        v_ones = values_with_ones(head % HEADS_PER_BLOCK, v)
        if state is None:
            return m_blk, _dot(jnp.exp(s - m_blk).astype(BF16), v_ones)
        m, acc = state
        m_new = jnp.maximum(m, m_blk)
        pv = _dot(jnp.exp(s - m_new).astype(BF16), v_ones)
        return m_new, jnp.exp(m - m_new) * acc + pv

    def finish(head, state):
        acc = state[1]
        ones_lane = (1 - head % HEADS_PER_BLOCK) * HEAD_DIM
        return acc / acc[:, ones_lane:ones_lane + 1]

    def stop(states, next_block):
        base = ((pl.program_id(0) * (W_BRANCH // HEAD_DIM) + pl.program_id(1) * n_heads) * n_blocks
                + next_block)
        slack = functools.reduce(jnp.minimum, [state[0] - fend_ref[base + head * n_blocks]
                                               for head, state in enumerate(states)])
        return jnp.min(slack) > zmax_ref[0] + F32_EXP_UNDERFLOW

    _run_chains(q_ref, k_ref, v_ref, o_ref, (logits, accumulate), finish, stop)


def _attention(kernel, qkv, cum_f=None, zmax=None):
    b, s, _ = qkv.shape
    bq = min(ATT_BLOCK, s)
    width = ATT_LANE_BLOCKS * LANES
    n_groups = W_BRANCH // width
    in_specs = [
        pl.BlockSpec((1, bq, width), lambda bi, g, qi: (bi, qi, g)),
        pl.BlockSpec((1, s, width), lambda bi, g, qi: (bi, 0, n_groups + g)),
        pl.BlockSpec((1, s, width), lambda bi, g, qi: (bi, 0, 2 * n_groups + g)),
    ]
    args = [qkv, qkv, qkv]
    if cum_f is not None:
        heads = ATT_LANE_BLOCKS * HEADS_PER_BLOCK
        smem = pl.BlockSpec(memory_space=pltpu.SMEM)
        in_specs = [smem, smem] + in_specs + [
            pl.BlockSpec((1, 1, heads, s), lambda bi, g, qi: (bi, g, 0, 0))]
        f_end = -cum_f[:, :, bq - 1::bq]
        args = [zmax, f_end.reshape(-1)] + args + [cum_f.reshape(b, n_groups, heads, s)]
    return pl.pallas_call(
        kernel,
        grid=(b, n_groups, s // bq),
        in_specs=in_specs,
        out_specs=pl.BlockSpec((1, bq, width), lambda bi, g, qi: (bi, qi, g)),
        out_shape=jax.ShapeDtypeStruct((b, s, W_BRANCH), BF16),
        compiler_params=_compiler_params(("parallel", "parallel", "arbitrary")),
        name=kernel.__name__.strip("_"),
    )(*args)


def _top2_combine(logits):
    lane = lax.broadcasted_iota(jnp.int32, logits.shape, 1).astype(F32)
    lg = jnp.where(lane < N_EXPERTS, logits, -jnp.inf)
    m1 = jnp.max(lg, axis=1, keepdims=True)
    i1 = jnp.min(jnp.where(lg == m1, lane, float(LANES)), axis=1, keepdims=True)
    lg2 = jnp.where(lane == i1, -jnp.inf, lg)
    m2 = jnp.max(lg2, axis=1, keepdims=True)
    i2 = jnp.min(jnp.where(lg2 == m2, lane, float(LANES)), axis=1, keepdims=True)
    e2 = jnp.exp(m2 - m1)
    w1 = 1.0 / (1.0 + e2)
    return jnp.where(lane == i1, w1, 0.0) + jnp.where(lane == i2, e2 * w1, 0.0)


def _merge_kernel(x_ref, ysb_ref, yfx_ref, gmix_ref, wgate_ref, bgate_ref, wosb_ref, wofx_ref,
                  wout_ref, gffn_ref, *rest, with_router):
    if with_router:
        wr_hi_ref, wr_lo_ref, x1_ref, h2_ref, comb_ref = rest
    else:
        x1_ref, h2_ref = rest
    d = x_ref.shape[1]
    x = x_ref[...]
    h = _rmsnorm(x, gmix_ref[...]).astype(BF16)
    merged = None
    for i, (y_ref, wo_ref) in enumerate(((ysb_ref, wosb_ref), (yfx_ref, wofx_ref))):
        gate = _sigmoid(_dot(h, wgate_ref[:, i * d:(i + 1) * d]) + bgate_ref[:, i * d:(i + 1) * d])
        branch = gate * _dot(y_ref[...], wo_ref[...])
        merged = branch if merged is None else merged + branch
    x1 = x + _dot(merged.astype(BF16), wout_ref[...])
    x1_ref[...] = x1
    h2 = _rmsnorm(x1, gffn_ref[...])
    h2_ref[...] = h2.astype(BF16)
    if with_router:
        h_hi, h_lo = _split_bf16(h2, 2)
        logits = _dot(h_hi, wr_hi_ref[...]) + _dot(h_lo, wr_hi_ref[...]) + _dot(h_hi, wr_lo_ref[...])
        comb_ref[...] = _top2_combine(logits)


def _merge(x, y_sb, y_fx, g_mix, w_gate, b_gate, w_o_sb, w_o_fox, w_out, g_ffn, w_router, tm):
    t, d = x.shape
    with_router = w_router is not None
    full = lambda i: (0, 0)
    tile = lambda i: (i, 0)
    in_specs = [
        pl.BlockSpec((tm, d), tile),
        pl.BlockSpec((tm, W_BRANCH), tile),
        pl.BlockSpec((tm, W_BRANCH), tile),
        pl.BlockSpec((1, d), full),
        pl.BlockSpec((d, 2 * d), full),
        pl.BlockSpec((1, 2 * d), full),
        pl.BlockSpec((W_BRANCH, d), full),
        pl.BlockSpec((W_BRANCH, d), full),
        pl.BlockSpec((d, d), full),
        pl.BlockSpec((1, d), full),
    ]
    args = [x, y_sb, y_fx, g_mix, w_gate, b_gate, w_o_sb, w_o_fox, w_out, g_ffn]
    out_specs = [pl.BlockSpec((tm, d), tile), pl.BlockSpec((tm, d), tile)]
    out_shape = [jax.ShapeDtypeStruct((t, d), F32), jax.ShapeDtypeStruct((t, d), BF16)]
    if with_router:
        in_specs += [pl.BlockSpec((d, LANES), full)] * 2
        args += list(w_router)
        out_specs.append(pl.BlockSpec((tm, LANES), tile))
        out_shape.append(jax.ShapeDtypeStruct((t, LANES), F32))
    return pl.pallas_call(
        functools.partial(_merge_kernel, with_router=with_router),
        grid=(t // tm,),
        in_specs=in_specs,
        out_specs=out_specs,
        out_shape=out_shape,
        compiler_params=_compiler_params(("parallel",)),
        name="merge_router" if with_router else "merge",
    )(*args)


def _swiglu_chunk(h, wg, wu, wd):
    g = _dot(h, wg)
    u = _dot(h, wu)
    return _dot((g * _sigmoid(g) * u).astype(BF16), wd)


def _ffn_kernel(h_ref, x_ref, wg_ref, wu_ref, wd_ref, o_ref):
    @pl.when(pl.program_id(1) == 0)
    def _():
        o_ref[...] = x_ref[...]

    o_ref[...] += _swiglu_chunk(h_ref[...], wg_ref[...], wu_ref[...], wd_ref[...])


def _ffn(h, x, w_gu, w_dn, tm, tf):
    t, d = x.shape
    d_ff = w_dn.shape[0]
    nf = d_ff // tf
    return pl.pallas_call(
        _ffn_kernel,
        grid=(t // tm, nf),
        in_specs=[
            pl.BlockSpec((tm, d), lambda i, j: (i, 0)),
            pl.BlockSpec((tm, d), lambda i, j: (i, 0)),
            pl.BlockSpec((d, tf), lambda i, j: (0, j)),
            pl.BlockSpec((d, tf), lambda i, j: (0, nf + j)),
            pl.BlockSpec((tf, d), lambda i, j: (j, 0)),
        ],
        out_specs=pl.BlockSpec((tm, d), lambda i, j: (i, 0)),
        out_shape=jax.ShapeDtypeStruct((t, d), F32),
        compiler_params=_compiler_params(("parallel", "arbitrary")),
        name="ffn",
    )(h, x, w_gu, w_gu, w_dn)


def _moe_kernel(h_ref, x_ref, comb_ref, wg_ref, wu_ref, wd_ref, o_ref):
    e = pl.program_id(1)

    @pl.when(e == 0)
    def _():
        o_ref[...] = x_ref[...]

    comb = comb_ref[...]
    lane = lax.broadcasted_iota(jnp.int32, comb.shape, 1)
    weight = jnp.sum(jnp.where(lane == e, comb, 0.0), axis=1, keepdims=True)
    o_ref[...] += weight * _swiglu_chunk(h_ref[...], wg_ref[0], wu_ref[0], wd_ref[0])


def _moe(h, x, comb, w_gu, w_dn, tm):
    t, d = x.shape
    n_exp, d_ff, _ = w_dn.shape
    return pl.pallas_call(
        _moe_kernel,
        grid=(t // tm, n_exp),
        in_specs=[
            pl.BlockSpec((tm, d), lambda i, e: (i, 0)),
            pl.BlockSpec((tm, d), lambda i, e: (i, 0)),
            pl.BlockSpec((tm, LANES), lambda i, e: (i, 0)),
            pl.BlockSpec((1, d, d_ff), lambda i, e: (e, 0, 0)),
            pl.BlockSpec((1, d, d_ff), lambda i, e: (e, 0, 1)),
            pl.BlockSpec((1, d_ff, d), lambda i, e: (e, 0, 0)),
        ],
        out_specs=pl.BlockSpec((tm, d), lambda i, e: (i, 0)),
        out_shape=jax.ShapeDtypeStruct((t, d), F32),
        compiler_params=_compiler_params(("parallel", "arbitrary")),
        name="moe",
    )(h, x, comb, w_gu, w_gu, w_dn)


def _pad_lanes(w):
    return jnp.pad(w, ((0, 0), (0, LANES - w.shape[1])))


def kernel(x, g_mix, w_in, b_f, b_gate, g_q, g_k, w_o_sb, w_o_fox, w_out, g_ffn, w_gu_dense,
           w_dn_dense, w_router, w_gu_exp, w_dn_exp):
    batch, seq, d = x.shape
    depth = w_in.shape[0]
    t = batch * seq
    tm_proj = min(512, t)
    tm_ffn = min(1024, t)
    n_qkv = 6 * W_BRANCH
    xt = x.reshape(t, d)
    for l in range(depth):
        w_l = w_in[l]
        w_proj = jnp.concatenate(
            [w_l[:, :n_qkv], _pad_lanes(w_l[:, n_qkv:n_qkv + N_HEADS])], axis=1).astype(BF16)
        w_gate = w_l[:, n_qkv + N_HEADS:].astype(BF16)
        row = lambda v: v.reshape(1, -1)
        gq = jnp.tile(g_q[l], N_HEADS).reshape(1, W_BRANCH)
        gk = jnp.tile(g_k[l], N_HEADS).reshape(1, W_BRANCH)
        qkv_sb, qkv_fx, f_pre = _proj(xt, row(g_mix[l]), w_proj, gq, gk, tm_proj)
        cum_f = _cumf(f_pre.reshape(batch, seq, LANES), _pad_lanes(row(b_f[l])))
        y_sb = _attention(_sb_kernel, qkv_sb.reshape(batch, seq, 3 * W_BRANCH))
        zmax = (QK_BOUND_MARGIN * 8.0 * jnp.max(jnp.abs(g_q[l])) * jnp.max(jnp.abs(g_k[l]))).reshape(1)
        y_fx = _attention(_fox_kernel, qkv_fx.reshape(batch, seq, 3 * W_BRANCH), cum_f, zmax)
        moe_layer = l % 2 == 1
        router = None
        if moe_layer:
            wr = _pad_lanes(w_router[l // 2])
            router = _split_bf16(wr, 2)
        outs = _merge(xt, y_sb.reshape(t, W_BRANCH), y_fx.reshape(t, W_BRANCH), row(g_mix[l]),
                      w_gate, row(b_gate[l]), w_o_sb[l].astype(BF16), w_o_fox[l].astype(BF16),
                      w_out[l].astype(BF16), row(g_ffn[l]), router, tm_proj)
        if moe_layer:
            x1, h2, comb = outs
            xt = _moe(h2, x1, comb, w_gu_exp[l // 2].astype(BF16), w_dn_exp[l // 2].astype(BF16),
                      tm_proj)
        else:
            x1, h2 = outs
            xt = _ffn(h2, x1, w_gu_dense[l // 2].astype(BF16), w_dn_dense[l // 2].astype(BF16),
                      tm_ffn, 256)
    return xt.reshape(batch, seq, d)
```

```python
import functools

import jax
import jax.numpy as jnp
from jax import lax
from jax.experimental import pallas as pl
from jax.experimental.pallas import tpu as pltpu

F32 = jnp.float32
BF16 = jnp.bfloat16

HEAD_DIM = 64
N_HEADS = 8
W_BRANCH = N_HEADS * HEAD_DIM
N_EXPERTS = 8
TOP_K = 2
MOE_CHUNK = 16
MOE_EXPERT_TILE = 512
RMS_EPS = 1e-6
QK_SCALE = 1.0 / 8.0
LANES = 128
HEADS_PER_BLOCK = LANES // HEAD_DIM
V7X_VMEM_LIMIT_BYTES = 56 * 1024 * 1024

ATT_BLOCK = 256
ATT_LANE_BLOCKS = 4
F32_EXP_UNDERFLOW = 106.0
QK_BOUND_MARGIN = 1.05
CUM_BLOCK = 512


def _split_bf16(x, parts):
    out = []
    for _ in range(parts - 1):
        hi = x.astype(BF16)
        out.append(hi)
        x = x - hi.astype(F32)
    out.append(x.astype(BF16))
    return out


def _dot(a, b):
    return jnp.dot(a, b, preferred_element_type=F32)


def _dot_nt(a, b):
    return lax.dot_general(a, b, (((1,), (1,)), ((), ())), preferred_element_type=F32)


def _rmsnorm(x, g):
    ms = jnp.mean(x * x, axis=-1, keepdims=True)
    return x * lax.rsqrt(ms + RMS_EPS) * g


def _sigmoid(x):
    return 1.0 / (1.0 + jnp.exp(-x))


def _compiler_params(semantics):
    return pltpu.CompilerParams(dimension_semantics=semantics,
                                vmem_limit_bytes=V7X_VMEM_LIMIT_BYTES)


def _proj_kernel(x_ref, g_ref, w_ref, gq_ref, gk_ref, sb_ref, fx_ref, f_ref):
    h = _rmsnorm(x_ref[...], g_ref[...]).astype(BF16)
    wb = W_BRANCH
    r = lax.broadcasted_iota(jnp.int32, (wb, wb), 0) // HEAD_DIM
    c = lax.broadcasted_iota(jnp.int32, (wb, wb), 1) // HEAD_DIM
    head_mean = jnp.where(r == c, 1.0 / HEAD_DIM, 0.0).astype(BF16)
    for i in range(3):
        acc = _dot(h, w_ref[:, i * wb:(i + 1) * wb])
        if i == 0:
            acc = acc * QK_SCALE
        sb_ref[:, i * wb:(i + 1) * wb] = acc.astype(BF16)
    for i, gain_ref in enumerate((gq_ref, gk_ref, None)):
        acc = _dot(h, w_ref[:, (3 + i) * wb:(4 + i) * wb])
        if gain_ref is not None:
            sq_hi, sq_lo = _split_bf16(acc * acc, 2)
            ms = _dot(sq_hi, head_mean) + _dot(sq_lo, head_mean)
            acc = acc * lax.rsqrt(ms + RMS_EPS) * gain_ref[...]
        if i == 0:
            acc = acc * QK_SCALE
        fx_ref[:, i * wb:(i + 1) * wb] = acc.astype(BF16)
    f_ref[...] = _dot(h, w_ref[:, 6 * wb:6 * wb + LANES])


def _proj(x, g, w, gq, gk, tm):
    t, d = x.shape
    n = w.shape[1]
    full = lambda i: (0, 0)
    return pl.pallas_call(
        _proj_kernel,
        grid=(t // tm,),
        in_specs=[
            pl.BlockSpec((tm, d), lambda i: (i, 0)),
            pl.BlockSpec((1, d), full),
            pl.BlockSpec((d, n), full),
            pl.BlockSpec((1, W_BRANCH), full),
            pl.BlockSpec((1, W_BRANCH), full),
        ],
        out_specs=[
            pl.BlockSpec((tm, 3 * W_BRANCH), lambda i: (i, 0)),
            pl.BlockSpec((tm, 3 * W_BRANCH), lambda i: (i, 0)),
            pl.BlockSpec((tm, LANES), lambda i: (i, 0)),
        ],
        out_shape=[
            jax.ShapeDtypeStruct((t, 3 * W_BRANCH), BF16),
            jax.ShapeDtypeStruct((t, 3 * W_BRANCH), BF16),
            jax.ShapeDtypeStruct((t, LANES), F32),
        ],
        compiler_params=_compiler_params(("parallel",)),
        name="proj",
    )(x, g, w, gq, gk)


def _cumf_kernel(f_ref, b_ref, o_ref):
    seq = f_ref.shape[1]
    cb = min(CUM_BLOCK, seq)
    r = lax.broadcasted_iota(jnp.int32, (cb, cb), 0)
    c = lax.broadcasted_iota(jnp.int32, (cb, cb), 1)
    prefix = jnp.where(r <= c, 1.0, 0.0).astype(BF16)
    carry = jnp.zeros((N_HEADS, 1), F32)
    for i in range(seq // cb):
        v = f_ref[0, i * cb:(i + 1) * cb, :] + b_ref[...]
        log_f = jnp.minimum(v, 0.0) - jnp.log1p(jnp.exp(-jnp.abs(v)))
        log_f = log_f.T[:N_HEADS, :]
        cum = carry
        for part in _split_bf16(log_f, 3):
            cum = cum + _dot(part, prefix)
        o_ref[0, :, i * cb:(i + 1) * cb] = cum
        carry = cum[:, cb - 1:cb]


def _cumf(f_pre, b_f):
    b, s, _ = f_pre.shape
    return pl.pallas_call(
        _cumf_kernel,
        grid=(b,),
        in_specs=[
            pl.BlockSpec((1, s, LANES), lambda i: (i, 0, 0)),
            pl.BlockSpec((1, LANES), lambda i: (0, 0)),
        ],
        out_specs=pl.BlockSpec((1, N_HEADS, s), lambda i: (i, 0, 0)),
        out_shape=jax.ShapeDtypeStruct((b, N_HEADS, s), F32),
        compiler_params=_compiler_params(("parallel",)),
        name="cumf",
    )(f_pre, b_f)


def _head_queries(q_ref, half):
    q = q_ref[0, :, half * LANES:(half + 1) * LANES].astype(F32)
    lane = lax.broadcasted_iota(jnp.int32, (1, LANES), 1)
    return [jnp.where((lane >= hh * HEAD_DIM) & (lane < (hh + 1) * HEAD_DIM), q, 0.0).astype(BF16)
            for hh in range(HEADS_PER_BLOCK)]


def _run_chains(q_ref, k_ref, v_ref, o_ref, stages, finish, stop):
    bk = ATT_BLOCK
    halves = range(q_ref.shape[2] // LANES)
    qi = pl.program_id(2)
    queries = [q for half in halves for q in _head_queries(q_ref, half)]

    def advance(kb, masked, states):
        start = pl.multiple_of(kb * bk, bk)
        kv = [(k_ref[0, pl.ds(start, bk), half * LANES:(half + 1) * LANES],
               v_ref[0, pl.ds(start, bk), half * LANES:(half + 1) * LANES]) for half in halves]
        args = [(head, q, *kv[head // HEADS_PER_BLOCK], start, masked)
                for head, q in enumerate(queries)]
        tmps = [None] * len(args)
        for stage in stages[:-1]:
            tmps = [stage(*a, tmp) for a, tmp in zip(args, tmps)]
        return tuple(stages[-1](*a, tmp, state) for a, tmp, state in zip(args, tmps, states))

    def body(carry):
        i, _, states = carry
        kb = qi - 1 - i
        states = advance(kb, False, states)
        return i + 1, stop(states, jnp.maximum(kb - 1, 0)).astype(jnp.int32), states

    states = advance(qi, True, [None] * len(queries))
    _, _, states = lax.while_loop(lambda c: jnp.logical_and(c[0] < qi, c[1] == 0), body,
                                  (jnp.int32(0), jnp.int32(0), states))
    lane = lax.broadcasted_iota(jnp.int32, (1, LANES), 1)
    for half in halves:
        outs = [finish(head, states[head])
                for head in range(half * HEADS_PER_BLOCK, (half + 1) * HEADS_PER_BLOCK)]
        o_ref[0, :, half * LANES:(half + 1) * LANES] = jnp.where(
            lane < HEAD_DIM, outs[0], outs[1]).astype(BF16)


def _sb_kernel(q_ref, k_ref, v_ref, o_ref):
    bk = ATT_BLOCK
    row = lax.broadcasted_iota(jnp.int32, (2 * bk, bk), 0)
    col = lax.broadcasted_iota(jnp.int32, (2 * bk, bk), 1)
    suffix2 = jnp.where((row >= col) & ((row < bk) | (row - bk >= col)), 1.0, 0.0).astype(BF16)
    below_diag = (lax.broadcasted_iota(jnp.int32, (bk, bk), 1)
                  < lax.broadcasted_iota(jnp.int32, (bk, bk), 0))

    def scores(head, q, k, v, start, masked, tmp):
        z = _dot_nt(q, k)
        neg_abs = lax.bitcast_convert_type(
            lax.bitcast_convert_type(z, jnp.uint32) | jnp.uint32(0x80000000), F32)
        sp = jnp.maximum(z, 0.0) + jnp.log(1.0 + jnp.exp(neg_abs))
        if masked:
            sp = jnp.where(below_diag, sp, 0.0)
        return z, jnp.concatenate(_split_bf16(sp, 2), axis=1)

    def suffix_sums(head, q, k, v, start, masked, tmp):
        z, sp_parts = tmp
        return z, _dot(sp_parts, suffix2)

    def weights(head, q, k, v, start, masked, tmp, state):
        z, incl = tmp
        arg = z - incl
        if state is not None:
            arg = arg - state[0]
        w = jnp.exp(arg)
        if masked:
            w = jnp.where(below_diag, w, 0.0)
        pv = _dot(w.astype(BF16), v)
        if state is None:
            return incl[:, 0:1], pv
        return state[0] + incl[:, 0:1], state[1] + pv

    def stop(states, next_block):
        carry = functools.reduce(jnp.minimum, [state[0] for state in states])
        return jnp.min(carry) > F32_EXP_UNDERFLOW

    _run_chains(q_ref, k_ref, v_ref, o_ref, (scores, suffix_sums, weights),
                lambda head, state: state[1], stop)


def _fox_kernel(zmax_ref, fend_ref, q_ref, k_ref, v_ref, f_ref, o_ref):
    bk = ATT_BLOCK
    n_blocks = k_ref.shape[1] // bk
    n_heads = f_ref.shape[2]
    row = lax.broadcasted_iota(jnp.int32, (bk, bk), 0)
    col = lax.broadcasted_iota(jnp.int32, (bk, bk), 1)
    causal = col <= row
    lane = lax.broadcasted_iota(jnp.int32, (1, LANES), 1)
    values = {}

    def values_with_ones(hh, v):
        if (hh, id(v)) not in values:
            own = (lane >= hh * HEAD_DIM) & (lane < (hh + 1) * HEAD_DIM)
            values[(hh, id(v))] = (v, jnp.where(own, v.astype(F32), 1.0).astype(BF16))
        return values[(hh, id(v))][1]

    def logits(head, q, k, v, start, masked, tmp):
        s = _dot_nt(q, k) - f_ref[0, 0, head:head + 1, pl.ds(start, bk)]
        if masked:
            s = jnp.where(causal, s, -jnp.inf)
        return s, jnp.max(s, axis=1, keepdims=True)

    def accumulate(head, q, k, v, start, masked, tmp, state):
        s, m_blk = tmp
        v_ones = values_with_ones(head % HEADS_PER_BLOCK, v)
        if state is None:
            return m_blk, _dot(jnp.exp(s - m_blk).astype(BF16), v_ones)
        m, acc = state
        m_new = jnp.maximum(m, m_blk)
        pv = _dot(jnp.exp(s - m_new).astype(BF16), v_ones)
        return m_new, jnp.exp(m - m_new) * acc + pv

    def finish(head, state):
        acc = state[1]
        ones_lane = (1 - head % HEADS_PER_BLOCK) * HEAD_DIM
        return acc / acc[:, ones_lane:ones_lane + 1]

    def stop(states, next_block):
        base = ((pl.program_id(0) * (W_BRANCH // HEAD_DIM) + pl.program_id(1) * n_heads) * n_blocks
                + next_block)
        slack = functools.reduce(jnp.minimum, [state[0] - fend_ref[base + head * n_blocks]
                                               for head, state in enumerate(states)])
        return jnp.min(slack) > zmax_ref[0] + F32_EXP_UNDERFLOW

    _run_chains(q_ref, k_ref, v_ref, o_ref, (logits, accumulate), finish, stop)


def _attention(kernel, qkv, cum_f=None, zmax=None):
    b, s, _ = qkv.shape
    bq = min(ATT_BLOCK, s)
    width = ATT_LANE_BLOCKS * LANES
    n_groups = W_BRANCH // width
    in_specs = [
        pl.BlockSpec((1, bq, width), lambda bi, g, qi: (bi, qi, g)),
        pl.BlockSpec((1, s, width), lambda bi, g, qi: (bi, 0, n_groups + g)),
        pl.BlockSpec((1, s, width), lambda bi, g, qi: (bi, 0, 2 * n_groups + g)),
    ]
    args = [qkv, qkv, qkv]
    if cum_f is not None:
        heads = ATT_LANE_BLOCKS * HEADS_PER_BLOCK
        smem = pl.BlockSpec(memory_space=pltpu.SMEM)
        in_specs = [smem, smem] + in_specs + [
            pl.BlockSpec((1, 1, heads, s), lambda bi, g, qi: (bi, g, 0, 0))]
        f_end = -cum_f[:, :, bq - 1::bq]
        args = [zmax, f_end.reshape(-1)] + args + [cum_f.reshape(b, n_groups, heads, s)]
    return pl.pallas_call(
        kernel,
        grid=(b, n_groups, s // bq),
        in_specs=in_specs,
        out_specs=pl.BlockSpec((1, bq, width), lambda bi, g, qi: (bi, qi, g)),
        out_shape=jax.ShapeDtypeStruct((b, s, W_BRANCH), BF16),
        compiler_params=_compiler_params(("parallel", "parallel", "arbitrary")),
        name=kernel.__name__.strip("_"),
    )(*args)


def _local_rows(tm):
    return TOP_K * tm + N_EXPERTS * MOE_CHUNK


def _route_and_sort(logits, h2, xs_ref, len_ref, route_ref):
    tm = logits.shape[0]
    lt = logits.T[:N_EXPERTS, :]
    expert = lax.broadcasted_iota(jnp.int32, lt.shape, 0).astype(F32)
    m1 = jnp.max(lt, axis=0, keepdims=True)
    i1 = jnp.min(jnp.where(lt == m1, expert, float(N_EXPERTS)), axis=0, keepdims=True)
    lt2 = jnp.where(expert == i1, -jnp.inf, lt)
    m2 = jnp.max(lt2, axis=0, keepdims=True)
    i2 = jnp.min(jnp.where(lt2 == m2, expert, float(N_EXPERTS)), axis=0, keepdims=True)
    e2 = jnp.exp(m2 - m1)
    w1 = 1.0 / (1.0 + e2)
    w2 = e2 * w1
    chosen = jnp.where((expert == i1) | (expert == i2), 1.0, 0.0)
    before = (lax.broadcasted_iota(jnp.int32, (tm, tm), 0)
              < lax.broadcasted_iota(jnp.int32, (tm, tm), 1))
    rank = _dot(chosen.astype(BF16), jnp.where(before, 1.0, 0.0).astype(BF16))
    count = jnp.sum(chosen, axis=1, keepdims=True)
    padded = jnp.floor((count + (MOE_CHUNK - 1)) * (1.0 / MOE_CHUNK)) * MOE_CHUNK
    r1 = jnp.sum(jnp.where(expert == i1, rank, 0.0), axis=0, keepdims=True)
    r2 = jnp.sum(jnp.where(expert == i2, rank, 0.0), axis=0, keepdims=True)
    start = jnp.zeros((1, 1), F32)
    for e in range(N_EXPERTS):
        r1 = r1 + jnp.where(i1 == e, start, 0.0)
        r2 = r2 + jnp.where(i2 == e, start, 0.0)
        start = start + padded[e:e + 1, :]
    rows = _local_rows(tm)
    row = lax.broadcasted_iota(jnp.int32, (rows, tm), 0).astype(F32)
    place = jnp.where((row == r1) | (row == r2), 1.0, 0.0).astype(BF16)
    xs_ref[0] = _dot(place, h2).astype(BF16)
    len_ref[0] = jnp.broadcast_to(padded, (N_EXPERTS, LANES))
    field = lax.broadcasted_iota(jnp.int32, (LANES, tm), 0)
    fields = jnp.where(field == 0, r1, jnp.where(field == 1, r2, jnp.where(
        field == 2, w1, jnp.where(field == 3, w2, 0.0))))
    route_ref[...] = fields.T


def _merge_kernel(x_ref, ysb_ref, yfx_ref, gmix_ref, wgate_ref, bgate_ref, wosb_ref, wofx_ref,
                  wout_ref, gffn_ref, *rest, with_router):
    if with_router:
        wr_hi_ref, wr_lo_ref, x1_ref, xs_ref, len_ref, route_ref = rest
    else:
        x1_ref, h2_ref = rest
    d = x_ref.shape[1]
    x = x_ref[...]
    h = _rmsnorm(x, gmix_ref[...]).astype(BF16)
    merged = None
    for i, (y_ref, wo_ref) in enumerate(((ysb_ref, wosb_ref), (yfx_ref, wofx_ref))):
        gate = _sigmoid(_dot(h, wgate_ref[:, i * d:(i + 1) * d]) + bgate_ref[:, i * d:(i + 1) * d])
        branch = gate * _dot(y_ref[...], wo_ref[...])
        merged = branch if merged is None else merged + branch
    x1 = x + _dot(merged.astype(BF16), wout_ref[...])
    x1_ref[...] = x1
    h2 = _rmsnorm(x1, gffn_ref[...])
    if with_router:
        h_hi, h_lo = _split_bf16(h2, 2)
        logits = _dot(h_hi, wr_hi_ref[...]) + _dot(h_lo, wr_hi_ref[...]) + _dot(h_hi, wr_lo_ref[...])
        _route_and_sort(logits, h_hi, xs_ref, len_ref, route_ref)
    else:
        h2_ref[...] = h2.astype(BF16)


def _merge(x, y_sb, y_fx, g_mix, w_gate, b_gate, w_o_sb, w_o_fox, w_out, g_ffn, w_router, tm):
    t, d = x.shape
    with_router = w_router is not None
    full = lambda i: (0, 0)
    tile = lambda i: (i, 0)
    in_specs = [
        pl.BlockSpec((tm, d), tile),
        pl.BlockSpec((tm, W_BRANCH), tile),
        pl.BlockSpec((tm, W_BRANCH), tile),
        pl.BlockSpec((1, d), full),
        pl.BlockSpec((d, 2 * d), full),
        pl.BlockSpec((1, 2 * d), full),
        pl.BlockSpec((W_BRANCH, d), full),
        pl.BlockSpec((W_BRANCH, d), full),
        pl.BlockSpec((d, d), full),
        pl.BlockSpec((1, d), full),
    ]
    args = [x, y_sb, y_fx, g_mix, w_gate, b_gate, w_o_sb, w_o_fox, w_out, g_ffn]
    out_specs = [pl.BlockSpec((tm, d), tile)]
    out_shape = [jax.ShapeDtypeStruct((t, d), F32)]
    if with_router:
        rows = _local_rows(tm)
        in_specs += [pl.BlockSpec((d, LANES), full)] * 2
        args += list(w_router)
        out_specs += [pl.BlockSpec((1, rows, d), lambda i: (i, 0, 0)),
                      pl.BlockSpec((1, N_EXPERTS, LANES), lambda i: (i, 0, 0)),
                      pl.BlockSpec((tm, LANES), tile)]
        out_shape += [jax.ShapeDtypeStruct((t // tm, rows, d), BF16),
                      jax.ShapeDtypeStruct((t // tm, N_EXPERTS, LANES), F32),
                      jax.ShapeDtypeStruct((t, LANES), F32)]
    else:
        out_specs.append(pl.BlockSpec((tm, d), tile))
        out_shape.append(jax.ShapeDtypeStruct((t, d), BF16))
    return pl.pallas_call(
        functools.partial(_merge_kernel, with_router=with_router),
        grid=(t // tm,),
        in_specs=in_specs,
        out_specs=out_specs,
        out_shape=out_shape,
        compiler_params=_compiler_params(("parallel",)),
        name="merge_router" if with_router else "merge",
    )(*args)


def _swiglu_chunk(h, wg, wu, wd):
    g = _dot(h, wg)
    u = _dot(h, wu)
    return _dot((g * _sigmoid(g) * u).astype(BF16), wd)


def _ffn_kernel(h_ref, x_ref, wg_ref, wu_ref, wd_ref, o_ref):
    @pl.when(pl.program_id(1) == 0)
    def _():
        o_ref[...] = x_ref[...]

    o_ref[...] += _swiglu_chunk(h_ref[...], wg_ref[...], wu_ref[...], wd_ref[...])


def _ffn(h, x, w_gu, w_dn, tm, tf):
    t, d = x.shape
    d_ff = w_dn.shape[0]
    nf = d_ff // tf
    return pl.pallas_call(
        _ffn_kernel,
        grid=(t // tm, nf),
        in_specs=[
            pl.BlockSpec((tm, d), lambda i, j: (i, 0)),
            pl.BlockSpec((tm, d), lambda i, j: (i, 0)),
            pl.BlockSpec((d, tf), lambda i, j: (0, j)),
            pl.BlockSpec((d, tf), lambda i, j: (0, nf + j)),
            pl.BlockSpec((tf, d), lambda i, j: (j, 0)),
        ],
        out_specs=pl.BlockSpec((tm, d), lambda i, j: (i, 0)),
        out_shape=jax.ShapeDtypeStruct((t, d), F32),
        compiler_params=_compiler_params(("parallel", "arbitrary")),
        name="ffn",
    )(h, x, w_gu, w_gu, w_dn)


def _expert_kernel(expert_ref, valid_ref, src_ref, xs_hbm, wg_ref, wu_ref, wd_ref, o_ref, buf, sems):
    j = pl.program_id(0)
    rows = buf.shape[1]
    chunks = rows // MOE_CHUNK

    def fetches(step, slot):
        return [pltpu.make_async_copy(
            xs_hbm.at[src_ref[step * chunks + c]],
            buf.at[slot, pl.ds(c * MOE_CHUNK, MOE_CHUNK), :], sems.at[slot]) for c in range(chunks)]

    slot = lax.rem(j, 2)

    @pl.when(j == 0)
    def _():
        for copy in fetches(0, 0):
            copy.start()

    @pl.when(j + 1 < pl.num_programs(0))
    def _():
        for copy in fetches(j + 1, 1 - slot):
            copy.start()

    for copy in fetches(j, slot):
        copy.wait()

    @pl.when(valid_ref[j] != 0)
    def _():
        o_ref[...] = _swiglu_chunk(buf[slot], wg_ref[0], wu_ref[0], wd_ref[0]).astype(BF16)

    @pl.when(valid_ref[j] == 0)
    def _():
        o_ref[...] = jnp.zeros_like(o_ref)


def _experts(xs, tile_expert, tile_valid, chunk_src, w_gu, w_dn, n_rows):
    d = xs.shape[2]
    d_ff = w_dn.shape[1]
    te = MOE_EXPERT_TILE
    grid_spec = pltpu.PrefetchScalarGridSpec(
        num_scalar_prefetch=3,
        grid=(n_rows // te,),
        in_specs=[
            pl.BlockSpec(memory_space=pl.ANY),
            pl.BlockSpec((1, d, d_ff), lambda j, ex, va, sr: (ex[j], 0, 0)),
            pl.BlockSpec((1, d, d_ff), lambda j, ex, va, sr: (ex[j], 0, 1)),
            pl.BlockSpec((1, d_ff, d), lambda j, ex, va, sr: (ex[j], 0, 0)),
        ],
        out_specs=pl.BlockSpec((te, d), lambda j, ex, va, sr: (j, 0)),
        scratch_shapes=[pltpu.VMEM((2, te, d), BF16), pltpu.SemaphoreType.DMA((2,))],
    )
    return pl.pallas_call(
        _expert_kernel,
        grid_spec=grid_spec,
        out_shape=jax.ShapeDtypeStruct((n_rows, d), BF16),
        compiler_params=_compiler_params(("arbitrary",)),
        name="experts",
    )(tile_expert, tile_valid, chunk_src, xs, w_gu, w_gu, w_dn)


def _combine_kernel(goff_ref, count_ref, loff_ref, x_ref, route_ref, ys_hbm, o_ref, buf, sem):
    i = pl.program_id(0)

    @pl.when(i == 0)
    def _():
        buf[...] = jnp.zeros_like(buf)

    def fetch(e, c):
        g = i * N_EXPERTS + e
        dst = pl.multiple_of((loff_ref[g] + c) * MOE_CHUNK, MOE_CHUNK)
        return pltpu.make_async_copy(ys_hbm.at[goff_ref[g] + c], buf.at[pl.ds(dst, MOE_CHUNK), :], sem)

    for e in range(N_EXPERTS):
        @pl.loop(0, count_ref[i * N_EXPERTS + e])
        def _(c):
            fetch(e, c).start()

    for e in range(N_EXPERTS):
        @pl.loop(0, count_ref[i * N_EXPERTS + e])
        def _(c):
            fetch(e, c).wait()

    route = route_ref[...]
    y = buf[...]
    col = lax.broadcasted_iota(jnp.int32, (route.shape[0], buf.shape[0]), 1).astype(F32)
    out = x_ref[...]
    for k in range(TOP_K):
        pick = jnp.where(col == route[:, k:k + 1], 1.0, 0.0).astype(BF16)
        out = out + route[:, TOP_K + k:TOP_K + k + 1] * _dot(pick, y)
    o_ref[...] = out


def _combine(x, route, ys, goff, count, loff, tm):
    t, d = x.shape
    grid_spec = pltpu.PrefetchScalarGridSpec(
        num_scalar_prefetch=3,
        grid=(t // tm,),
        in_specs=[
            pl.BlockSpec((tm, d), lambda i, *_: (i, 0)),
            pl.BlockSpec((tm, LANES), lambda i, *_: (i, 0)),
            pl.BlockSpec(memory_space=pl.ANY),
        ],
        out_specs=pl.BlockSpec((tm, d), lambda i, *_: (i, 0)),
        scratch_shapes=[pltpu.VMEM((_local_rows(tm), d), BF16), pltpu.SemaphoreType.DMA(())],
    )
    return pl.pallas_call(
        _combine_kernel,
        grid_spec=grid_spec,
        out_shape=jax.ShapeDtypeStruct((t, d), F32),
        compiler_params=_compiler_params(("arbitrary",)),
        name="combine",
    )(goff, count, loff, x, route, ys)


def _moe(x1, xs_local, padded_len, route, w_gu, w_dn, tm):
    t, d = x1.shape
    n_tiles, local_rows, _ = xs_local.shape
    te = MOE_EXPERT_TILE
    n_rows = -(-(TOP_K * t + n_tiles * N_EXPERTS * (MOE_CHUNK - 1) + N_EXPERTS * (te - 1)) // te) * te
    length = padded_len[:, :, 0].astype(jnp.int32)
    count = length // MOE_CHUNK
    used = jnp.sum(length, axis=0)
    span = -(-used // te) * te
    base = jnp.cumsum(span) - span
    goff = (base[None, :] + jnp.cumsum(length, axis=0) - length) // MOE_CHUNK
    loff = (jnp.cumsum(length, axis=1) - length) // MOE_CHUNK
    seg_start = goff.T.reshape(-1)
    seg_count = count.T.reshape(-1)
    seg_src = (jnp.arange(n_tiles, dtype=jnp.int32)[:, None] * (local_rows // MOE_CHUNK) + loff).T.reshape(-1)
    chunk = jnp.arange(n_rows // MOE_CHUNK, dtype=jnp.int32)
    seg = jnp.searchsorted(seg_start, chunk, side="right").astype(jnp.int32) - 1
    within = chunk - seg_start[seg]
    chunk_src = jnp.where(within < seg_count[seg], seg_src[seg] + within, 0).astype(jnp.int32)
    tile_row = jnp.arange(n_rows // te, dtype=jnp.int32) * te
    tile_expert = (jnp.searchsorted(base, tile_row, side="right") - 1).astype(jnp.int32)
    tile_valid = (tile_row < base[tile_expert] + used[tile_expert]).astype(jnp.int32)
    ys = _experts(xs_local.reshape(-1, MOE_CHUNK, d), tile_expert, tile_valid, chunk_src, w_gu, w_dn, n_rows)
    return _combine(x1, route, ys.reshape(-1, MOE_CHUNK, d), goff.reshape(-1), count.reshape(-1),
                    loff.reshape(-1), tm)


def _pad_lanes(w):
    return jnp.pad(w, ((0, 0), (0, LANES - w.shape[1])))


def kernel(x, g_mix, w_in, b_f, b_gate, g_q, g_k, w_o_sb, w_o_fox, w_out, g_ffn, w_gu_dense,
           w_dn_dense, w_router, w_gu_exp, w_dn_exp):
    batch, seq, d = x.shape
    depth = w_in.shape[0]
    t = batch * seq
    tm_proj = min(512, t)
    tm_ffn = min(1024, t)
    n_qkv = 6 * W_BRANCH
    xt = x.reshape(t, d)
    for l in range(depth):
        w_l = w_in[l]
        w_proj = jnp.concatenate(
            [w_l[:, :n_qkv], _pad_lanes(w_l[:, n_qkv:n_qkv + N_HEADS])], axis=1).astype(BF16)
        w_gate = w_l[:, n_qkv + N_HEADS:].astype(BF16)
        row = lambda v: v.reshape(1, -1)
        gq = jnp.tile(g_q[l], N_HEADS).reshape(1, W_BRANCH)
        gk = jnp.tile(g_k[l], N_HEADS).reshape(1, W_BRANCH)
        qkv_sb, qkv_fx, f_pre = _proj(xt, row(g_mix[l]), w_proj, gq, gk, tm_proj)
        cum_f = _cumf(f_pre.reshape(batch, seq, LANES), _pad_lanes(row(b_f[l])))
        y_sb = _attention(_sb_kernel, qkv_sb.reshape(batch, seq, 3 * W_BRANCH))
        zmax = (QK_BOUND_MARGIN * 8.0 * jnp.max(jnp.abs(g_q[l])) * jnp.max(jnp.abs(g_k[l]))).reshape(1)
        y_fx = _attention(_fox_kernel, qkv_fx.reshape(batch, seq, 3 * W_BRANCH), cum_f, zmax)
        moe_layer = l % 2 == 1
        router = None
        if moe_layer:
            wr = _pad_lanes(w_router[l // 2])
            router = _split_bf16(wr, 2)
        outs = _merge(xt, y_sb.reshape(t, W_BRANCH), y_fx.reshape(t, W_BRANCH), row(g_mix[l]),
                      w_gate, row(b_gate[l]), w_o_sb[l].astype(BF16), w_o_fox[l].astype(BF16),
                      w_out[l].astype(BF16), row(g_ffn[l]), router, tm_proj)
        if moe_layer:
            x1, xs_local, padded_len, route = outs
            xt = _moe(x1, xs_local, padded_len, route, w_gu_exp[l // 2].astype(BF16),
                      w_dn_exp[l // 2].astype(BF16), tm_proj)
        else:
            x1, h2 = outs
            xt = _ffn(h2, x1, w_gu_dense[l // 2].astype(BF16), w_dn_dense[l // 2].astype(BF16),
                      tm_ffn, 256)
    return xt.reshape(batch, seq, d)
```

```python
import functools

import jax
import jax.numpy as jnp
from jax import lax
from jax.experimental import pallas as pl
from jax.experimental.pallas import tpu as pltpu

F32 = jnp.float32
BF16 = jnp.bfloat16

HEAD_DIM = 64
N_HEADS = 8
W_BRANCH = N_HEADS * HEAD_DIM
N_EXPERTS = 8
TOP_K = 2
MOE_CHUNK = 16
MOE_EXPERT_TILE = 512
RMS_EPS = 1e-6
QK_SCALE = 1.0 / 8.0
LANES = 128
HEADS_PER_BLOCK = LANES // HEAD_DIM
V7X_VMEM_LIMIT_BYTES = 56 * 1024 * 1024

ATT_BLOCK = 256
ATT_LANE_BLOCKS = 4
F32_EXP_UNDERFLOW = 106.0
QK_BOUND_MARGIN = 1.05
CUM_BLOCK = 512


def _split_bf16(x, parts):
    out = []
    for _ in range(parts - 1):
        hi = x.astype(BF16)
        out.append(hi)
        x = x - hi.astype(F32)
    out.append(x.astype(BF16))
    return out


def _dot(a, b):
    return jnp.dot(a, b, preferred_element_type=F32)


def _dot_nt(a, b):
    return lax.dot_general(a, b, (((1,), (1,)), ((), ())), preferred_element_type=F32)


def _rmsnorm(x, g):
    ms = jnp.mean(x * x, axis=-1, keepdims=True)
    return x * lax.rsqrt(ms + RMS_EPS) * g


def _sigmoid(x):
    return 1.0 / (1.0 + jnp.exp(-x))


def _compiler_params(semantics):
    return pltpu.CompilerParams(dimension_semantics=semantics,
                                vmem_limit_bytes=V7X_VMEM_LIMIT_BYTES)


def _proj_kernel(x_ref, g_ref, w_ref, gq_ref, gk_ref, sb_ref, fx_ref, f_ref):
    h = _rmsnorm(x_ref[...], g_ref[...]).astype(BF16)
    wb = W_BRANCH
    r = lax.broadcasted_iota(jnp.int32, (wb, wb), 0) // HEAD_DIM
    c = lax.broadcasted_iota(jnp.int32, (wb, wb), 1) // HEAD_DIM
    head_mean = jnp.where(r == c, 1.0 / HEAD_DIM, 0.0).astype(BF16)
    for i in range(3):
        acc = _dot(h, w_ref[:, i * wb:(i + 1) * wb])
        if i == 0:
            acc = acc * QK_SCALE
        sb_ref[:, i * wb:(i + 1) * wb] = acc.astype(BF16)
    for i, gain_ref in enumerate((gq_ref, gk_ref, None)):
        acc = _dot(h, w_ref[:, (3 + i) * wb:(4 + i) * wb])
        if gain_ref is not None:
            sq_hi, sq_lo = _split_bf16(acc * acc, 2)
            ms = _dot(sq_hi, head_mean) + _dot(sq_lo, head_mean)
            acc = acc * lax.rsqrt(ms + RMS_EPS) * gain_ref[...]
        if i == 0:
            acc = acc * QK_SCALE
        fx_ref[:, i * wb:(i + 1) * wb] = acc.astype(BF16)
    f_ref[...] = _dot(h, w_ref[:, 6 * wb:6 * wb + LANES])


def _proj(x, g, w, gq, gk, tm):
    t, d = x.shape
    n = w.shape[1]
    full = lambda i: (0, 0)
    return pl.pallas_call(
        _proj_kernel,
        grid=(t // tm,),
        in_specs=[
            pl.BlockSpec((tm, d), lambda i: (i, 0)),
            pl.BlockSpec((1, d), full),
            pl.BlockSpec((d, n), full),
            pl.BlockSpec((1, W_BRANCH), full),
            pl.BlockSpec((1, W_BRANCH), full),
        ],
        out_specs=[
            pl.BlockSpec((tm, 3 * W_BRANCH), lambda i: (i, 0)),
            pl.BlockSpec((tm, 3 * W_BRANCH), lambda i: (i, 0)),
            pl.BlockSpec((tm, LANES), lambda i: (i, 0)),
        ],
        out_shape=[
            jax.ShapeDtypeStruct((t, 3 * W_BRANCH), BF16),
            jax.ShapeDtypeStruct((t, 3 * W_BRANCH), BF16),
            jax.ShapeDtypeStruct((t, LANES), F32),
        ],
        compiler_params=_compiler_params(("parallel",)),
        name="proj",
    )(x, g, w, gq, gk)


def _cumf_kernel(f_ref, b_ref, o_ref):
    seq = f_ref.shape[1]
    cb = min(CUM_BLOCK, seq)
    r = lax.broadcasted_iota(jnp.int32, (cb, cb), 0)
    c = lax.broadcasted_iota(jnp.int32, (cb, cb), 1)
    prefix = jnp.where(r <= c, 1.0, 0.0).astype(BF16)
    carry = jnp.zeros((N_HEADS, 1), F32)
    for i in range(seq // cb):
        v = f_ref[0, i * cb:(i + 1) * cb, :] + b_ref[...]
        log_f = jnp.minimum(v, 0.0) - jnp.log1p(jnp.exp(-jnp.abs(v)))
        log_f = log_f.T[:N_HEADS, :]
        cum = carry
        for part in _split_bf16(log_f, 3):
            cum = cum + _dot(part, prefix)
        o_ref[0, :, i * cb:(i + 1) * cb] = cum
        carry = cum[:, cb - 1:cb]


def _cumf(f_pre, b_f):
    b, s, _ = f_pre.shape
    return pl.pallas_call(
        _cumf_kernel,
        grid=(b,),
        in_specs=[
            pl.BlockSpec((1, s, LANES), lambda i: (i, 0, 0)),
            pl.BlockSpec((1, LANES), lambda i: (0, 0)),
        ],
        out_specs=pl.BlockSpec((1, N_HEADS, s), lambda i: (i, 0, 0)),
        out_shape=jax.ShapeDtypeStruct((b, N_HEADS, s), F32),
        compiler_params=_compiler_params(("parallel",)),
        name="cumf",
    )(f_pre, b_f)


def _head_queries(q_ref, half):
    q = q_ref[0, :, half * LANES:(half + 1) * LANES].astype(F32)
    lane = lax.broadcasted_iota(jnp.int32, (1, LANES), 1)
    return [jnp.where((lane >= hh * HEAD_DIM) & (lane < (hh + 1) * HEAD_DIM), q, 0.0).astype(BF16)
            for hh in range(HEADS_PER_BLOCK)]


def _run_chains(q_ref, k_ref, v_ref, o_ref, stages, finish, stop):
    bk = ATT_BLOCK
    halves = range(q_ref.shape[2] // LANES)
    qi = pl.program_id(2)
    queries = [q for half in halves for q in _head_queries(q_ref, half)]

    def advance(kb, masked, states):
        start = pl.multiple_of(kb * bk, bk)
        kv = [(k_ref[0, pl.ds(start, bk), half * LANES:(half + 1) * LANES],
               v_ref[0, pl.ds(start, bk), half * LANES:(half + 1) * LANES]) for half in halves]
        args = [(head, q, *kv[head // HEADS_PER_BLOCK], start, masked)
                for head, q in enumerate(queries)]
        tmps = [None] * len(args)
        for stage in stages[:-1]:
            tmps = [stage(*a, tmp) for a, tmp in zip(args, tmps)]
        return tuple(stages[-1](*a, tmp, state) for a, tmp, state in zip(args, tmps, states))

    def body(carry):
        i, _, states = carry
        kb = qi - 1 - i
        states = advance(kb, False, states)
        return i + 1, stop(states, jnp.maximum(kb - 1, 0)).astype(jnp.int32), states

    states = advance(qi, True, [None] * len(queries))
    _, _, states = lax.while_loop(lambda c: jnp.logical_and(c[0] < qi, c[1] == 0), body,
                                  (jnp.int32(0), jnp.int32(0), states))
    lane = lax.broadcasted_iota(jnp.int32, (1, LANES), 1)
    for half in halves:
        outs = [finish(head, states[head])
                for head in range(half * HEADS_PER_BLOCK, (half + 1) * HEADS_PER_BLOCK)]
        o_ref[0, :, half * LANES:(half + 1) * LANES] = jnp.where(
            lane < HEAD_DIM, outs[0], outs[1]).astype(BF16)


def _sb_kernel(q_ref, k_ref, v_ref, o_ref):
    bk = ATT_BLOCK
    row = lax.broadcasted_iota(jnp.int32, (2 * bk, bk), 0)
    col = lax.broadcasted_iota(jnp.int32, (2 * bk, bk), 1)
    suffix2 = jnp.where((row >= col) & ((row < bk) | (row - bk >= col)), 1.0, 0.0).astype(BF16)
    below_diag = (lax.broadcasted_iota(jnp.int32, (bk, bk), 1)
                  < lax.broadcasted_iota(jnp.int32, (bk, bk), 0))

    def scores(head, q, k, v, start, masked, tmp):
        z = _dot_nt(q, k)
        neg_abs = lax.bitcast_convert_type(
            lax.bitcast_convert_type(z, jnp.uint32) | jnp.uint32(0x80000000), F32)
        sp = jnp.maximum(z, 0.0) + jnp.log(1.0 + jnp.exp(neg_abs))
        if masked:
            sp = jnp.where(below_diag, sp, 0.0)
        return z, jnp.concatenate(_split_bf16(sp, 2), axis=1)

    def suffix_sums(head, q, k, v, start, masked, tmp):
        z, sp_parts = tmp
        return z, _dot(sp_parts, suffix2)

    def weights(head, q, k, v, start, masked, tmp, state):
        z, incl = tmp
        arg = z - incl
        if state is not None:
            arg = arg - state[0]
        w = jnp.exp(arg)
        if masked:
            w = jnp.where(below_diag, w, 0.0)
        pv = _dot(w.astype(BF16), v)
        if state is None:
            return incl[:, 0:1], pv
        return state[0] + incl[:, 0:1], state[1] + pv

    def stop(states, next_block):
        carry = functools.reduce(jnp.minimum, [state[0] for state in states])
        return jnp.min(carry) > F32_EXP_UNDERFLOW

    _run_chains(q_ref, k_ref, v_ref, o_ref, (scores, suffix_sums, weights),
                lambda head, state: state[1], stop)


def _fox_kernel(zmax_ref, fend_ref, q_ref, k_ref, v_ref, f_ref, o_ref):
    bk = ATT_BLOCK
    n_blocks = k_ref.shape[1] // bk
    n_heads = f_ref.shape[2]
    row = lax.broadcasted_iota(jnp.int32, (bk, bk), 0)
    col = lax.broadcasted_iota(jnp.int32, (bk, bk), 1)
    causal = col <= row
    lane = lax.broadcasted_iota(jnp.int32, (1, LANES), 1)
    values = {}

    def values_with_ones(hh, v):
        if (hh, id(v)) not in values:
            own = (lane >= hh * HEAD_DIM) & (lane < (hh + 1) * HEAD_DIM)
            values[(hh, id(v))] = (v, jnp.where(own, v.astype(F32), 1.0).astype(BF16))
        return values[(hh, id(v))][1]

    def logits(head, q, k, v, start, masked, tmp):
        s = _dot_nt(q, k) - f_ref[0, 0, head:head + 1, pl.ds(start, bk)]
        if masked:
            s = jnp.where(causal, s, -jnp.inf)
        return s, jnp.max(s, axis=1, keepdims=True)

    def accumulate(head, q, k, v, start, masked, tmp, state):
        s, m_blk = tmp
        v_ones = values_with_ones(head % HEADS_PER_BLOCK, v)
        if state is None:
            return m_blk, _dot(jnp.exp(s - m_blk).astype(BF16), v_ones)
        m, acc = state
        m_new = jnp.maximum(m, m_blk)
        pv = _dot(jnp.exp(s - m_new).astype(BF16), v_ones)
        return m_new, jnp.exp(m - m_new) * acc + pv

    def finish(head, state):
        acc = state[1]
        ones_lane = (1 - head % HEADS_PER_BLOCK) * HEAD_DIM
        return acc / acc[:, ones_lane:ones_lane + 1]

    def stop(states, next_block):
        base = ((pl.program_id(0) * (W_BRANCH // HEAD_DIM) + pl.program_id(1) * n_heads) * n_blocks
                + next_block)
        slack = functools.reduce(jnp.minimum, [state[0] - fend_ref[base + head * n_blocks]
                                               for head, state in enumerate(states)])
        return jnp.min(slack) > zmax_ref[0] + F32_EXP_UNDERFLOW

    _run_chains(q_ref, k_ref, v_ref, o_ref, (logits, accumulate), finish, stop)


def _attention(kernel, qkv, cum_f=None, zmax=None):
    b, s, _ = qkv.shape
    bq = min(ATT_BLOCK, s)
    width = ATT_LANE_BLOCKS * LANES
    n_groups = W_BRANCH // width
    in_specs = [
        pl.BlockSpec((1, bq, width), lambda bi, g, qi: (bi, qi, g)),
        pl.BlockSpec((1, s, width), lambda bi, g, qi: (bi, 0, n_groups + g)),
        pl.BlockSpec((1, s, width), lambda bi, g, qi: (bi, 0, 2 * n_groups + g)),
    ]
    args = [qkv, qkv, qkv]
    if cum_f is not None:
        heads = ATT_LANE_BLOCKS * HEADS_PER_BLOCK
        smem = pl.BlockSpec(memory_space=pltpu.SMEM)
        in_specs = [smem, smem] + in_specs + [
            pl.BlockSpec((1, 1, heads, s), lambda bi, g, qi: (bi, g, 0, 0))]
        f_end = -cum_f[:, :, bq - 1::bq]
        args = [zmax, f_end.reshape(-1)] + args + [cum_f.reshape(b, n_groups, heads, s)]
    return pl.pallas_call(
        kernel,
        grid=(b, n_groups, s // bq),
        in_specs=in_specs,
        out_specs=pl.BlockSpec((1, bq, width), lambda bi, g, qi: (bi, qi, g)),
        out_shape=jax.ShapeDtypeStruct((b, s, W_BRANCH), BF16),
        compiler_params=_compiler_params(("parallel", "parallel", "arbitrary")),
        name=kernel.__name__.strip("_"),
    )(*args)


def _local_rows(tm):
    return TOP_K * tm + N_EXPERTS * MOE_CHUNK


def _route_and_sort(logits, h2, xs_ref, len_ref, route_ref):
    tm = logits.shape[0]
    lt = logits.T[:N_EXPERTS, :]
    expert = lax.broadcasted_iota(jnp.int32, lt.shape, 0).astype(F32)
    m1 = jnp.max(lt, axis=0, keepdims=True)
    i1 = jnp.min(jnp.where(lt == m1, expert, float(N_EXPERTS)), axis=0, keepdims=True)
    lt2 = jnp.where(expert == i1, -jnp.inf, lt)
    m2 = jnp.max(lt2, axis=0, keepdims=True)
    i2 = jnp.min(jnp.where(lt2 == m2, expert, float(N_EXPERTS)), axis=0, keepdims=True)
    e2 = jnp.exp(m2 - m1)
    w1 = 1.0 / (1.0 + e2)
    w2 = e2 * w1
    chosen = jnp.where((expert == i1) | (expert == i2), 1.0, 0.0)
    before = (lax.broadcasted_iota(jnp.int32, (tm, tm), 0)
              < lax.broadcasted_iota(jnp.int32, (tm, tm), 1))
    rank = _dot(chosen.astype(BF16), jnp.where(before, 1.0, 0.0).astype(BF16))
    count = jnp.sum(chosen, axis=1, keepdims=True)
    padded = jnp.floor((count + (MOE_CHUNK - 1)) * (1.0 / MOE_CHUNK)) * MOE_CHUNK
    r1 = jnp.sum(jnp.where(expert == i1, rank, 0.0), axis=0, keepdims=True)
    r2 = jnp.sum(jnp.where(expert == i2, rank, 0.0), axis=0, keepdims=True)
    start = jnp.zeros((1, 1), F32)
    for e in range(N_EXPERTS):
        r1 = r1 + jnp.where(i1 == e, start, 0.0)
        r2 = r2 + jnp.where(i2 == e, start, 0.0)
        start = start + padded[e:e + 1, :]
    rows = _local_rows(tm)
    row = lax.broadcasted_iota(jnp.int32, (rows, tm), 0).astype(F32)
    place = jnp.where((row == r1) | (row == r2), 1.0, 0.0).astype(BF16)
    xs_ref[0] = _dot(place, h2).astype(BF16)
    len_ref[0] = jnp.broadcast_to(padded, (N_EXPERTS, LANES))
    field = lax.broadcasted_iota(jnp.int32, (LANES, tm), 0)
    fields = jnp.where(field == 0, r1, jnp.where(field == 1, r2, jnp.where(
        field == 2, w1, jnp.where(field == 3, w2, 0.0))))
    route_ref[...] = fields.T


def _merge_kernel(x_ref, ysb_ref, yfx_ref, gmix_ref, wgate_ref, bgate_ref, wosb_ref, wofx_ref,
                  wout_ref, gffn_ref, *rest, with_router):
    if with_router:
        wr_hi_ref, wr_lo_ref, x1_ref, xs_ref, len_ref, route_ref = rest
    else:
        x1_ref, h2_ref = rest
    d = x_ref.shape[1]
    x = x_ref[...]
    h = _rmsnorm(x, gmix_ref[...]).astype(BF16)
    merged = None
    for i, (y_ref, wo_ref) in enumerate(((ysb_ref, wosb_ref), (yfx_ref, wofx_ref))):
        gate = _sigmoid(_dot(h, wgate_ref[:, i * d:(i + 1) * d]) + bgate_ref[:, i * d:(i + 1) * d])
        branch = gate * _dot(y_ref[...], wo_ref[...])
        merged = branch if merged is None else merged + branch
    x1 = x + _dot(merged.astype(BF16), wout_ref[...])
    x1_ref[...] = x1
    h2 = _rmsnorm(x1, gffn_ref[...])
    if with_router:
        h_hi, h_lo = _split_bf16(h2, 2)
        logits = _dot(h_hi, wr_hi_ref[...]) + _dot(h_lo, wr_hi_ref[...]) + _dot(h_hi, wr_lo_ref[...])
        _route_and_sort(logits, h_hi, xs_ref, len_ref, route_ref)
    else:
        h2_ref[...] = h2.astype(BF16)


def _merge(x, y_sb, y_fx, g_mix, w_gate, b_gate, w_o_sb, w_o_fox, w_out, g_ffn, w_router, tm):
    t, d = x.shape
    with_router = w_router is not None
    full = lambda i: (0, 0)
    tile = lambda i: (i, 0)
    in_specs = [
        pl.BlockSpec((tm, d), tile),
        pl.BlockSpec((tm, W_BRANCH), tile),
        pl.BlockSpec((tm, W_BRANCH), tile),
        pl.BlockSpec((1, d), full),
        pl.BlockSpec((d, 2 * d), full),
        pl.BlockSpec((1, 2 * d), full),
        pl.BlockSpec((W_BRANCH, d), full),
        pl.BlockSpec((W_BRANCH, d), full),
        pl.BlockSpec((d, d), full),
        pl.BlockSpec((1, d), full),
    ]
    args = [x, y_sb, y_fx, g_mix, w_gate, b_gate, w_o_sb, w_o_fox, w_out, g_ffn]
    out_specs = [pl.BlockSpec((tm, d), tile)]
    out_shape = [jax.ShapeDtypeStruct((t, d), F32)]
    if with_router:
        rows = _local_rows(tm)
        in_specs += [pl.BlockSpec((d, LANES), full)] * 2
        args += list(w_router)
        out_specs += [pl.BlockSpec((1, rows, d), lambda i: (i, 0, 0)),
                      pl.BlockSpec((1, N_EXPERTS, LANES), lambda i: (i, 0, 0)),
                      pl.BlockSpec((tm, LANES), tile)]
        out_shape += [jax.ShapeDtypeStruct((t // tm, rows, d), BF16),
                      jax.ShapeDtypeStruct((t // tm, N_EXPERTS, LANES), F32),
                      jax.ShapeDtypeStruct((t, LANES), F32)]
    else:
        out_specs.append(pl.BlockSpec((tm, d), tile))
        out_shape.append(jax.ShapeDtypeStruct((t, d), BF16))
    return pl.pallas_call(
        functools.partial(_merge_kernel, with_router=with_router),
        grid=(t // tm,),
        in_specs=in_specs,
        out_specs=out_specs,
        out_shape=out_shape,
        compiler_params=_compiler_params(("parallel",)),
        name="merge_router" if with_router else "merge",
    )(*args)


def _swiglu_chunk(h, wg, wu, wd):
    g = _dot(h, wg)
    u = _dot(h, wu)
    return _dot((g * _sigmoid(g) * u).astype(BF16), wd)


def _ffn_kernel(h_ref, x_ref, wg_ref, wu_ref, wd_ref, o_ref):
    @pl.when(pl.program_id(1) == 0)
    def _():
        o_ref[...] = x_ref[...]

    o_ref[...] += _swiglu_chunk(h_ref[...], wg_ref[...], wu_ref[...], wd_ref[...])


def _ffn(h, x, w_gu, w_dn, tm, tf):
    t, d = x.shape
    d_ff = w_dn.shape[0]
    nf = d_ff // tf
    return pl.pallas_call(
        _ffn_kernel,
        grid=(t // tm, nf),
        in_specs=[
            pl.BlockSpec((tm, d), lambda i, j: (i, 0)),
            pl.BlockSpec((tm, d), lambda i, j: (i, 0)),
            pl.BlockSpec((d, tf), lambda i, j: (0, j)),
            pl.BlockSpec((d, tf), lambda i, j: (0, nf + j)),
            pl.BlockSpec((tf, d), lambda i, j: (j, 0)),
        ],
        out_specs=pl.BlockSpec((tm, d), lambda i, j: (i, 0)),
        out_shape=jax.ShapeDtypeStruct((t, d), F32),
        compiler_params=_compiler_params(("parallel", "arbitrary")),
        name="ffn",
    )(h, x, w_gu, w_gu, w_dn)


def _expert_kernel(expert_ref, valid_ref, src_ref, xs_hbm, wg_ref, wu_ref, wd_ref, o_ref, buf, sems):
    j = pl.program_id(0)
    rows = buf.shape[1]
    chunks = rows // MOE_CHUNK

    def fetches(step, slot):
        return [pltpu.make_async_copy(
            xs_hbm.at[src_ref[step * chunks + c]],
            buf.at[slot, pl.ds(c * MOE_CHUNK, MOE_CHUNK), :], sems.at[slot]) for c in range(chunks)]

    slot = lax.rem(j, 2)

    @pl.when(j == 0)
    def _():
        for copy in fetches(0, 0):
            copy.start()

    @pl.when(j + 1 < pl.num_programs(0))
    def _():
        for copy in fetches(j + 1, 1 - slot):
            copy.start()

    for copy in fetches(j, slot):
        copy.wait()

    @pl.when(valid_ref[j] != 0)
    def _():
        o_ref[...] = _swiglu_chunk(buf[slot], wg_ref[0], wu_ref[0], wd_ref[0]).astype(BF16)

    @pl.when(valid_ref[j] == 0)
    def _():
        o_ref[...] = jnp.zeros_like(o_ref)


def _experts(xs, tile_expert, tile_valid, chunk_src, w_gu, w_dn, n_rows):
    d = xs.shape[2]
    d_ff = w_dn.shape[1]
    te = MOE_EXPERT_TILE
    grid_spec = pltpu.PrefetchScalarGridSpec(
        num_scalar_prefetch=3,
        grid=(n_rows // te,),
        in_specs=[
            pl.BlockSpec(memory_space=pl.ANY),
            pl.BlockSpec((1, d, d_ff), lambda j, ex, va, sr: (ex[j], 0, 0)),
            pl.BlockSpec((1, d, d_ff), lambda j, ex, va, sr: (ex[j], 0, 1)),
            pl.BlockSpec((1, d_ff, d), lambda j, ex, va, sr: (ex[j], 0, 0)),
        ],
        out_specs=pl.BlockSpec((te, d), lambda j, ex, va, sr: (j, 0)),
        scratch_shapes=[pltpu.VMEM((2, te, d), BF16), pltpu.SemaphoreType.DMA((2,))],
    )
    return pl.pallas_call(
        _expert_kernel,
        grid_spec=grid_spec,
        out_shape=jax.ShapeDtypeStruct((n_rows, d), BF16),
        compiler_params=_compiler_params(("arbitrary",)),
        name="experts",
    )(tile_expert, tile_valid, chunk_src, xs, w_gu, w_gu, w_dn)


def _combine_kernel(goff_ref, count_ref, loff_ref, x_ref, route_ref, ys_hbm, o_ref, buf, sems):
    i = pl.program_id(0)
    slot = lax.rem(i, 2)

    def for_each_chunk(step, which, act):
        for e in range(N_EXPERTS):
            g = step * N_EXPERTS + e

            @pl.loop(0, count_ref[g])
            def _(c):
                dst = pl.multiple_of((loff_ref[g] + c) * MOE_CHUNK, MOE_CHUNK)
                act(pltpu.make_async_copy(ys_hbm.at[goff_ref[g] + c],
                                          buf.at[which, pl.ds(dst, MOE_CHUNK), :], sems.at[which]))

    @pl.when(i == 0)
    def _():
        buf[...] = jnp.zeros_like(buf)
        for_each_chunk(0, 0, lambda copy: copy.start())

    @pl.when(i + 1 < pl.num_programs(0))
    def _():
        for_each_chunk(i + 1, 1 - slot, lambda copy: copy.start())

    for_each_chunk(i, slot, lambda copy: copy.wait())

    route = route_ref[...]
    y = buf[slot]
    col = lax.broadcasted_iota(jnp.int32, (route.shape[0], buf.shape[1]), 1).astype(F32)
    out = x_ref[...]
    for k in range(TOP_K):
        pick = jnp.where(col == route[:, k:k + 1], 1.0, 0.0).astype(BF16)
        out = out + route[:, TOP_K + k:TOP_K + k + 1] * _dot(pick, y)
    o_ref[...] = out


def _combine(x, route, ys, goff, count, loff, tm):
    t, d = x.shape
    grid_spec = pltpu.PrefetchScalarGridSpec(
        num_scalar_prefetch=3,
        grid=(t // tm,),
        in_specs=[
            pl.BlockSpec((tm, d), lambda i, *_: (i, 0)),
            pl.BlockSpec((tm, LANES), lambda i, *_: (i, 0)),
            pl.BlockSpec(memory_space=pl.ANY),
        ],
        out_specs=pl.BlockSpec((tm, d), lambda i, *_: (i, 0)),
        scratch_shapes=[pltpu.VMEM((2, _local_rows(tm), d), BF16), pltpu.SemaphoreType.DMA((2,))],
    )
    return pl.pallas_call(
        _combine_kernel,
        grid_spec=grid_spec,
        out_shape=jax.ShapeDtypeStruct((t, d), F32),
        compiler_params=_compiler_params(("arbitrary",)),
        name="combine",
    )(goff, count, loff, x, route, ys)


def _moe(x1, xs_local, padded_len, route, w_gu, w_dn, tm):
    t, d = x1.shape
    n_tiles, local_rows, _ = xs_local.shape
    te = MOE_EXPERT_TILE
    n_rows = -(-(TOP_K * t + n_tiles * N_EXPERTS * (MOE_CHUNK - 1) + N_EXPERTS * (te - 1)) // te) * te
    length = padded_len[:, :, 0].astype(jnp.int32)
    count = length // MOE_CHUNK
    used = jnp.sum(length, axis=0)
    span = -(-used // te) * te
    base = jnp.cumsum(span) - span
    goff = (base[None, :] + jnp.cumsum(length, axis=0) - length) // MOE_CHUNK
    loff = (jnp.cumsum(length, axis=1) - length) // MOE_CHUNK
    seg_start = goff.T.reshape(-1)
    seg_count = count.T.reshape(-1)
    seg_src = (jnp.arange(n_tiles, dtype=jnp.int32)[:, None] * (local_rows // MOE_CHUNK) + loff).T.reshape(-1)
    chunk = jnp.arange(n_rows // MOE_CHUNK, dtype=jnp.int32)[:, None]
    within = chunk - seg_start[None, :]
    owner = (within >= 0) & (within < seg_count[None, :])
    chunk_src = jnp.sum(jnp.where(owner, seg_src[None, :] + within, 0), axis=1).astype(jnp.int32)
    tile_row = jnp.arange(n_rows // te, dtype=jnp.int32)[:, None] * te
    tile_expert = (jnp.sum(base[None, :] <= tile_row, axis=1) - 1).astype(jnp.int32)
    tile_valid = jnp.any((base[None, :] <= tile_row) & (tile_row < (base + used)[None, :]),
                         axis=1).astype(jnp.int32)
    ys = _experts(xs_local.reshape(-1, MOE_CHUNK, d), tile_expert, tile_valid, chunk_src, w_gu, w_dn, n_rows)
    return _combine(x1, route, ys.reshape(-1, MOE_CHUNK, d), goff.reshape(-1), count.reshape(-1),
                    loff.reshape(-1), tm)


def _pad_lanes(w):
    return jnp.pad(w, ((0, 0), (0, LANES - w.shape[1])))


def kernel(x, g_mix, w_in, b_f, b_gate, g_q, g_k, w_o_sb, w_o_fox, w_out, g_ffn, w_gu_dense,
           w_dn_dense, w_router, w_gu_exp, w_dn_exp):
    batch, seq, d = x.shape
    depth = w_in.shape[0]
    t = batch * seq
    tm_proj = min(512, t)
    tm_ffn = min(1024, t)
    n_qkv = 6 * W_BRANCH
    xt = x.reshape(t, d)
    for l in range(depth):
        w_l = w_in[l]
        w_proj = jnp.concatenate(
            [w_l[:, :n_qkv], _pad_lanes(w_l[:, n_qkv:n_qkv + N_HEADS])], axis=1).astype(BF16)
        w_gate = w_l[:, n_qkv + N_HEADS:].astype(BF16)
        row = lambda v: v.reshape(1, -1)
        gq = jnp.tile(g_q[l], N_HEADS).reshape(1, W_BRANCH)
        gk = jnp.tile(g_k[l], N_HEADS).reshape(1, W_BRANCH)
        qkv_sb, qkv_fx, f_pre = _proj(xt, row(g_mix[l]), w_proj, gq, gk, tm_proj)
        cum_f = _cumf(f_pre.reshape(batch, seq, LANES), _pad_lanes(row(b_f[l])))
        y_sb = _attention(_sb_kernel, qkv_sb.reshape(batch, seq, 3 * W_BRANCH))
        zmax = (QK_BOUND_MARGIN * 8.0 * jnp.max(jnp.abs(g_q[l])) * jnp.max(jnp.abs(g_k[l]))).reshape(1)
        y_fx = _attention(_fox_kernel, qkv_fx.reshape(batch, seq, 3 * W_BRANCH), cum_f, zmax)
        moe_layer = l % 2 == 1
        router = None
        if moe_layer:
            wr = _pad_lanes(w_router[l // 2])
            router = _split_bf16(wr, 2)
        outs = _merge(xt, y_sb.reshape(t, W_BRANCH), y_fx.reshape(t, W_BRANCH), row(g_mix[l]),
                      w_gate, row(b_gate[l]), w_o_sb[l].astype(BF16), w_o_fox[l].astype(BF16),
                      w_out[l].astype(BF16), row(g_ffn[l]), router, tm_proj)
        if moe_layer:
            x1, xs_local, padded_len, route = outs
            xt = _moe(x1, xs_local, padded_len, route, w_gu_exp[l // 2].astype(BF16),
                      w_dn_exp[l // 2].astype(BF16), tm_proj)
        else:
            x1, h2 = outs
            xt = _ffn(h2, x1, w_gu_dense[l // 2].astype(BF16), w_dn_dense[l // 2].astype(BF16),
                      tm_ffn, 256)
    return xt.reshape(batch, seq, d)
```

```python
import functools

import jax
import jax.numpy as jnp
from jax import lax
from jax.experimental import pallas as pl
from jax.experimental.pallas import tpu as pltpu

F32 = jnp.float32
BF16 = jnp.bfloat16

HEAD_DIM = 64
N_HEADS = 8
W_BRANCH = N_HEADS * HEAD_DIM
N_EXPERTS = 8
TOP_K = 2
MOE_CHUNK = 16
MOE_EXPERT_TILE = 512
RMS_EPS = 1e-6
QK_SCALE = 1.0 / 8.0
LANES = 128
HEADS_PER_BLOCK = LANES // HEAD_DIM
V7X_VMEM_LIMIT_BYTES = 56 * 1024 * 1024

ATT_BLOCK = 256
ATT_LANE_BLOCKS = 4
F32_EXP_UNDERFLOW = 106.0
FOX_CONSTANT_SHIFT_MAX = 30.0
QK_BOUND_MARGIN = 1.05
CUM_BLOCK = 512


def _split_bf16(x, parts):
    out = []
    for _ in range(parts - 1):
        hi = x.astype(BF16)
        out.append(hi)
        x = x - hi.astype(F32)
    out.append(x.astype(BF16))
    return out


def _dot(a, b):
    return jnp.dot(a, b, preferred_element_type=F32)


def _dot_nt(a, b):
    return lax.dot_general(a, b, (((1,), (1,)), ((), ())), preferred_element_type=F32)


def _rmsnorm(x, g):
    ms = jnp.mean(x * x, axis=-1, keepdims=True)
    return x * lax.rsqrt(ms + RMS_EPS) * g


def _sigmoid(x):
    return 1.0 / (1.0 + jnp.exp(-x))


def _compiler_params(semantics):
    return pltpu.CompilerParams(dimension_semantics=semantics,
                                vmem_limit_bytes=V7X_VMEM_LIMIT_BYTES)


def _proj_kernel(x_ref, g_ref, w_ref, gq_ref, gk_ref, sb_ref, fx_ref, f_ref):
    h = _rmsnorm(x_ref[...], g_ref[...]).astype(BF16)
    wb = W_BRANCH
    r = lax.broadcasted_iota(jnp.int32, (wb, wb), 0) // HEAD_DIM
    c = lax.broadcasted_iota(jnp.int32, (wb, wb), 1) // HEAD_DIM
    head_mean = jnp.where(r == c, 1.0 / HEAD_DIM, 0.0).astype(BF16)
    for i in range(3):
        acc = _dot(h, w_ref[:, i * wb:(i + 1) * wb])
        if i == 0:
            acc = acc * QK_SCALE
        sb_ref[:, i * wb:(i + 1) * wb] = acc.astype(BF16)
    for i, gain_ref in enumerate((gq_ref, gk_ref, None)):
        acc = _dot(h, w_ref[:, (3 + i) * wb:(4 + i) * wb])
        if gain_ref is not None:
            sq_hi, sq_lo = _split_bf16(acc * acc, 2)
            ms = _dot(sq_hi, head_mean) + _dot(sq_lo, head_mean)
            acc = acc * lax.rsqrt(ms + RMS_EPS) * gain_ref[...]
        if i == 0:
            acc = acc * QK_SCALE
        fx_ref[:, i * wb:(i + 1) * wb] = acc.astype(BF16)
    f_ref[...] = _dot(h, w_ref[:, 6 * wb:6 * wb + LANES])


def _proj(x, g, w, gq, gk, tm):
    t, d = x.shape
    n = w.shape[1]
    full = lambda i: (0, 0)
    return pl.pallas_call(
        _proj_kernel,
        grid=(t // tm,),
        in_specs=[
            pl.BlockSpec((tm, d), lambda i: (i, 0)),
            pl.BlockSpec((1, d), full),
            pl.BlockSpec((d, n), full),
            pl.BlockSpec((1, W_BRANCH), full),
            pl.BlockSpec((1, W_BRANCH), full),
        ],
        out_specs=[
            pl.BlockSpec((tm, 3 * W_BRANCH), lambda i: (i, 0)),
            pl.BlockSpec((tm, 3 * W_BRANCH), lambda i: (i, 0)),
            pl.BlockSpec((tm, LANES), lambda i: (i, 0)),
        ],
        out_shape=[
            jax.ShapeDtypeStruct((t, 3 * W_BRANCH), BF16),
            jax.ShapeDtypeStruct((t, 3 * W_BRANCH), BF16),
            jax.ShapeDtypeStruct((t, LANES), F32),
        ],
        compiler_params=_compiler_params(("parallel",)),
        name="proj",
    )(x, g, w, gq, gk)


def _cumf_kernel(f_ref, b_ref, o_ref):
    seq = f_ref.shape[1]
    cb = min(CUM_BLOCK, seq)
    r = lax.broadcasted_iota(jnp.int32, (cb, cb), 0)
    c = lax.broadcasted_iota(jnp.int32, (cb, cb), 1)
    prefix = jnp.where(r <= c, 1.0, 0.0).astype(BF16)
    carry = jnp.zeros((N_HEADS, 1), F32)
    for i in range(seq // cb):
        v = f_ref[0, i * cb:(i + 1) * cb, :] + b_ref[...]
        log_f = jnp.minimum(v, 0.0) - jnp.log1p(jnp.exp(-jnp.abs(v)))
        log_f = log_f.T[:N_HEADS, :]
        cum = carry
        for part in _split_bf16(log_f, 3):
            cum = cum + _dot(part, prefix)
        o_ref[0, :, i * cb:(i + 1) * cb] = cum
        carry = cum[:, cb - 1:cb]


def _cumf(f_pre, b_f):
    b, s, _ = f_pre.shape
    return pl.pallas_call(
        _cumf_kernel,
        grid=(b,),
        in_specs=[
            pl.BlockSpec((1, s, LANES), lambda i: (i, 0, 0)),
            pl.BlockSpec((1, LANES), lambda i: (0, 0)),
        ],
        out_specs=pl.BlockSpec((1, N_HEADS, s), lambda i: (i, 0, 0)),
        out_shape=jax.ShapeDtypeStruct((b, N_HEADS, s), F32),
        compiler_params=_compiler_params(("parallel",)),
        name="cumf",
    )(f_pre, b_f)


def _head_queries(q_ref, half):
    q = q_ref[0, :, half * LANES:(half + 1) * LANES].astype(F32)
    lane = lax.broadcasted_iota(jnp.int32, (1, LANES), 1)
    return [jnp.where((lane >= hh * HEAD_DIM) & (lane < (hh + 1) * HEAD_DIM), q, 0.0).astype(BF16)
            for hh in range(HEADS_PER_BLOCK)]


def _run_chains(q_ref, k_ref, v_ref, o_ref, stages, finish, stop):
    bk = ATT_BLOCK
    halves = range(q_ref.shape[2] // LANES)
    qi = pl.program_id(2)
    queries = [q for half in halves for q in _head_queries(q_ref, half)]

    def advance(kb, masked, states):
        start = pl.multiple_of(kb * bk, bk)
        kv = [(k_ref[0, pl.ds(start, bk), half * LANES:(half + 1) * LANES],
               v_ref[0, pl.ds(start, bk), half * LANES:(half + 1) * LANES]) for half in halves]
        args = [(head, q, *kv[head // HEADS_PER_BLOCK], start, masked)
                for head, q in enumerate(queries)]
        tmps = [None] * len(args)
        for stage in stages[:-1]:
            tmps = [stage(*a, tmp) for a, tmp in zip(args, tmps)]
        return tuple(stages[-1](*a, tmp, state) for a, tmp, state in zip(args, tmps, states))

    def body(carry):
        i, _, states = carry
        kb = qi - 1 - i
        states = advance(kb, False, states)
        return i + 1, stop(states, jnp.maximum(kb - 1, 0)).astype(jnp.int32), states

    states = advance(qi, True, [None] * len(queries))
    _, _, states = lax.while_loop(lambda c: jnp.logical_and(c[0] < qi, c[1] == 0), body,
                                  (jnp.int32(0), jnp.int32(0), states))
    lane = lax.broadcasted_iota(jnp.int32, (1, LANES), 1)
    for half in halves:
        outs = [finish(head, states[head])
                for head in range(half * HEADS_PER_BLOCK, (half + 1) * HEADS_PER_BLOCK)]
        o_ref[0, :, half * LANES:(half + 1) * LANES] = jnp.where(
            lane < HEAD_DIM, outs[0], outs[1]).astype(BF16)


def _sb_kernel(q_ref, k_ref, v_ref, o_ref):
    bk = ATT_BLOCK
    row = lax.broadcasted_iota(jnp.int32, (2 * bk, bk), 0)
    col = lax.broadcasted_iota(jnp.int32, (2 * bk, bk), 1)
    suffix2 = jnp.where((row >= col) & ((row < bk) | (row - bk >= col)), 1.0, 0.0).astype(BF16)
    below_diag = (lax.broadcasted_iota(jnp.int32, (bk, bk), 1)
                  < lax.broadcasted_iota(jnp.int32, (bk, bk), 0))

    def scores(head, q, k, v, start, masked, tmp):
        z = _dot_nt(q, k)
        neg_abs = lax.bitcast_convert_type(
            lax.bitcast_convert_type(z, jnp.uint32) | jnp.uint32(0x80000000), F32)
        sp = jnp.maximum(z, 0.0) + jnp.log(1.0 + jnp.exp(neg_abs))
        if masked:
            sp = jnp.where(below_diag, sp, 0.0)
        return z, jnp.concatenate(_split_bf16(sp, 2), axis=1)

    def suffix_sums(head, q, k, v, start, masked, tmp):
        z, sp_parts = tmp
        return z, _dot(sp_parts, suffix2)

    def weights(head, q, k, v, start, masked, tmp, state):
        z, incl = tmp
        arg = z - incl
        if state is not None:
            arg = arg - state[0]
        w = jnp.exp(arg)
        if masked:
            w = jnp.where(below_diag, w, 0.0)
        pv = _dot(w.astype(BF16), v)
        if state is None:
            return incl[:, 0:1], pv
        return state[0] + incl[:, 0:1], state[1] + pv

    def stop(states, next_block):
        carry = functools.reduce(jnp.minimum, [state[0] for state in states])
        return jnp.min(carry) > F32_EXP_UNDERFLOW

    _run_chains(q_ref, k_ref, v_ref, o_ref, (scores, suffix_sums, weights),
                lambda head, state: state[1], stop)


def _fox_kernel(zmax_ref, fend_ref, q_ref, k_ref, v_ref, f_ref, fcol_ref, o_ref):
    bk = ATT_BLOCK
    n_blocks = k_ref.shape[1] // bk
    n_heads = f_ref.shape[2]
    qi = pl.program_id(2)
    row = lax.broadcasted_iota(jnp.int32, (bk, bk), 0)
    col = lax.broadcasted_iota(jnp.int32, (bk, bk), 1)
    causal = col <= row
    lane = lax.broadcasted_iota(jnp.int32, (1, LANES), 1)
    table = (pl.program_id(0) * (W_BRANCH // HEAD_DIM) + pl.program_id(1) * n_heads) * n_blocks

    def ones_lanes_cache():
        values = {}

        def values_with_ones(hh, v):
            if (hh, id(v)) not in values:
                own = (lane >= hh * HEAD_DIM) & (lane < (hh + 1) * HEAD_DIM)
                values[(hh, id(v))] = (v, jnp.where(own, v.astype(F32), 1.0).astype(BF16))
            return values[(hh, id(v))][1]

        return values_with_ones

    def normalised(head, acc):
        ones_lane = (1 - head % HEADS_PER_BLOCK) * HEAD_DIM
        return acc / acc[:, ones_lane:ones_lane + 1]

    @pl.when(zmax_ref[0] <= FOX_CONSTANT_SHIFT_MAX)
    def _():
        values_with_ones = ones_lanes_cache()
        shift = [fcol_ref[0, :, head:head + 1] - zmax_ref[0] for head in range(n_heads)]

        def weights(head, q, k, v, start, masked, tmp):
            s = _dot_nt(q, k) - f_ref[0, 0, head:head + 1, pl.ds(start, bk)] + shift[head]
            if masked:
                s = jnp.where(causal, s, -jnp.inf)
            return jnp.exp(s).astype(BF16)

        def accumulate(head, q, k, v, start, masked, tmp, state):
            pv = _dot(tmp, values_with_ones(head % HEADS_PER_BLOCK, v))
            return pv if state is None else state + pv

        def stop(states, next_block):
            last = jnp.maximum(qi - 1, 0)
            gaps = [fend_ref[table + head * n_blocks + next_block] - fend_ref[table + head * n_blocks + last]
                    for head in range(n_heads)]
            return functools.reduce(jnp.maximum, gaps) < -F32_EXP_UNDERFLOW

        _run_chains(q_ref, k_ref, v_ref, o_ref, (weights, accumulate), normalised, stop)

    @pl.when(zmax_ref[0] > FOX_CONSTANT_SHIFT_MAX)
    def _():
        values_with_ones = ones_lanes_cache()

        def logits(head, q, k, v, start, masked, tmp):
            s = _dot_nt(q, k) - f_ref[0, 0, head:head + 1, pl.ds(start, bk)]
            if masked:
                s = jnp.where(causal, s, -jnp.inf)
            return s, jnp.max(s, axis=1, keepdims=True)

        def accumulate(head, q, k, v, start, masked, tmp, state):
            s, m_blk = tmp
            v_ones = values_with_ones(head % HEADS_PER_BLOCK, v)
            if state is None:
                return m_blk, _dot(jnp.exp(s - m_blk).astype(BF16), v_ones)
            m, acc = state
            m_new = jnp.maximum(m, m_blk)
            pv = _dot(jnp.exp(s - m_new).astype(BF16), v_ones)
            return m_new, jnp.exp(m - m_new) * acc + pv

        def stop(states, next_block):
            slack = functools.reduce(jnp.minimum, [
                state[0] - fend_ref[table + head * n_blocks + next_block]
                for head, state in enumerate(states)])
            return jnp.min(slack) > zmax_ref[0] + F32_EXP_UNDERFLOW

        _run_chains(q_ref, k_ref, v_ref, o_ref, (logits, accumulate),
                    lambda head, state: normalised(head, state[1]), stop)


def _attention(kernel, qkv, cum_f=None, zmax=None):
    b, s, _ = qkv.shape
    bq = min(ATT_BLOCK, s)
    width = ATT_LANE_BLOCKS * LANES
    n_groups = W_BRANCH // width
    in_specs = [
        pl.BlockSpec((1, bq, width), lambda bi, g, qi: (bi, qi, g)),
        pl.BlockSpec((1, s, width), lambda bi, g, qi: (bi, 0, n_groups + g)),
        pl.BlockSpec((1, s, width), lambda bi, g, qi: (bi, 0, 2 * n_groups + g)),
    ]
    args = [qkv, qkv, qkv]
    if cum_f is not None:
        heads = ATT_LANE_BLOCKS * HEADS_PER_BLOCK
        smem = pl.BlockSpec(memory_space=pltpu.SMEM)
        in_specs = [smem, smem] + in_specs + [
            pl.BlockSpec((1, 1, heads, s), lambda bi, g, qi: (bi, g, 0, 0)),
            pl.BlockSpec((1, bq, heads), lambda bi, g, qi: (bi * n_groups + g, qi, 0))]
        f_end = -cum_f[:, :, bq - 1::bq]
        by_head = cum_f.reshape(b, n_groups, heads, s)
        args = [zmax, f_end.reshape(-1)] + args + [
            by_head, jnp.swapaxes(by_head, 2, 3).reshape(b * n_groups, s, heads)]
    return pl.pallas_call(
        kernel,
        grid=(b, n_groups, s // bq),
        in_specs=in_specs,
        out_specs=pl.BlockSpec((1, bq, width), lambda bi, g, qi: (bi, qi, g)),
        out_shape=jax.ShapeDtypeStruct((b, s, W_BRANCH), BF16),
        compiler_params=_compiler_params(("parallel", "parallel", "arbitrary")),
        name=kernel.__name__.strip("_"),
    )(*args)


def _local_rows(tm):
    return TOP_K * tm + N_EXPERTS * MOE_CHUNK


def _route_and_sort(logits, h2, xs_ref, len_ref, route_ref):
    tm = logits.shape[0]
    lt = logits.T[:N_EXPERTS, :]
    expert = lax.broadcasted_iota(jnp.int32, lt.shape, 0).astype(F32)
    m1 = jnp.max(lt, axis=0, keepdims=True)
    i1 = jnp.min(jnp.where(lt == m1, expert, float(N_EXPERTS)), axis=0, keepdims=True)
    lt2 = jnp.where(expert == i1, -jnp.inf, lt)
    m2 = jnp.max(lt2, axis=0, keepdims=True)
    i2 = jnp.min(jnp.where(lt2 == m2, expert, float(N_EXPERTS)), axis=0, keepdims=True)
    e2 = jnp.exp(m2 - m1)
    w1 = 1.0 / (1.0 + e2)
    w2 = e2 * w1
    chosen = jnp.where((expert == i1) | (expert == i2), 1.0, 0.0)
    before = (lax.broadcasted_iota(jnp.int32, (tm, tm), 0)
              < lax.broadcasted_iota(jnp.int32, (tm, tm), 1))
    rank = _dot(chosen.astype(BF16), jnp.where(before, 1.0, 0.0).astype(BF16))
    count = jnp.sum(chosen, axis=1, keepdims=True)
    padded = jnp.floor((count + (MOE_CHUNK - 1)) * (1.0 / MOE_CHUNK)) * MOE_CHUNK
    r1 = jnp.sum(jnp.where(expert == i1, rank, 0.0), axis=0, keepdims=True)
    r2 = jnp.sum(jnp.where(expert == i2, rank, 0.0), axis=0, keepdims=True)
    start = jnp.zeros((1, 1), F32)
    for e in range(N_EXPERTS):
        r1 = r1 + jnp.where(i1 == e, start, 0.0)
        r2 = r2 + jnp.where(i2 == e, start, 0.0)
        start = start + padded[e:e + 1, :]
    rows = _local_rows(tm)
    row = lax.broadcasted_iota(jnp.int32, (rows, tm), 0).astype(F32)
    place = jnp.where((row == r1) | (row == r2), 1.0, 0.0).astype(BF16)
    xs_ref[0] = _dot(place, h2).astype(BF16)
    len_ref[0] = jnp.broadcast_to(padded, (N_EXPERTS, LANES))
    field = lax.broadcasted_iota(jnp.int32, (LANES, tm), 0)
    fields = jnp.where(field == 0, r1, jnp.where(field == 1, r2, jnp.where(
        field == 2, w1, jnp.where(field == 3, w2, 0.0))))
    route_ref[...] = fields.T


def _merge_kernel(x_ref, ysb_ref, yfx_ref, gmix_ref, wgate_ref, bgate_ref, wosb_ref, wofx_ref,
                  wout_ref, gffn_ref, *rest, with_router):
    if with_router:
        wr_hi_ref, wr_lo_ref, x1_ref, xs_ref, len_ref, route_ref = rest
    else:
        x1_ref, h2_ref = rest
    d = x_ref.shape[1]
    x = x_ref[...]
    h = _rmsnorm(x, gmix_ref[...]).astype(BF16)
    merged = None
    for i, (y_ref, wo_ref) in enumerate(((ysb_ref, wosb_ref), (yfx_ref, wofx_ref))):
        gate = _sigmoid(_dot(h, wgate_ref[:, i * d:(i + 1) * d]) + bgate_ref[:, i * d:(i + 1) * d])
        branch = gate * _dot(y_ref[...], wo_ref[...])
        merged = branch if merged is None else merged + branch
    x1 = x + _dot(merged.astype(BF16), wout_ref[...])
    x1_ref[...] = x1
    h2 = _rmsnorm(x1, gffn_ref[...])
    if with_router:
        h_hi, h_lo = _split_bf16(h2, 2)
        logits = _dot(h_hi, wr_hi_ref[...]) + _dot(h_lo, wr_hi_ref[...]) + _dot(h_hi, wr_lo_ref[...])
        _route_and_sort(logits, h_hi, xs_ref, len_ref, route_ref)
    else:
        h2_ref[...] = h2.astype(BF16)


def _merge(x, y_sb, y_fx, g_mix, w_gate, b_gate, w_o_sb, w_o_fox, w_out, g_ffn, w_router, tm):
    t, d = x.shape
    with_router = w_router is not None
    full = lambda i: (0, 0)
    tile = lambda i: (i, 0)
    in_specs = [
        pl.BlockSpec((tm, d), tile),
        pl.BlockSpec((tm, W_BRANCH), tile),
        pl.BlockSpec((tm, W_BRANCH), tile),
        pl.BlockSpec((1, d), full),
        pl.BlockSpec((d, 2 * d), full),
        pl.BlockSpec((1, 2 * d), full),
        pl.BlockSpec((W_BRANCH, d), full),
        pl.BlockSpec((W_BRANCH, d), full),
        pl.BlockSpec((d, d), full),
        pl.BlockSpec((1, d), full),
    ]
    args = [x, y_sb, y_fx, g_mix, w_gate, b_gate, w_o_sb, w_o_fox, w_out, g_ffn]
    out_specs = [pl.BlockSpec((tm, d), tile)]
    out_shape = [jax.ShapeDtypeStruct((t, d), F32)]
    if with_router:
        rows = _local_rows(tm)
        in_specs += [pl.BlockSpec((d, LANES), full)] * 2
        args += list(w_router)
        out_specs += [pl.BlockSpec((1, rows, d), lambda i: (i, 0, 0)),
                      pl.BlockSpec((1, N_EXPERTS, LANES), lambda i: (i, 0, 0)),
                      pl.BlockSpec((tm, LANES), tile)]
        out_shape += [jax.ShapeDtypeStruct((t // tm, rows, d), BF16),
                      jax.ShapeDtypeStruct((t // tm, N_EXPERTS, LANES), F32),
                      jax.ShapeDtypeStruct((t, LANES), F32)]
    else:
        out_specs.append(pl.BlockSpec((tm, d), tile))
        out_shape.append(jax.ShapeDtypeStruct((t, d), BF16))
    return pl.pallas_call(
        functools.partial(_merge_kernel, with_router=with_router),
        grid=(t // tm,),
        in_specs=in_specs,
        out_specs=out_specs,
        out_shape=out_shape,
        compiler_params=_compiler_params(("parallel",)),
        name="merge_router" if with_router else "merge",
    )(*args)


def _swiglu_chunk(h, wg, wu, wd):
    g = _dot(h, wg)
    u = _dot(h, wu)
    return _dot((g * _sigmoid(g) * u).astype(BF16), wd)


def _ffn_kernel(h_ref, x_ref, wg_ref, wu_ref, wd_ref, o_ref):
    @pl.when(pl.program_id(1) == 0)
    def _():
        o_ref[...] = x_ref[...]

    o_ref[...] += _swiglu_chunk(h_ref[...], wg_ref[...], wu_ref[...], wd_ref[...])


def _ffn(h, x, w_gu, w_dn, tm, tf):
    t, d = x.shape
    d_ff = w_dn.shape[0]
    nf = d_ff // tf
    return pl.pallas_call(
        _ffn_kernel,
        grid=(t // tm, nf),
        in_specs=[
            pl.BlockSpec((tm, d), lambda i, j: (i, 0)),
            pl.BlockSpec((tm, d), lambda i, j: (i, 0)),
            pl.BlockSpec((d, tf), lambda i, j: (0, j)),
            pl.BlockSpec((d, tf), lambda i, j: (0, nf + j)),
            pl.BlockSpec((tf, d), lambda i, j: (j, 0)),
        ],
        out_specs=pl.BlockSpec((tm, d), lambda i, j: (i, 0)),
        out_shape=jax.ShapeDtypeStruct((t, d), F32),
        compiler_params=_compiler_params(("parallel", "arbitrary")),
        name="ffn",
    )(h, x, w_gu, w_gu, w_dn)


def _expert_kernel(expert_ref, valid_ref, src_ref, xs_hbm, wg_ref, wu_ref, wd_ref, o_ref, buf, sems):
    j = pl.program_id(0)
    rows = buf.shape[1]
    chunks = rows // MOE_CHUNK

    def fetches(step, slot):
        return [pltpu.make_async_copy(
            xs_hbm.at[src_ref[step * chunks + c]],
            buf.at[slot, pl.ds(c * MOE_CHUNK, MOE_CHUNK), :], sems.at[slot]) for c in range(chunks)]

    slot = lax.rem(j, 2)

    @pl.when(j == 0)
    def _():
        for copy in fetches(0, 0):
            copy.start()

    @pl.when(j + 1 < pl.num_programs(0))
    def _():
        for copy in fetches(j + 1, 1 - slot):
            copy.start()

    for copy in fetches(j, slot):
        copy.wait()

    @pl.when(valid_ref[j] != 0)
    def _():
        o_ref[...] = _swiglu_chunk(buf[slot], wg_ref[0], wu_ref[0], wd_ref[0]).astype(BF16)

    @pl.when(valid_ref[j] == 0)
    def _():
        o_ref[...] = jnp.zeros_like(o_ref)


def _experts(xs, tile_expert, tile_valid, chunk_src, w_gu, w_dn, n_rows):
    d = xs.shape[2]
    d_ff = w_dn.shape[1]
    te = MOE_EXPERT_TILE
    grid_spec = pltpu.PrefetchScalarGridSpec(
        num_scalar_prefetch=3,
        grid=(n_rows // te,),
        in_specs=[
            pl.BlockSpec(memory_space=pl.ANY),
            pl.BlockSpec((1, d, d_ff), lambda j, ex, va, sr: (ex[j], 0, 0)),
            pl.BlockSpec((1, d, d_ff), lambda j, ex, va, sr: (ex[j], 0, 1)),
            pl.BlockSpec((1, d_ff, d), lambda j, ex, va, sr: (ex[j], 0, 0)),
        ],
        out_specs=pl.BlockSpec((te, d), lambda j, ex, va, sr: (j, 0)),
        scratch_shapes=[pltpu.VMEM((2, te, d), BF16), pltpu.SemaphoreType.DMA((2,))],
    )
    return pl.pallas_call(
        _expert_kernel,
        grid_spec=grid_spec,
        out_shape=jax.ShapeDtypeStruct((n_rows, d), BF16),
        compiler_params=_compiler_params(("arbitrary",)),
        name="experts",
    )(tile_expert, tile_valid, chunk_src, xs, w_gu, w_gu, w_dn)


def _combine_kernel(goff_ref, count_ref, loff_ref, x_ref, route_ref, ys_hbm, o_ref, buf, sems):
    i = pl.program_id(0)
    slot = lax.rem(i, 2)

    def for_each_chunk(step, which, act):
        for e in range(N_EXPERTS):
            g = step * N_EXPERTS + e

            @pl.loop(0, count_ref[g])
            def _(c):
                dst = pl.multiple_of((loff_ref[g] + c) * MOE_CHUNK, MOE_CHUNK)
                act(pltpu.make_async_copy(ys_hbm.at[goff_ref[g] + c],
                                          buf.at[which, pl.ds(dst, MOE_CHUNK), :], sems.at[which]))

    @pl.when(i == 0)
    def _():
        buf[...] = jnp.zeros_like(buf)
        for_each_chunk(0, 0, lambda copy: copy.start())

    @pl.when(i + 1 < pl.num_programs(0))
    def _():
        for_each_chunk(i + 1, 1 - slot, lambda copy: copy.start())

    for_each_chunk(i, slot, lambda copy: copy.wait())

    route = route_ref[...]
    y = buf[slot]
    col = lax.broadcasted_iota(jnp.int32, (route.shape[0], buf.shape[1]), 1).astype(F32)
    out = x_ref[...]
    for k in range(TOP_K):
        pick = jnp.where(col == route[:, k:k + 1], 1.0, 0.0).astype(BF16)
        out = out + route[:, TOP_K + k:TOP_K + k + 1] * _dot(pick, y)
    o_ref[...] = out


def _combine(x, route, ys, goff, count, loff, tm):
    t, d = x.shape
    grid_spec = pltpu.PrefetchScalarGridSpec(
        num_scalar_prefetch=3,
        grid=(t // tm,),
        in_specs=[
            pl.BlockSpec((tm, d), lambda i, *_: (i, 0)),
            pl.BlockSpec((tm, LANES), lambda i, *_: (i, 0)),
            pl.BlockSpec(memory_space=pl.ANY),
        ],
        out_specs=pl.BlockSpec((tm, d), lambda i, *_: (i, 0)),
        scratch_shapes=[pltpu.VMEM((2, _local_rows(tm), d), BF16), pltpu.SemaphoreType.DMA((2,))],
    )
    return pl.pallas_call(
        _combine_kernel,
        grid_spec=grid_spec,
        out_shape=jax.ShapeDtypeStruct((t, d), F32),
        compiler_params=_compiler_params(("arbitrary",)),
        name="combine",
    )(goff, count, loff, x, route, ys)


def _moe(x1, xs_local, padded_len, route, w_gu, w_dn, tm):
    t, d = x1.shape
    n_tiles, local_rows, _ = xs_local.shape
    te = MOE_EXPERT_TILE
    n_rows = -(-(TOP_K * t + n_tiles * N_EXPERTS * (MOE_CHUNK - 1) + N_EXPERTS * (te - 1)) // te) * te
    length = padded_len[:, :, 0].astype(jnp.int32)
    count = length // MOE_CHUNK
    used = jnp.sum(length, axis=0)
    span = -(-used // te) * te
    base = jnp.cumsum(span) - span
    goff = (base[None, :] + jnp.cumsum(length, axis=0) - length) // MOE_CHUNK
    loff = (jnp.cumsum(length, axis=1) - length) // MOE_CHUNK
    seg_start = goff.T.reshape(-1)
    seg_count = count.T.reshape(-1)
    seg_src = (jnp.arange(n_tiles, dtype=jnp.int32)[:, None] * (local_rows // MOE_CHUNK) + loff).T.reshape(-1)
    chunk = jnp.arange(n_rows // MOE_CHUNK, dtype=jnp.int32)[:, None]
    within = chunk - seg_start[None, :]
    owner = (within >= 0) & (within < seg_count[None, :])
    chunk_src = jnp.sum(jnp.where(owner, seg_src[None, :] + within, 0), axis=1).astype(jnp.int32)
    tile_row = jnp.arange(n_rows // te, dtype=jnp.int32)[:, None] * te
    tile_expert = (jnp.sum(base[None, :] <= tile_row, axis=1) - 1).astype(jnp.int32)
    tile_valid = jnp.any((base[None, :] <= tile_row) & (tile_row < (base + used)[None, :]),
                         axis=1).astype(jnp.int32)
    ys = _experts(xs_local.reshape(-1, MOE_CHUNK, d), tile_expert, tile_valid, chunk_src, w_gu, w_dn, n_rows)
    return _combine(x1, route, ys.reshape(-1, MOE_CHUNK, d), goff.reshape(-1), count.reshape(-1),
                    loff.reshape(-1), tm)


def _pad_lanes(w):
    return jnp.pad(w, ((0, 0), (0, LANES - w.shape[1])))


def kernel(x, g_mix, w_in, b_f, b_gate, g_q, g_k, w_o_sb, w_o_fox, w_out, g_ffn, w_gu_dense,
           w_dn_dense, w_router, w_gu_exp, w_dn_exp):
    batch, seq, d = x.shape
    depth = w_in.shape[0]
    t = batch * seq
    tm_proj = min(512, t)
    tm_ffn = min(1024, t)
    n_qkv = 6 * W_BRANCH
    xt = x.reshape(t, d)
    for l in range(depth):
        w_l = w_in[l]
        w_proj = jnp.concatenate(
            [w_l[:, :n_qkv], _pad_lanes(w_l[:, n_qkv:n_qkv + N_HEADS])], axis=1).astype(BF16)
        w_gate = w_l[:, n_qkv + N_HEADS:].astype(BF16)
        row = lambda v: v.reshape(1, -1)
        gq = jnp.tile(g_q[l], N_HEADS).reshape(1, W_BRANCH)
        gk = jnp.tile(g_k[l], N_HEADS).reshape(1, W_BRANCH)
        qkv_sb, qkv_fx, f_pre = _proj(xt, row(g_mix[l]), w_proj, gq, gk, tm_proj)
        cum_f = _cumf(f_pre.reshape(batch, seq, LANES), _pad_lanes(row(b_f[l])))
        y_sb = _attention(_sb_kernel, qkv_sb.reshape(batch, seq, 3 * W_BRANCH))
        zmax = (QK_BOUND_MARGIN * 8.0 * jnp.max(jnp.abs(g_q[l])) * jnp.max(jnp.abs(g_k[l]))).reshape(1)
        y_fx = _attention(_fox_kernel, qkv_fx.reshape(batch, seq, 3 * W_BRANCH), cum_f, zmax)
        moe_layer = l % 2 == 1
        router = None
        if moe_layer:
            wr = _pad_lanes(w_router[l // 2])
            router = _split_bf16(wr, 2)
        outs = _merge(xt, y_sb.reshape(t, W_BRANCH), y_fx.reshape(t, W_BRANCH), row(g_mix[l]),
                      w_gate, row(b_gate[l]), w_o_sb[l].astype(BF16), w_o_fox[l].astype(BF16),
                      w_out[l].astype(BF16), row(g_ffn[l]), router, tm_proj)
        if moe_layer:
            x1, xs_local, padded_len, route = outs
            xt = _moe(x1, xs_local, padded_len, route, w_gu_exp[l // 2].astype(BF16),
                      w_dn_exp[l // 2].astype(BF16), tm_proj)
        else:
            x1, h2 = outs
            xt = _ffn(h2, x1, w_gu_dense[l // 2].astype(BF16), w_dn_dense[l // 2].astype(BF16),
                      tm_proj, w_dn_dense.shape[1] // 2)
    return xt.reshape(batch, seq, d)
```

```python
import functools

import jax
import jax.numpy as jnp
from jax import lax
from jax.experimental import pallas as pl
from jax.experimental.pallas import tpu as pltpu

F32 = jnp.float32
BF16 = jnp.bfloat16

HEAD_DIM = 64
N_HEADS = 8
W_BRANCH = N_HEADS * HEAD_DIM
N_EXPERTS = 8
TOP_K = 2
MOE_CHUNK = 16
MOE_EXPERT_TILE = 512
RMS_EPS = 1e-6
QK_SCALE = 1.0 / 8.0
LANES = 128
HEADS_PER_BLOCK = LANES // HEAD_DIM
V7X_VMEM_LIMIT_BYTES = 56 * 1024 * 1024

ATT_BLOCK = 256
ATT_LANE_BLOCKS = 4
F32_EXP_UNDERFLOW = 106.0
FOX_CONSTANT_SHIFT_MAX = 30.0
QK_BOUND_MARGIN = 1.05
CUM_BLOCK = 512


def _split_bf16(x, parts):
    out = []
    for _ in range(parts - 1):
        hi = x.astype(BF16)
        out.append(hi)
        x = x - hi.astype(F32)
    out.append(x.astype(BF16))
    return out


def _dot(a, b):
    return jnp.dot(a, b, preferred_element_type=F32)


def _dot_nt(a, b):
    return lax.dot_general(a, b, (((1,), (1,)), ((), ())), preferred_element_type=F32)


def _rmsnorm(x, g):
    ms = jnp.mean(x * x, axis=-1, keepdims=True)
    return x * lax.rsqrt(ms + RMS_EPS) * g


def _sigmoid(x):
    return 1.0 / (1.0 + jnp.exp(-x))


def _compiler_params(semantics):
    return pltpu.CompilerParams(dimension_semantics=semantics,
                                vmem_limit_bytes=V7X_VMEM_LIMIT_BYTES)


def _proj_kernel(x_ref, g_ref, w_ref, gq_ref, gk_ref, sb_ref, fx_ref, f_ref):
    h = _rmsnorm(x_ref[...], g_ref[...]).astype(BF16)
    wb = W_BRANCH
    r = lax.broadcasted_iota(jnp.int32, (wb, wb), 0) // HEAD_DIM
    c = lax.broadcasted_iota(jnp.int32, (wb, wb), 1) // HEAD_DIM
    head_mean = jnp.where(r == c, 1.0 / HEAD_DIM, 0.0).astype(BF16)
    for i in range(3):
        acc = _dot(h, w_ref[:, i * wb:(i + 1) * wb])
        if i == 0:
            acc = acc * QK_SCALE
        sb_ref[:, i * wb:(i + 1) * wb] = acc.astype(BF16)
    for i, gain_ref in enumerate((gq_ref, gk_ref, None)):
        acc = _dot(h, w_ref[:, (3 + i) * wb:(4 + i) * wb])
        if gain_ref is not None:
            ms = _dot((acc * acc).astype(BF16), head_mean)
            acc = acc * lax.rsqrt(ms + RMS_EPS) * gain_ref[...]
        if i == 0:
            acc = acc * QK_SCALE
        fx_ref[:, i * wb:(i + 1) * wb] = acc.astype(BF16)
    f_ref[...] = _dot(h, w_ref[:, 6 * wb:6 * wb + LANES])


def _proj(x, g, w, gq, gk, tm):
    t, d = x.shape
    n = w.shape[1]
    full = lambda i: (0, 0)
    return pl.pallas_call(
        _proj_kernel,
        grid=(t // tm,),
        in_specs=[
            pl.BlockSpec((tm, d), lambda i: (i, 0)),
            pl.BlockSpec((1, d), full),
            pl.BlockSpec((d, n), full),
            pl.BlockSpec((1, W_BRANCH), full),
            pl.BlockSpec((1, W_BRANCH), full),
        ],
        out_specs=[
            pl.BlockSpec((tm, 3 * W_BRANCH), lambda i: (i, 0)),
            pl.BlockSpec((tm, 3 * W_BRANCH), lambda i: (i, 0)),
            pl.BlockSpec((tm, LANES), lambda i: (i, 0)),
        ],
        out_shape=[
            jax.ShapeDtypeStruct((t, 3 * W_BRANCH), BF16),
            jax.ShapeDtypeStruct((t, 3 * W_BRANCH), BF16),
            jax.ShapeDtypeStruct((t, LANES), F32),
        ],
        compiler_params=_compiler_params(("parallel",)),
        name="proj",
    )(x, g, w, gq, gk)


def _cumf_kernel(f_ref, b_ref, o_ref):
    seq = f_ref.shape[1]
    cb = min(CUM_BLOCK, seq)
    r = lax.broadcasted_iota(jnp.int32, (cb, cb), 0)
    c = lax.broadcasted_iota(jnp.int32, (cb, cb), 1)
    prefix = jnp.where(r <= c, 1.0, 0.0).astype(BF16)
    carry = jnp.zeros((N_HEADS, 1), F32)
    for i in range(seq // cb):
        v = f_ref[0, i * cb:(i + 1) * cb, :] + b_ref[...]
        log_f = jnp.minimum(v, 0.0) - jnp.log1p(jnp.exp(-jnp.abs(v)))
        log_f = log_f.T[:N_HEADS, :]
        cum = carry
        for part in _split_bf16(log_f, 3):
            cum = cum + _dot(part, prefix)
        o_ref[0, :, i * cb:(i + 1) * cb] = cum
        carry = cum[:, cb - 1:cb]


def _cumf(f_pre, b_f):
    b, s, _ = f_pre.shape
    return pl.pallas_call(
        _cumf_kernel,
        grid=(b,),
        in_specs=[
            pl.BlockSpec((1, s, LANES), lambda i: (i, 0, 0)),
            pl.BlockSpec((1, LANES), lambda i: (0, 0)),
        ],
        out_specs=pl.BlockSpec((1, N_HEADS, s), lambda i: (i, 0, 0)),
        out_shape=jax.ShapeDtypeStruct((b, N_HEADS, s), F32),
        compiler_params=_compiler_params(("parallel",)),
        name="cumf",
    )(f_pre, b_f)


def _head_queries(q_ref, half):
    q = q_ref[0, :, half * LANES:(half + 1) * LANES].astype(F32)
    lane = lax.broadcasted_iota(jnp.int32, (1, LANES), 1)
    return [jnp.where((lane >= hh * HEAD_DIM) & (lane < (hh + 1) * HEAD_DIM), q, 0.0).astype(BF16)
            for hh in range(HEADS_PER_BLOCK)]


def _run_chains(q_ref, k_ref, v_ref, o_ref, stages, finish, stop):
    bk = ATT_BLOCK
    halves = range(q_ref.shape[2] // LANES)
    qi = pl.program_id(2)
    queries = [q for half in halves for q in _head_queries(q_ref, half)]

    def advance(kb, masked, states):
        start = pl.multiple_of(kb * bk, bk)
        kv = [(k_ref[0, pl.ds(start, bk), half * LANES:(half + 1) * LANES],
               v_ref[0, pl.ds(start, bk), half * LANES:(half + 1) * LANES]) for half in halves]
        args = [(head, q, *kv[head // HEADS_PER_BLOCK], start, masked)
                for head, q in enumerate(queries)]
        tmps = [None] * len(args)
        for stage in stages[:-1]:
            tmps = [stage(*a, tmp) for a, tmp in zip(args, tmps)]
        return tuple(stages[-1](*a, tmp, state) for a, tmp, state in zip(args, tmps, states))

    def body(carry):
        i, _, states = carry
        kb = qi - 1 - i
        states = advance(kb, False, states)
        return i + 1, stop(states, jnp.maximum(kb - 1, 0)).astype(jnp.int32), states

    states = advance(qi, True, [None] * len(queries))
    _, _, states = lax.while_loop(lambda c: jnp.logical_and(c[0] < qi, c[1] == 0), body,
                                  (jnp.int32(0), jnp.int32(0), states))
    lane = lax.broadcasted_iota(jnp.int32, (1, LANES), 1)
    for half in halves:
        outs = [finish(head, states[head])
                for head in range(half * HEADS_PER_BLOCK, (half + 1) * HEADS_PER_BLOCK)]
        o_ref[0, :, half * LANES:(half + 1) * LANES] = jnp.where(
            lane < HEAD_DIM, outs[0], outs[1]).astype(BF16)


def _sb_kernel(q_ref, k_ref, v_ref, o_ref):
    bk = ATT_BLOCK
    row = lax.broadcasted_iota(jnp.int32, (bk, bk), 0)
    col = lax.broadcasted_iota(jnp.int32, (bk, bk), 1)
    suffix = jnp.where(row >= col, 1.0, 0.0).astype(BF16)
    below_diag = col < row

    def scores(head, q, k, v, start, masked, tmp):
        z = _dot_nt(q, k)
        neg_abs = lax.bitcast_convert_type(
            lax.bitcast_convert_type(z, jnp.uint32) | jnp.uint32(0x80000000), F32)
        sp = jnp.maximum(z, 0.0) + jnp.log(1.0 + jnp.exp(neg_abs))
        if masked:
            sp = jnp.where(below_diag, sp, 0.0)
        return z, sp.astype(BF16)

    def suffix_sums(head, q, k, v, start, masked, tmp):
        z, sp_bf16 = tmp
        return z, _dot(sp_bf16, suffix)

    def weights(head, q, k, v, start, masked, tmp, state):
        z, incl = tmp
        arg = z - incl
        if state is not None:
            arg = arg - state[0]
        w = jnp.exp(arg)
        if masked:
            w = jnp.where(below_diag, w, 0.0)
        pv = _dot(w.astype(BF16), v)
        if state is None:
            return incl[:, 0:1], pv
        return state[0] + incl[:, 0:1], state[1] + pv

    def stop(states, next_block):
        carry = functools.reduce(jnp.minimum, [state[0] for state in states])
        return jnp.min(carry) > F32_EXP_UNDERFLOW

    _run_chains(q_ref, k_ref, v_ref, o_ref, (scores, suffix_sums, weights),
                lambda head, state: state[1], stop)


def _fox_kernel(zmax_ref, fend_ref, q_ref, k_ref, v_ref, f_ref, fcol_ref, o_ref):
    bk = ATT_BLOCK
    n_blocks = k_ref.shape[1] // bk
    n_heads = f_ref.shape[2]
    qi = pl.program_id(2)
    row = lax.broadcasted_iota(jnp.int32, (bk, bk), 0)
    col = lax.broadcasted_iota(jnp.int32, (bk, bk), 1)
    causal = col <= row
    lane = lax.broadcasted_iota(jnp.int32, (1, LANES), 1)
    table = (pl.program_id(0) * (W_BRANCH // HEAD_DIM) + pl.program_id(1) * n_heads) * n_blocks

    def ones_lanes_cache():
        values = {}

        def values_with_ones(hh, v):
            if (hh, id(v)) not in values:
                own = (lane >= hh * HEAD_DIM) & (lane < (hh + 1) * HEAD_DIM)
                values[(hh, id(v))] = (v, jnp.where(own, v.astype(F32), 1.0).astype(BF16))
            return values[(hh, id(v))][1]

        return values_with_ones

    def normalised(head, acc):
        ones_lane = (1 - head % HEADS_PER_BLOCK) * HEAD_DIM
        return acc / acc[:, ones_lane:ones_lane + 1]

    @pl.when(zmax_ref[0] <= FOX_CONSTANT_SHIFT_MAX)
    def _():
        values_with_ones = ones_lanes_cache()
        shift = [fcol_ref[0, :, head:head + 1] - zmax_ref[0] for head in range(n_heads)]

        def weights(head, q, k, v, start, masked, tmp):
            s = _dot_nt(q, k) - f_ref[0, 0, head:head + 1, pl.ds(start, bk)] + shift[head]
            if masked:
                s = jnp.where(causal, s, -jnp.inf)
            return jnp.exp(s).astype(BF16)

        def accumulate(head, q, k, v, start, masked, tmp, state):
            pv = _dot(tmp, values_with_ones(head % HEADS_PER_BLOCK, v))
            return pv if state is None else state + pv

        def stop(states, next_block):
            last = jnp.maximum(qi - 1, 0)
            gaps = [fend_ref[table + head * n_blocks + next_block] - fend_ref[table + head * n_blocks + last]
                    for head in range(n_heads)]
            return functools.reduce(jnp.maximum, gaps) < -F32_EXP_UNDERFLOW

        _run_chains(q_ref, k_ref, v_ref, o_ref, (weights, accumulate), normalised, stop)

    @pl.when(zmax_ref[0] > FOX_CONSTANT_SHIFT_MAX)
    def _():
        values_with_ones = ones_lanes_cache()

        def logits(head, q, k, v, start, masked, tmp):
            s = _dot_nt(q, k) - f_ref[0, 0, head:head + 1, pl.ds(start, bk)]
            if masked:
                s = jnp.where(causal, s, -jnp.inf)
            return s, jnp.max(s, axis=1, keepdims=True)

        def accumulate(head, q, k, v, start, masked, tmp, state):
            s, m_blk = tmp
            v_ones = values_with_ones(head % HEADS_PER_BLOCK, v)
            if state is None:
                return m_blk, _dot(jnp.exp(s - m_blk).astype(BF16), v_ones)
            m, acc = state
            m_new = jnp.maximum(m, m_blk)
            pv = _dot(jnp.exp(s - m_new).astype(BF16), v_ones)
            return m_new, jnp.exp(m - m_new) * acc + pv

        def stop(states, next_block):
            slack = functools.reduce(jnp.minimum, [
                state[0] - fend_ref[table + head * n_blocks + next_block]
                for head, state in enumerate(states)])
            return jnp.min(slack) > zmax_ref[0] + F32_EXP_UNDERFLOW

        _run_chains(q_ref, k_ref, v_ref, o_ref, (logits, accumulate),
                    lambda head, state: normalised(head, state[1]), stop)


def _attention(kernel, qkv, cum_f=None, zmax=None):
    b, s, _ = qkv.shape
    bq = min(ATT_BLOCK, s)
    width = ATT_LANE_BLOCKS * LANES
    n_groups = W_BRANCH // width
    in_specs = [
        pl.BlockSpec((1, bq, width), lambda bi, g, qi: (bi, qi, g)),
        pl.BlockSpec((1, s, width), lambda bi, g, qi: (bi, 0, n_groups + g)),
        pl.BlockSpec((1, s, width), lambda bi, g, qi: (bi, 0, 2 * n_groups + g)),
    ]
    args = [qkv, qkv, qkv]
    if cum_f is not None:
        heads = ATT_LANE_BLOCKS * HEADS_PER_BLOCK
        smem = pl.BlockSpec(memory_space=pltpu.SMEM)
        in_specs = [smem, smem] + in_specs + [
            pl.BlockSpec((1, 1, heads, s), lambda bi, g, qi: (bi, g, 0, 0)),
            pl.BlockSpec((1, bq, heads), lambda bi, g, qi: (bi * n_groups + g, qi, 0))]
        f_end = -cum_f[:, :, bq - 1::bq]
        by_head = cum_f.reshape(b, n_groups, heads, s)
        args = [zmax, f_end.reshape(-1)] + args + [
            by_head, jnp.swapaxes(by_head, 2, 3).reshape(b * n_groups, s, heads)]
    return pl.pallas_call(
        kernel,
        grid=(b, n_groups, s // bq),
        in_specs=in_specs,
        out_specs=pl.BlockSpec((1, bq, width), lambda bi, g, qi: (bi, qi, g)),
        out_shape=jax.ShapeDtypeStruct((b, s, W_BRANCH), BF16),
        compiler_params=_compiler_params(("parallel", "parallel", "arbitrary")),
        name=kernel.__name__.strip("_"),
    )(*args)


def _local_rows(tm):
    return TOP_K * tm + N_EXPERTS * MOE_CHUNK


def _route_and_sort(logits, h2, xs_ref, len_ref, route_ref):
    tm = logits.shape[0]
    lt = logits.T[:N_EXPERTS, :]
    expert = lax.broadcasted_iota(jnp.int32, lt.shape, 0).astype(F32)
    m1 = jnp.max(lt, axis=0, keepdims=True)
    i1 = jnp.min(jnp.where(lt == m1, expert, float(N_EXPERTS)), axis=0, keepdims=True)
    lt2 = jnp.where(expert == i1, -jnp.inf, lt)
    m2 = jnp.max(lt2, axis=0, keepdims=True)
    i2 = jnp.min(jnp.where(lt2 == m2, expert, float(N_EXPERTS)), axis=0, keepdims=True)
    e2 = jnp.exp(m2 - m1)
    w1 = 1.0 / (1.0 + e2)
    w2 = e2 * w1
    chosen = jnp.where((expert == i1) | (expert == i2), 1.0, 0.0)
    before = (lax.broadcasted_iota(jnp.int32, (tm, tm), 0)
              < lax.broadcasted_iota(jnp.int32, (tm, tm), 1))
    rank = _dot(chosen.astype(BF16), jnp.where(before, 1.0, 0.0).astype(BF16))
    count = jnp.sum(chosen, axis=1, keepdims=True)
    padded = jnp.floor((count + (MOE_CHUNK - 1)) * (1.0 / MOE_CHUNK)) * MOE_CHUNK
    r1 = jnp.sum(jnp.where(expert == i1, rank, 0.0), axis=0, keepdims=True)
    r2 = jnp.sum(jnp.where(expert == i2, rank, 0.0), axis=0, keepdims=True)
    start = jnp.zeros((1, 1), F32)
    for e in range(N_EXPERTS):
        r1 = r1 + jnp.where(i1 == e, start, 0.0)
        r2 = r2 + jnp.where(i2 == e, start, 0.0)
        start = start + padded[e:e + 1, :]
    rows = _local_rows(tm)
    row = lax.broadcasted_iota(jnp.int32, (rows, tm), 0).astype(F32)
    place = jnp.where((row == r1) | (row == r2), 1.0, 0.0).astype(BF16)
    xs_ref[0] = _dot(place, h2).astype(BF16)
    len_ref[0] = jnp.broadcast_to(padded, (N_EXPERTS, LANES))
    field = lax.broadcasted_iota(jnp.int32, (LANES, tm), 0)
    fields = jnp.where(field == 0, r1, jnp.where(field == 1, r2, jnp.where(
        field == 2, w1, jnp.where(field == 3, w2, 0.0))))
    route_ref[...] = fields.T


def _merge_kernel(x_ref, ysb_ref, yfx_ref, gmix_ref, wgate_ref, bgate_ref, wosb_ref, wofx_ref,
                  wout_ref, gffn_ref, *rest, with_router):
    if with_router:
        wr_hi_ref, wr_lo_ref, x1_ref, xs_ref, len_ref, route_ref = rest
    else:
        x1_ref, h2_ref = rest
    d = x_ref.shape[1]
    x = x_ref[...]
    h = _rmsnorm(x, gmix_ref[...]).astype(BF16)
    merged = None
    for i, (y_ref, wo_ref) in enumerate(((ysb_ref, wosb_ref), (yfx_ref, wofx_ref))):
        gate = _sigmoid(_dot(h, wgate_ref[:, i * d:(i + 1) * d]) + bgate_ref[:, i * d:(i + 1) * d])
        branch = gate * _dot(y_ref[...], wo_ref[...])
        merged = branch if merged is None else merged + branch
    x1 = x + _dot(merged.astype(BF16), wout_ref[...])
    x1_ref[...] = x1
    h2 = _rmsnorm(x1, gffn_ref[...])
    if with_router:
        h_hi, h_lo = _split_bf16(h2, 2)
        logits = _dot(h_hi, wr_hi_ref[...]) + _dot(h_lo, wr_hi_ref[...]) + _dot(h_hi, wr_lo_ref[...])
        _route_and_sort(logits, h_hi, xs_ref, len_ref, route_ref)
    else:
        h2_ref[...] = h2.astype(BF16)


def _merge(x, y_sb, y_fx, g_mix, w_gate, b_gate, w_o_sb, w_o_fox, w_out, g_ffn, w_router, tm):
    t, d = x.shape
    with_router = w_router is not None
    full = lambda i: (0, 0)
    tile = lambda i: (i, 0)
    in_specs = [
        pl.BlockSpec((tm, d), tile),
        pl.BlockSpec((tm, W_BRANCH), tile),
        pl.BlockSpec((tm, W_BRANCH), tile),
        pl.BlockSpec((1, d), full),
        pl.BlockSpec((d, 2 * d), full),
        pl.BlockSpec((1, 2 * d), full),
        pl.BlockSpec((W_BRANCH, d), full),
        pl.BlockSpec((W_BRANCH, d), full),
        pl.BlockSpec((d, d), full),
        pl.BlockSpec((1, d), full),
    ]
    args = [x, y_sb, y_fx, g_mix, w_gate, b_gate, w_o_sb, w_o_fox, w_out, g_ffn]
    out_specs = [pl.BlockSpec((tm, d), tile)]
    out_shape = [jax.ShapeDtypeStruct((t, d), F32)]
    if with_router:
        rows = _local_rows(tm)
        in_specs += [pl.BlockSpec((d, LANES), full)] * 2
        args += list(w_router)
        out_specs += [pl.BlockSpec((1, rows, d), lambda i: (i, 0, 0)),
                      pl.BlockSpec((1, N_EXPERTS, LANES), lambda i: (i, 0, 0)),
                      pl.BlockSpec((tm, LANES), tile)]
        out_shape += [jax.ShapeDtypeStruct((t // tm, rows, d), BF16),
                      jax.ShapeDtypeStruct((t // tm, N_EXPERTS, LANES), F32),
                      jax.ShapeDtypeStruct((t, LANES), F32)]
    else:
        out_specs.append(pl.BlockSpec((tm, d), tile))
        out_shape.append(jax.ShapeDtypeStruct((t, d), BF16))
    return pl.pallas_call(
        functools.partial(_merge_kernel, with_router=with_router),
        grid=(t // tm,),
        in_specs=in_specs,
        out_specs=out_specs,
        out_shape=out_shape,
        compiler_params=_compiler_params(("parallel",)),
        name="merge_router" if with_router else "merge",
    )(*args)


def _swiglu_chunk(h, wg, wu, wd):
    g = _dot(h, wg)
    u = _dot(h, wu)
    return _dot((g * _sigmoid(g) * u).astype(BF16), wd)


def _ffn_kernel(h_ref, x_ref, wg_ref, wu_ref, wd_ref, o_ref):
    @pl.when(pl.program_id(1) == 0)
    def _():
        o_ref[...] = x_ref[...]

    o_ref[...] += _swiglu_chunk(h_ref[...], wg_ref[...], wu_ref[...], wd_ref[...])


def _ffn(h, x, w_gu, w_dn, tm, tf):
    t, d = x.shape
    d_ff = w_dn.shape[0]
    nf = d_ff // tf
    return pl.pallas_call(
        _ffn_kernel,
        grid=(t // tm, nf),
        in_specs=[
            pl.BlockSpec((tm, d), lambda i, j: (i, 0)),
            pl.BlockSpec((tm, d), lambda i, j: (i, 0)),
            pl.BlockSpec((d, tf), lambda i, j: (0, j)),
            pl.BlockSpec((d, tf), lambda i, j: (0, nf + j)),
            pl.BlockSpec((tf, d), lambda i, j: (j, 0)),
        ],
        out_specs=pl.BlockSpec((tm, d), lambda i, j: (i, 0)),
        out_shape=jax.ShapeDtypeStruct((t, d), F32),
        compiler_params=_compiler_params(("parallel", "arbitrary")),
        name="ffn",
    )(h, x, w_gu, w_gu, w_dn)


def _expert_kernel(expert_ref, valid_ref, src_ref, xs_hbm, wg_ref, wu_ref, wd_ref, o_ref, buf, sems):
    j = pl.program_id(0)
    rows = buf.shape[1]
    chunks = rows // MOE_CHUNK

    def fetches(step, slot):
        return [pltpu.make_async_copy(
            xs_hbm.at[src_ref[step * chunks + c]],
            buf.at[slot, pl.ds(c * MOE_CHUNK, MOE_CHUNK), :], sems.at[slot]) for c in range(chunks)]

    slot = lax.rem(j, 2)

    @pl.when(j == 0)
    def _():
        for copy in fetches(0, 0):
            copy.start()

    @pl.when(j + 1 < pl.num_programs(0))
    def _():
        for copy in fetches(j + 1, 1 - slot):
            copy.start()

    for copy in fetches(j, slot):
        copy.wait()

    @pl.when(valid_ref[j] != 0)
    def _():
        o_ref[...] = _swiglu_chunk(buf[slot], wg_ref[0], wu_ref[0], wd_ref[0]).astype(BF16)

    @pl.when(valid_ref[j] == 0)
    def _():
        o_ref[...] = jnp.zeros_like(o_ref)


def _experts(xs, tile_expert, tile_valid, chunk_src, w_gu, w_dn, n_rows):
    d = xs.shape[2]
    d_ff = w_dn.shape[1]
    te = MOE_EXPERT_TILE
    grid_spec = pltpu.PrefetchScalarGridSpec(
        num_scalar_prefetch=3,
        grid=(n_rows // te,),
        in_specs=[
            pl.BlockSpec(memory_space=pl.ANY),
            pl.BlockSpec((1, d, d_ff), lambda j, ex, va, sr: (ex[j], 0, 0)),
            pl.BlockSpec((1, d, d_ff), lambda j, ex, va, sr: (ex[j], 0, 1)),
            pl.BlockSpec((1, d_ff, d), lambda j, ex, va, sr: (ex[j], 0, 0)),
        ],
        out_specs=pl.BlockSpec((te, d), lambda j, ex, va, sr: (j, 0)),
        scratch_shapes=[pltpu.VMEM((2, te, d), BF16), pltpu.SemaphoreType.DMA((2,))],
    )
    return pl.pallas_call(
        _expert_kernel,
        grid_spec=grid_spec,
        out_shape=jax.ShapeDtypeStruct((n_rows, d), BF16),
        compiler_params=_compiler_params(("arbitrary",)),
        name="experts",
    )(tile_expert, tile_valid, chunk_src, xs, w_gu, w_gu, w_dn)


def _combine_kernel(goff_ref, count_ref, loff_ref, x_ref, route_ref, ys_hbm, o_ref, buf, sems):
    i = pl.program_id(0)
    slot = lax.rem(i, 2)

    def for_each_chunk(step, which, act):
        for e in range(N_EXPERTS):
            g = step * N_EXPERTS + e

            @pl.loop(0, count_ref[g])
            def _(c):
                dst = pl.multiple_of((loff_ref[g] + c) * MOE_CHUNK, MOE_CHUNK)
                act(pltpu.make_async_copy(ys_hbm.at[goff_ref[g] + c],
                                          buf.at[which, pl.ds(dst, MOE_CHUNK), :], sems.at[which]))

    @pl.when(i == 0)
    def _():
        buf[...] = jnp.zeros_like(buf)
        for_each_chunk(0, 0, lambda copy: copy.start())

    @pl.when(i + 1 < pl.num_programs(0))
    def _():
        for_each_chunk(i + 1, 1 - slot, lambda copy: copy.start())

    for_each_chunk(i, slot, lambda copy: copy.wait())

    route = route_ref[...]
    y = buf[slot]
    col = lax.broadcasted_iota(jnp.int32, (route.shape[0], buf.shape[1]), 1).astype(F32)
    out = x_ref[...]
    for k in range(TOP_K):
        pick = jnp.where(col == route[:, k:k + 1], 1.0, 0.0).astype(BF16)
        out = out + route[:, TOP_K + k:TOP_K + k + 1] * _dot(pick, y)
    o_ref[...] = out


def _combine(x, route, ys, goff, count, loff, tm):
    t, d = x.shape
    grid_spec = pltpu.PrefetchScalarGridSpec(
        num_scalar_prefetch=3,
        grid=(t // tm,),
        in_specs=[
            pl.BlockSpec((tm, d), lambda i, *_: (i, 0)),
            pl.BlockSpec((tm, LANES), lambda i, *_: (i, 0)),
            pl.BlockSpec(memory_space=pl.ANY),
        ],
        out_specs=pl.BlockSpec((tm, d), lambda i, *_: (i, 0)),
        scratch_shapes=[pltpu.VMEM((2, _local_rows(tm), d), BF16), pltpu.SemaphoreType.DMA((2,))],
    )
    return pl.pallas_call(
        _combine_kernel,
        grid_spec=grid_spec,
        out_shape=jax.ShapeDtypeStruct((t, d), F32),
        compiler_params=_compiler_params(("arbitrary",)),
        name="combine",
    )(goff, count, loff, x, route, ys)


def _moe(x1, xs_local, padded_len, route, w_gu, w_dn, tm):
    t, d = x1.shape
    n_tiles, local_rows, _ = xs_local.shape
    te = MOE_EXPERT_TILE
    n_rows = -(-(TOP_K * t + n_tiles * N_EXPERTS * (MOE_CHUNK - 1) + N_EXPERTS * (te - 1)) // te) * te
    length = padded_len[:, :, 0].astype(jnp.int32)
    count = length // MOE_CHUNK
    used = jnp.sum(length, axis=0)
    span = -(-used // te) * te
    base = jnp.cumsum(span) - span
    goff = (base[None, :] + jnp.cumsum(length, axis=0) - length) // MOE_CHUNK
    loff = (jnp.cumsum(length, axis=1) - length) // MOE_CHUNK
    seg_start = goff.T.reshape(-1)
    seg_count = count.T.reshape(-1)
    seg_src = (jnp.arange(n_tiles, dtype=jnp.int32)[:, None] * (local_rows // MOE_CHUNK) + loff).T.reshape(-1)
    chunk = jnp.arange(n_rows // MOE_CHUNK, dtype=jnp.int32)[:, None]
    within = chunk - seg_start[None, :]
    owner = (within >= 0) & (within < seg_count[None, :])
    chunk_src = jnp.sum(jnp.where(owner, seg_src[None, :] + within, 0), axis=1).astype(jnp.int32)
    tile_row = jnp.arange(n_rows // te, dtype=jnp.int32)[:, None] * te
    tile_expert = (jnp.sum(base[None, :] <= tile_row, axis=1) - 1).astype(jnp.int32)
    tile_valid = jnp.any((base[None, :] <= tile_row) & (tile_row < (base + used)[None, :]),
                         axis=1).astype(jnp.int32)
    ys = _experts(xs_local.reshape(-1, MOE_CHUNK, d), tile_expert, tile_valid, chunk_src, w_gu, w_dn, n_rows)
    return _combine(x1, route, ys.reshape(-1, MOE_CHUNK, d), goff.reshape(-1), count.reshape(-1),
                    loff.reshape(-1), tm)


def _pad_lanes(w):
    return jnp.pad(w, ((0, 0), (0, LANES - w.shape[1])))


def kernel(x, g_mix, w_in, b_f, b_gate, g_q, g_k, w_o_sb, w_o_fox, w_out, g_ffn, w_gu_dense,
           w_dn_dense, w_router, w_gu_exp, w_dn_exp):
    batch, seq, d = x.shape
    depth = w_in.shape[0]
    t = batch * seq
    tm_proj = min(512, t)
    tm_ffn = min(1024, t)
    n_qkv = 6 * W_BRANCH
    xt = x.reshape(t, d)
    for l in range(depth):
        w_l = w_in[l]
        w_proj = jnp.concatenate(
            [w_l[:, :n_qkv], _pad_lanes(w_l[:, n_qkv:n_qkv + N_HEADS])], axis=1).astype(BF16)
        w_gate = w_l[:, n_qkv + N_HEADS:].astype(BF16)
        row = lambda v: v.reshape(1, -1)
        gq = jnp.tile(g_q[l], N_HEADS).reshape(1, W_BRANCH)
        gk = jnp.tile(g_k[l], N_HEADS).reshape(1, W_BRANCH)
        qkv_sb, qkv_fx, f_pre = _proj(xt, row(g_mix[l]), w_proj, gq, gk, tm_proj)
        cum_f = _cumf(f_pre.reshape(batch, seq, LANES), _pad_lanes(row(b_f[l])))
        y_sb = _attention(_sb_kernel, qkv_sb.reshape(batch, seq, 3 * W_BRANCH))
        zmax = (QK_BOUND_MARGIN * 8.0 * jnp.max(jnp.abs(g_q[l])) * jnp.max(jnp.abs(g_k[l]))).reshape(1)
        y_fx = _attention(_fox_kernel, qkv_fx.reshape(batch, seq, 3 * W_BRANCH), cum_f, zmax)
        moe_layer = l % 2 == 1
        router = None
        if moe_layer:
            wr = _pad_lanes(w_router[l // 2])
            router = _split_bf16(wr, 2)
        outs = _merge(xt, y_sb.reshape(t, W_BRANCH), y_fx.reshape(t, W_BRANCH), row(g_mix[l]),
                      w_gate, row(b_gate[l]), w_o_sb[l].astype(BF16), w_o_fox[l].astype(BF16),
                      w_out[l].astype(BF16), row(g_ffn[l]), router, tm_proj)
        if moe_layer:
            x1, xs_local, padded_len, route = outs
            xt = _moe(x1, xs_local, padded_len, route, w_gu_exp[l // 2].astype(BF16),
                      w_dn_exp[l // 2].astype(BF16), tm_proj)
        else:
            x1, h2 = outs
            xt = _ffn(h2, x1, w_gu_dense[l // 2].astype(BF16), w_dn_dense[l // 2].astype(BF16),
                      tm_proj, w_dn_dense.shape[1] // 2)
    return xt.reshape(batch, seq, d)
```

```python
import functools

import jax
import jax.numpy as jnp
from jax import lax
from jax.experimental import pallas as pl
from jax.experimental.pallas import tpu as pltpu

F32 = jnp.float32
BF16 = jnp.bfloat16

HEAD_DIM = 64
N_HEADS = 8
W_BRANCH = N_HEADS * HEAD_DIM
N_EXPERTS = 8
TOP_K = 2
MOE_CHUNK = 16
MOE_EXPERT_TILE = 512
RMS_EPS = 1e-6
QK_SCALE = 1.0 / 8.0
LANES = 128
HEADS_PER_BLOCK = LANES // HEAD_DIM
V7X_VMEM_LIMIT_BYTES = 56 * 1024 * 1024

ATT_BLOCK = 256
ATT_LANE_BLOCKS = 4
F32_EXP_UNDERFLOW = 106.0
FOX_CONSTANT_SHIFT_MAX = 30.0
QK_BOUND_MARGIN = 1.05
CUM_BLOCK = 512


def _split_bf16(x, parts):
    out = []
    for _ in range(parts - 1):
        hi = x.astype(BF16)
        out.append(hi)
        x = x - hi.astype(F32)
    out.append(x.astype(BF16))
    return out


def _dot(a, b):
    return jnp.dot(a, b, preferred_element_type=F32)


def _dot_nt(a, b):
    return lax.dot_general(a, b, (((1,), (1,)), ((), ())), preferred_element_type=F32)


def _rmsnorm(x, g):
    ms = jnp.mean(x * x, axis=-1, keepdims=True)
    return x * lax.rsqrt(ms + RMS_EPS) * g


def _sigmoid(x):
    return 1.0 / (1.0 + jnp.exp(-x))


def _compiler_params(semantics):
    return pltpu.CompilerParams(dimension_semantics=semantics,
                                vmem_limit_bytes=V7X_VMEM_LIMIT_BYTES)


def _proj_kernel(x_ref, g_ref, w_ref, gq_ref, gk_ref, sb_ref, fx_ref, f_ref):
    h = _rmsnorm(x_ref[...], g_ref[...]).astype(BF16)
    wb = W_BRANCH
    r = lax.broadcasted_iota(jnp.int32, (wb, wb), 0) // HEAD_DIM
    c = lax.broadcasted_iota(jnp.int32, (wb, wb), 1) // HEAD_DIM
    head_mean = jnp.where(r == c, 1.0 / HEAD_DIM, 0.0).astype(BF16)
    for i in range(3):
        acc = _dot(h, w_ref[:, i * wb:(i + 1) * wb])
        if i == 0:
            acc = acc * QK_SCALE
        sb_ref[:, i * wb:(i + 1) * wb] = acc.astype(BF16)
    for i, gain_ref in enumerate((gq_ref, gk_ref, None)):
        acc = _dot(h, w_ref[:, (3 + i) * wb:(4 + i) * wb])
        if gain_ref is not None:
            ms = _dot((acc * acc).astype(BF16), head_mean)
            acc = acc * lax.rsqrt(ms + RMS_EPS) * gain_ref[...]
        if i == 0:
            acc = acc * QK_SCALE
        fx_ref[:, i * wb:(i + 1) * wb] = acc.astype(BF16)
    f_ref[...] = _dot(h, w_ref[:, 6 * wb:6 * wb + LANES])


def _proj(x, g, w, gq, gk, tm):
    t, d = x.shape
    n = w.shape[1]
    full = lambda i: (0, 0)
    return pl.pallas_call(
        _proj_kernel,
        grid=(t // tm,),
        in_specs=[
            pl.BlockSpec((tm, d), lambda i: (i, 0)),
            pl.BlockSpec((1, d), full),
            pl.BlockSpec((d, n), full),
            pl.BlockSpec((1, W_BRANCH), full),
            pl.BlockSpec((1, W_BRANCH), full),
        ],
        out_specs=[
            pl.BlockSpec((tm, 3 * W_BRANCH), lambda i: (i, 0)),
            pl.BlockSpec((tm, 3 * W_BRANCH), lambda i: (i, 0)),
            pl.BlockSpec((tm, LANES), lambda i: (i, 0)),
        ],
        out_shape=[
            jax.ShapeDtypeStruct((t, 3 * W_BRANCH), BF16),
            jax.ShapeDtypeStruct((t, 3 * W_BRANCH), BF16),
            jax.ShapeDtypeStruct((t, LANES), F32),
        ],
        compiler_params=_compiler_params(("parallel",)),
        name="proj",
    )(x, g, w, gq, gk)


def _cumf_kernel(f_ref, b_ref, o_ref):
    seq = f_ref.shape[1]
    cb = min(CUM_BLOCK, seq)
    r = lax.broadcasted_iota(jnp.int32, (cb, cb), 0)
    c = lax.broadcasted_iota(jnp.int32, (cb, cb), 1)
    prefix = jnp.where(r <= c, 1.0, 0.0).astype(BF16)
    carry = jnp.zeros((N_HEADS, 1), F32)
    for i in range(seq // cb):
        v = f_ref[0, i * cb:(i + 1) * cb, :] + b_ref[...]
        log_f = jnp.minimum(v, 0.0) - jnp.log1p(jnp.exp(-jnp.abs(v)))
        log_f = log_f.T[:N_HEADS, :]
        cum = carry
        for part in _split_bf16(log_f, 3):
            cum = cum + _dot(part, prefix)
        o_ref[0, :, i * cb:(i + 1) * cb] = cum
        carry = cum[:, cb - 1:cb]


def _cumf(f_pre, b_f):
    b, s, _ = f_pre.shape
    return pl.pallas_call(
        _cumf_kernel,
        grid=(b,),
        in_specs=[
            pl.BlockSpec((1, s, LANES), lambda i: (i, 0, 0)),
            pl.BlockSpec((1, LANES), lambda i: (0, 0)),
        ],
        out_specs=pl.BlockSpec((1, N_HEADS, s), lambda i: (i, 0, 0)),
        out_shape=jax.ShapeDtypeStruct((b, N_HEADS, s), F32),
        compiler_params=_compiler_params(("parallel",)),
        name="cumf",
    )(f_pre, b_f)


def _head_queries(q_ref, half):
    q = q_ref[0, :, half * LANES:(half + 1) * LANES].astype(F32)
    lane = lax.broadcasted_iota(jnp.int32, (1, LANES), 1)
    return [jnp.where((lane >= hh * HEAD_DIM) & (lane < (hh + 1) * HEAD_DIM), q, 0.0).astype(BF16)
            for hh in range(HEADS_PER_BLOCK)]


def _run_chains(q_ref, k_ref, v_ref, o_ref, stages, finish, stop):
    bk = ATT_BLOCK
    halves = range(q_ref.shape[2] // LANES)
    qi = pl.program_id(2)
    queries = [q for half in halves for q in _head_queries(q_ref, half)]

    def advance(kb, masked, states):
        start = pl.multiple_of(kb * bk, bk)
        kv = [(k_ref[0, pl.ds(start, bk), half * LANES:(half + 1) * LANES],
               v_ref[0, pl.ds(start, bk), half * LANES:(half + 1) * LANES]) for half in halves]
        args = [(head, q, *kv[head // HEADS_PER_BLOCK], start, masked)
                for head, q in enumerate(queries)]
        tmps = [None] * len(args)
        for stage in stages[:-1]:
            tmps = [stage(*a, tmp) for a, tmp in zip(args, tmps)]
        return tuple(stages[-1](*a, tmp, state) for a, tmp, state in zip(args, tmps, states))

    def body(carry):
        i, _, states = carry
        kb = qi - 1 - i
        states = advance(kb, False, states)
        return i + 1, stop(states, jnp.maximum(kb - 1, 0)).astype(jnp.int32), states

    states = advance(qi, True, [None] * len(queries))
    _, _, states = lax.while_loop(lambda c: jnp.logical_and(c[0] < qi, c[1] == 0), body,
                                  (jnp.int32(0), jnp.int32(0), states))
    lane = lax.broadcasted_iota(jnp.int32, (1, LANES), 1)
    for half in halves:
        outs = [finish(head, states[head])
                for head in range(half * HEADS_PER_BLOCK, (half + 1) * HEADS_PER_BLOCK)]
        o_ref[0, :, half * LANES:(half + 1) * LANES] = jnp.where(
            lane < HEAD_DIM, outs[0], outs[1]).astype(BF16)


def _sb_kernel(q_ref, k_ref, v_ref, o_ref):
    bk = ATT_BLOCK
    row = lax.broadcasted_iota(jnp.int32, (bk, bk), 0)
    col = lax.broadcasted_iota(jnp.int32, (bk, bk), 1)
    suffix = jnp.where(row >= col, 1.0, 0.0).astype(BF16)
    below_diag = col < row

    def scores(head, q, k, v, start, masked, tmp):
        z = _dot_nt(q, k)
        neg_abs = lax.bitcast_convert_type(
            lax.bitcast_convert_type(z, jnp.uint32) | jnp.uint32(0x80000000), F32)
        sp = jnp.maximum(z, 0.0) + jnp.log(1.0 + jnp.exp(neg_abs))
        if masked:
            sp = jnp.where(below_diag, sp, 0.0)
        return z, sp.astype(BF16)

    def suffix_sums(head, q, k, v, start, masked, tmp):
        z, sp_bf16 = tmp
        return z, _dot(sp_bf16, suffix)

    def weights(head, q, k, v, start, masked, tmp, state):
        z, incl = tmp
        arg = z - incl
        if state is not None:
            arg = arg - state[0]
        w = jnp.exp(arg)
        if masked:
            w = jnp.where(below_diag, w, 0.0)
        pv = _dot(w.astype(BF16), v)
        if state is None:
            return incl[:, 0:1], pv
        return state[0] + incl[:, 0:1], state[1] + pv

    def stop(states, next_block):
        carry = functools.reduce(jnp.minimum, [state[0] for state in states])
        return jnp.min(carry) > F32_EXP_UNDERFLOW

    _run_chains(q_ref, k_ref, v_ref, o_ref, (scores, suffix_sums, weights),
                lambda head, state: state[1], stop)


def _fox_kernel(zmax_ref, fend_ref, q_ref, k_ref, v_ref, f_ref, fcol_ref, o_ref):
    bk = ATT_BLOCK
    n_blocks = k_ref.shape[1] // bk
    n_heads = f_ref.shape[2]
    qi = pl.program_id(2)
    row = lax.broadcasted_iota(jnp.int32, (bk, bk), 0)
    col = lax.broadcasted_iota(jnp.int32, (bk, bk), 1)
    causal = col <= row
    lane = lax.broadcasted_iota(jnp.int32, (1, LANES), 1)
    table = (pl.program_id(0) * (W_BRANCH // HEAD_DIM) + pl.program_id(1) * n_heads) * n_blocks

    def ones_lanes_cache():
        values = {}

        def values_with_ones(hh, v):
            if (hh, id(v)) not in values:
                own = (lane >= hh * HEAD_DIM) & (lane < (hh + 1) * HEAD_DIM)
                values[(hh, id(v))] = (v, jnp.where(own, v.astype(F32), 1.0).astype(BF16))
            return values[(hh, id(v))][1]

        return values_with_ones

    def normalised(head, acc):
        ones_lane = (1 - head % HEADS_PER_BLOCK) * HEAD_DIM
        return acc / acc[:, ones_lane:ones_lane + 1]

    @pl.when(zmax_ref[0] <= FOX_CONSTANT_SHIFT_MAX)
    def _():
        values_with_ones = ones_lanes_cache()
        shift = [fcol_ref[0, :, head:head + 1] - zmax_ref[0] for head in range(n_heads)]

        def weights(head, q, k, v, start, masked, tmp):
            s = _dot_nt(q, k) - f_ref[0, 0, head:head + 1, pl.ds(start, bk)] + shift[head]
            if masked:
                s = jnp.where(causal, s, -jnp.inf)
            return jnp.exp(s).astype(BF16)

        def accumulate(head, q, k, v, start, masked, tmp, state):
            pv = _dot(tmp, values_with_ones(head % HEADS_PER_BLOCK, v))
            return pv if state is None else state + pv

        def stop(states, next_block):
            last = jnp.maximum(qi - 1, 0)
            gaps = [fend_ref[table + head * n_blocks + next_block] - fend_ref[table + head * n_blocks + last]
                    for head in range(n_heads)]
            return functools.reduce(jnp.maximum, gaps) < -F32_EXP_UNDERFLOW

        _run_chains(q_ref, k_ref, v_ref, o_ref, (weights, accumulate), normalised, stop)

    @pl.when(zmax_ref[0] > FOX_CONSTANT_SHIFT_MAX)
    def _():
        values_with_ones = ones_lanes_cache()

        def logits(head, q, k, v, start, masked, tmp):
            s = _dot_nt(q, k) - f_ref[0, 0, head:head + 1, pl.ds(start, bk)]
            if masked:
                s = jnp.where(causal, s, -jnp.inf)
            return s, jnp.max(s, axis=1, keepdims=True)

        def accumulate(head, q, k, v, start, masked, tmp, state):
            s, m_blk = tmp
            v_ones = values_with_ones(head % HEADS_PER_BLOCK, v)
            if state is None:
                return m_blk, _dot(jnp.exp(s - m_blk).astype(BF16), v_ones)
            m, acc = state
            m_new = jnp.maximum(m, m_blk)
            pv = _dot(jnp.exp(s - m_new).astype(BF16), v_ones)
            return m_new, jnp.exp(m - m_new) * acc + pv

        def stop(states, next_block):
            slack = functools.reduce(jnp.minimum, [
                state[0] - fend_ref[table + head * n_blocks + next_block]
                for head, state in enumerate(states)])
            return jnp.min(slack) > zmax_ref[0] + F32_EXP_UNDERFLOW

        _run_chains(q_ref, k_ref, v_ref, o_ref, (logits, accumulate),
                    lambda head, state: normalised(head, state[1]), stop)


def _attention(kernel, qkv, cum_f=None, zmax=None):
    b, s, _ = qkv.shape
    bq = min(ATT_BLOCK, s)
    width = ATT_LANE_BLOCKS * LANES
    n_groups = W_BRANCH // width
    in_specs = [
        pl.BlockSpec((1, bq, width), lambda bi, g, qi: (bi, qi, g)),
        pl.BlockSpec((1, s, width), lambda bi, g, qi: (bi, 0, n_groups + g)),
        pl.BlockSpec((1, s, width), lambda bi, g, qi: (bi, 0, 2 * n_groups + g)),
    ]
    args = [qkv, qkv, qkv]
    if cum_f is not None:
        heads = ATT_LANE_BLOCKS * HEADS_PER_BLOCK
        smem = pl.BlockSpec(memory_space=pltpu.SMEM)
        in_specs = [smem, smem] + in_specs + [
            pl.BlockSpec((1, 1, heads, s), lambda bi, g, qi: (bi, g, 0, 0)),
            pl.BlockSpec((1, bq, heads), lambda bi, g, qi: (bi * n_groups + g, qi, 0))]
        f_end = -cum_f[:, :, bq - 1::bq]
        by_head = cum_f.reshape(b, n_groups, heads, s)
        args = [zmax, f_end.reshape(-1)] + args + [
            by_head, jnp.swapaxes(by_head, 2, 3).reshape(b * n_groups, s, heads)]
    return pl.pallas_call(
        kernel,
        grid=(b, n_groups, s // bq),
        in_specs=in_specs,
        out_specs=pl.BlockSpec((1, bq, width), lambda bi, g, qi: (bi, qi, g)),
        out_shape=jax.ShapeDtypeStruct((b, s, W_BRANCH), BF16),
        compiler_params=_compiler_params(("parallel", "parallel", "arbitrary")),
        name=kernel.__name__.strip("_"),
    )(*args)


def _local_rows(tm):
    return TOP_K * tm + N_EXPERTS * MOE_CHUNK


def _route_and_sort(logits, h2, xs_ref, len_ref, route_ref):
    tm = logits.shape[0]
    lt = logits.T[:N_EXPERTS, :]
    expert = lax.broadcasted_iota(jnp.int32, lt.shape, 0).astype(F32)
    m1 = jnp.max(lt, axis=0, keepdims=True)
    i1 = jnp.min(jnp.where(lt == m1, expert, float(N_EXPERTS)), axis=0, keepdims=True)
    lt2 = jnp.where(expert == i1, -jnp.inf, lt)
    m2 = jnp.max(lt2, axis=0, keepdims=True)
    i2 = jnp.min(jnp.where(lt2 == m2, expert, float(N_EXPERTS)), axis=0, keepdims=True)
    e2 = jnp.exp(m2 - m1)
    w1 = 1.0 / (1.0 + e2)
    w2 = e2 * w1
    chosen = jnp.where((expert == i1) | (expert == i2), 1.0, 0.0)
    yield
    before = (lax.broadcasted_iota(jnp.int32, (tm, tm), 0)
              < lax.broadcasted_iota(jnp.int32, (tm, tm), 1))
    rank = _dot(chosen.astype(BF16), jnp.where(before, 1.0, 0.0).astype(BF16))
    count = jnp.sum(chosen, axis=1, keepdims=True)
    padded = jnp.floor((count + (MOE_CHUNK - 1)) * (1.0 / MOE_CHUNK)) * MOE_CHUNK
    r1 = jnp.sum(jnp.where(expert == i1, rank, 0.0), axis=0, keepdims=True)
    r2 = jnp.sum(jnp.where(expert == i2, rank, 0.0), axis=0, keepdims=True)
    start = jnp.zeros((1, 1), F32)
    for e in range(N_EXPERTS):
        r1 = r1 + jnp.where(i1 == e, start, 0.0)
        r2 = r2 + jnp.where(i2 == e, start, 0.0)
        start = start + padded[e:e + 1, :]
    rows = _local_rows(tm)
    row = lax.broadcasted_iota(jnp.int32, (rows, tm), 0).astype(F32)
    place = jnp.where((row == r1) | (row == r2), 1.0, 0.0).astype(BF16)
    yield
    xs_ref[0] = _dot(place, h2).astype(BF16)
    len_ref[0] = jnp.broadcast_to(padded, (N_EXPERTS, LANES))
    field = lax.broadcasted_iota(jnp.int32, (LANES, tm), 0)
    fields = jnp.where(field == 0, r1, jnp.where(field == 1, r2, jnp.where(
        field == 2, w1, jnp.where(field == 3, w2, 0.0))))
    route_ref[...] = fields.T


def _merge_kernel(x_ref, ysb_ref, yfx_ref, gmix_ref, wgate_ref, bgate_ref, wosb_ref, wofx_ref,
                  wout_ref, gffn_ref, *rest, with_router):
    if with_router:
        wr_hi_ref, wr_lo_ref, x1_ref, xs_ref, len_ref, route_ref, h2_keep, logits_keep = rest
        step = pl.program_id(0)
        keep = lax.rem(step, 2)

        @pl.when(step == 0)
        def _():
            h2_keep[...] = jnp.zeros_like(h2_keep)
            logits_keep[...] = jnp.zeros_like(logits_keep)

        routing = _route_and_sort(logits_keep[1 - keep], h2_keep[1 - keep], xs_ref, len_ref, route_ref)
        next(routing)
    else:
        x1_ref, h2_ref = rest
    d = x_ref.shape[1]
    x = x_ref[...]
    h = _rmsnorm(x, gmix_ref[...]).astype(BF16)
    merged = None
    for i, (y_ref, wo_ref) in enumerate(((ysb_ref, wosb_ref), (yfx_ref, wofx_ref))):
        gate = _sigmoid(_dot(h, wgate_ref[:, i * d:(i + 1) * d]) + bgate_ref[:, i * d:(i + 1) * d])
        branch = gate * _dot(y_ref[...], wo_ref[...])
        merged = branch if merged is None else merged + branch
    if with_router:
        next(routing)
    x1 = x + _dot(merged.astype(BF16), wout_ref[...])
    x1_ref[...] = x1
    h2 = _rmsnorm(x1, gffn_ref[...])
    if with_router:
        h_hi, h_lo = _split_bf16(h2, 2)
        logits_keep[keep] = (_dot(h_hi, wr_hi_ref[...]) + _dot(h_lo, wr_hi_ref[...])
                             + _dot(h_hi, wr_lo_ref[...]))
        h2_keep[keep] = h_hi
        next(routing, None)
    else:
        h2_ref[...] = h2.astype(BF16)


def _merge(x, y_sb, y_fx, g_mix, w_gate, b_gate, w_o_sb, w_o_fox, w_out, g_ffn, w_router, tm):
    t, d = x.shape
    with_router = w_router is not None
    n_tiles = t // tm
    full = lambda i: (0, 0)
    tile = lambda i: (jnp.minimum(i, n_tiles - 1), 0)
    routed = lambda i: (jnp.maximum(i - 1, 0), 0)
    scratch = []
    in_specs = [
        pl.BlockSpec((tm, d), tile),
        pl.BlockSpec((tm, W_BRANCH), tile),
        pl.BlockSpec((tm, W_BRANCH), tile),
        pl.BlockSpec((1, d), full),
        pl.BlockSpec((d, 2 * d), full),
        pl.BlockSpec((1, 2 * d), full),
        pl.BlockSpec((W_BRANCH, d), full),
        pl.BlockSpec((W_BRANCH, d), full),
        pl.BlockSpec((d, d), full),
        pl.BlockSpec((1, d), full),
    ]
    args = [x, y_sb, y_fx, g_mix, w_gate, b_gate, w_o_sb, w_o_fox, w_out, g_ffn]
    out_specs = [pl.BlockSpec((tm, d), tile)]
    out_shape = [jax.ShapeDtypeStruct((t, d), F32)]
    if with_router:
        rows = _local_rows(tm)
        in_specs += [pl.BlockSpec((d, LANES), full)] * 2
        args += list(w_router)
        out_specs += [pl.BlockSpec((1, rows, d), lambda i: routed(i) + (0,)),
                      pl.BlockSpec((1, N_EXPERTS, LANES), lambda i: routed(i) + (0,)),
                      pl.BlockSpec((tm, LANES), routed)]
        out_shape += [jax.ShapeDtypeStruct((n_tiles, rows, d), BF16),
                      jax.ShapeDtypeStruct((n_tiles, N_EXPERTS, LANES), F32),
                      jax.ShapeDtypeStruct((t, LANES), F32)]
        scratch = [pltpu.VMEM((2, tm, d), BF16), pltpu.VMEM((2, tm, LANES), F32)]
    else:
        out_specs.append(pl.BlockSpec((tm, d), tile))
        out_shape.append(jax.ShapeDtypeStruct((t, d), BF16))
    return pl.pallas_call(
        functools.partial(_merge_kernel, with_router=with_router),
        grid=(n_tiles + (1 if with_router else 0),),
        in_specs=in_specs,
        out_specs=out_specs,
        out_shape=out_shape,
        scratch_shapes=scratch,
        compiler_params=_compiler_params(("arbitrary",) if with_router else ("parallel",)),
        name="merge_router" if with_router else "merge",
    )(*args)


def _swiglu_chunk(h, wg, wu, wd):
    g = _dot(h, wg)
    u = _dot(h, wu)
    return _dot((g * _sigmoid(g) * u).astype(BF16), wd)


def _ffn_kernel(h_ref, x_ref, wg_ref, wu_ref, wd_ref, o_ref):
    @pl.when(pl.program_id(1) == 0)
    def _():
        o_ref[...] = x_ref[...]

    o_ref[...] += _swiglu_chunk(h_ref[...], wg_ref[...], wu_ref[...], wd_ref[...])


def _ffn(h, x, w_gu, w_dn, tm, tf):
    t, d = x.shape
    d_ff = w_dn.shape[0]
    nf = d_ff // tf
    return pl.pallas_call(
        _ffn_kernel,
        grid=(t // tm, nf),
        in_specs=[
            pl.BlockSpec((tm, d), lambda i, j: (i, 0)),
            pl.BlockSpec((tm, d), lambda i, j: (i, 0)),
            pl.BlockSpec((d, tf), lambda i, j: (0, j)),
            pl.BlockSpec((d, tf), lambda i, j: (0, nf + j)),
            pl.BlockSpec((tf, d), lambda i, j: (j, 0)),
        ],
        out_specs=pl.BlockSpec((tm, d), lambda i, j: (i, 0)),
        out_shape=jax.ShapeDtypeStruct((t, d), F32),
        compiler_params=_compiler_params(("parallel", "arbitrary")),
        name="ffn",
    )(h, x, w_gu, w_gu, w_dn)


def _expert_kernel(expert_ref, valid_ref, src_ref, xs_hbm, wg_ref, wu_ref, wd_ref, o_ref, buf, sems):
    j = pl.program_id(0)
    rows = buf.shape[1]
    chunks = rows // MOE_CHUNK

    def fetches(step, slot):
        return [pltpu.make_async_copy(
            xs_hbm.at[src_ref[step * chunks + c]],
            buf.at[slot, pl.ds(c * MOE_CHUNK, MOE_CHUNK), :], sems.at[slot]) for c in range(chunks)]

    slot = lax.rem(j, 2)

    @pl.when(j == 0)
    def _():
        for copy in fetches(0, 0):
            copy.start()

    @pl.when(j + 1 < pl.num_programs(0))
    def _():
        for copy in fetches(j + 1, 1 - slot):
            copy.start()

    for copy in fetches(j, slot):
        copy.wait()

    @pl.when(valid_ref[j] != 0)
    def _():
        o_ref[...] = _swiglu_chunk(buf[slot], wg_ref[0], wu_ref[0], wd_ref[0]).astype(BF16)

    @pl.when(valid_ref[j] == 0)
    def _():
        o_ref[...] = jnp.zeros_like(o_ref)


def _experts(xs, tile_expert, tile_valid, chunk_src, w_gu, w_dn, n_rows):
    d = xs.shape[2]
    d_ff = w_dn.shape[1]
    te = MOE_EXPERT_TILE
    grid_spec = pltpu.PrefetchScalarGridSpec(
        num_scalar_prefetch=3,
        grid=(n_rows // te,),
        in_specs=[
            pl.BlockSpec(memory_space=pl.ANY),
            pl.BlockSpec((1, d, d_ff), lambda j, ex, va, sr: (ex[j], 0, 0)),
            pl.BlockSpec((1, d, d_ff), lambda j, ex, va, sr: (ex[j], 0, 1)),
            pl.BlockSpec((1, d_ff, d), lambda j, ex, va, sr: (ex[j], 0, 0)),
        ],
        out_specs=pl.BlockSpec((te, d), lambda j, ex, va, sr: (j, 0)),
        scratch_shapes=[pltpu.VMEM((2, te, d), BF16), pltpu.SemaphoreType.DMA((2,))],
    )
    return pl.pallas_call(
        _expert_kernel,
        grid_spec=grid_spec,
        out_shape=jax.ShapeDtypeStruct((n_rows, d), BF16),
        compiler_params=_compiler_params(("arbitrary",)),
        name="experts",
    )(tile_expert, tile_valid, chunk_src, xs, w_gu, w_gu, w_dn)


def _combine_kernel(goff_ref, count_ref, loff_ref, x_ref, route_ref, ys_hbm, o_ref, buf, sems):
    i = pl.program_id(0)
    slot = lax.rem(i, 2)

    def for_each_chunk(step, which, act):
        for e in range(N_EXPERTS):
            g = step * N_EXPERTS + e

            @pl.loop(0, count_ref[g])
            def _(c):
                dst = pl.multiple_of((loff_ref[g] + c) * MOE_CHUNK, MOE_CHUNK)
                act(pltpu.make_async_copy(ys_hbm.at[goff_ref[g] + c],
                                          buf.at[which, pl.ds(dst, MOE_CHUNK), :], sems.at[which]))

    @pl.when(i == 0)
    def _():
        buf[...] = jnp.zeros_like(buf)
        for_each_chunk(0, 0, lambda copy: copy.start())

    @pl.when(i + 1 < pl.num_programs(0))
    def _():
        for_each_chunk(i + 1, 1 - slot, lambda copy: copy.start())

    for_each_chunk(i, slot, lambda copy: copy.wait())

    route = route_ref[...]
    y = buf[slot]
    col = lax.broadcasted_iota(jnp.int32, (route.shape[0], buf.shape[1]), 1).astype(F32)
    out = x_ref[...]
    for k in range(TOP_K):
        pick = jnp.where(col == route[:, k:k + 1], 1.0, 0.0).astype(BF16)
        out = out + route[:, TOP_K + k:TOP_K + k + 1] * _dot(pick, y)
    o_ref[...] = out


def _combine(x, route, ys, goff, count, loff, tm):
    t, d = x.shape
    grid_spec = pltpu.PrefetchScalarGridSpec(
        num_scalar_prefetch=3,
        grid=(t // tm,),
        in_specs=[
            pl.BlockSpec((tm, d), lambda i, *_: (i, 0)),
            pl.BlockSpec((tm, LANES), lambda i, *_: (i, 0)),
            pl.BlockSpec(memory_space=pl.ANY),
        ],
        out_specs=pl.BlockSpec((tm, d), lambda i, *_: (i, 0)),
        scratch_shapes=[pltpu.VMEM((2, _local_rows(tm), d), BF16), pltpu.SemaphoreType.DMA((2,))],
    )
    return pl.pallas_call(
        _combine_kernel,
        grid_spec=grid_spec,
        out_shape=jax.ShapeDtypeStruct((t, d), F32),
        compiler_params=_compiler_params(("arbitrary",)),
        name="combine",
    )(goff, count, loff, x, route, ys)


def _moe(x1, xs_local, padded_len, route, w_gu, w_dn, tm):
    t, d = x1.shape
    n_tiles, local_rows, _ = xs_local.shape
    te = MOE_EXPERT_TILE
    n_rows = -(-(TOP_K * t + n_tiles * N_EXPERTS * (MOE_CHUNK - 1) + N_EXPERTS * (te - 1)) // te) * te
    length = padded_len[:, :, 0].astype(jnp.int32)
    count = length // MOE_CHUNK
    used = jnp.sum(length, axis=0)
    span = -(-used // te) * te
    base = jnp.cumsum(span) - span
    goff = (base[None, :] + jnp.cumsum(length, axis=0) - length) // MOE_CHUNK
    loff = (jnp.cumsum(length, axis=1) - length) // MOE_CHUNK
    seg_start = goff.T.reshape(-1)
    seg_count = count.T.reshape(-1)
    seg_src = (jnp.arange(n_tiles, dtype=jnp.int32)[:, None] * (local_rows // MOE_CHUNK) + loff).T.reshape(-1)
    chunk = jnp.arange(n_rows // MOE_CHUNK, dtype=jnp.int32)[:, None]
    within = chunk - seg_start[None, :]
    owner = (within >= 0) & (within < seg_count[None, :])
    chunk_src = jnp.sum(jnp.where(owner, seg_src[None, :] + within, 0), axis=1).astype(jnp.int32)
    tile_row = jnp.arange(n_rows // te, dtype=jnp.int32)[:, None] * te
    tile_expert = (jnp.sum(base[None, :] <= tile_row, axis=1) - 1).astype(jnp.int32)
    tile_valid = jnp.any((base[None, :] <= tile_row) & (tile_row < (base + used)[None, :]),
                         axis=1).astype(jnp.int32)
    ys = _experts(xs_local.reshape(-1, MOE_CHUNK, d), tile_expert, tile_valid, chunk_src, w_gu, w_dn, n_rows)
    return _combine(x1, route, ys.reshape(-1, MOE_CHUNK, d), goff.reshape(-1), count.reshape(-1),
                    loff.reshape(-1), tm)


def _pad_lanes(w):
    return jnp.pad(w, ((0, 0), (0, LANES - w.shape[1])))


def kernel(x, g_mix, w_in, b_f, b_gate, g_q, g_k, w_o_sb, w_o_fox, w_out, g_ffn, w_gu_dense,
           w_dn_dense, w_router, w_gu_exp, w_dn_exp):
    batch, seq, d = x.shape
    depth = w_in.shape[0]
    t = batch * seq
    tm_proj = min(512, t)
    tm_ffn = min(1024, t)
    n_qkv = 6 * W_BRANCH
    xt = x.reshape(t, d)
    for l in range(depth):
        w_l = w_in[l]
        w_proj = jnp.concatenate(
            [w_l[:, :n_qkv], _pad_lanes(w_l[:, n_qkv:n_qkv + N_HEADS])], axis=1).astype(BF16)
        w_gate = w_l[:, n_qkv + N_HEADS:].astype(BF16)
        row = lambda v: v.reshape(1, -1)
        gq = jnp.tile(g_q[l], N_HEADS).reshape(1, W_BRANCH)
        gk = jnp.tile(g_k[l], N_HEADS).reshape(1, W_BRANCH)
        qkv_sb, qkv_fx, f_pre = _proj(xt, row(g_mix[l]), w_proj, gq, gk, tm_proj)
        cum_f = _cumf(f_pre.reshape(batch, seq, LANES), _pad_lanes(row(b_f[l])))
        y_sb = _attention(_sb_kernel, qkv_sb.reshape(batch, seq, 3 * W_BRANCH))
        zmax = (QK_BOUND_MARGIN * 8.0 * jnp.max(jnp.abs(g_q[l])) * jnp.max(jnp.abs(g_k[l]))).reshape(1)
        y_fx = _attention(_fox_kernel, qkv_fx.reshape(batch, seq, 3 * W_BRANCH), cum_f, zmax)
        moe_layer = l % 2 == 1
        router = None
        if moe_layer:
            wr = _pad_lanes(w_router[l // 2])
            router = _split_bf16(wr, 2)
        outs = _merge(xt, y_sb.reshape(t, W_BRANCH), y_fx.reshape(t, W_BRANCH), row(g_mix[l]),
                      w_gate, row(b_gate[l]), w_o_sb[l].astype(BF16), w_o_fox[l].astype(BF16),
                      w_out[l].astype(BF16), row(g_ffn[l]), router, tm_proj)
        if moe_layer:
            x1, xs_local, padded_len, route = outs
            xt = _moe(x1, xs_local, padded_len, route, w_gu_exp[l // 2].astype(BF16),
                      w_dn_exp[l // 2].astype(BF16), tm_proj)
        else:
            x1, h2 = outs
            xt = _ffn(h2, x1, w_gu_dense[l // 2].astype(BF16), w_dn_dense[l // 2].astype(BF16),
                      tm_proj, w_dn_dense.shape[1] // 2)
    return xt.reshape(batch, seq, d)
```

```python
import functools

import jax
import jax.numpy as jnp
from jax import lax
from jax.experimental import pallas as pl
from jax.experimental.pallas import tpu as pltpu

F32 = jnp.float32
BF16 = jnp.bfloat16

HEAD_DIM = 64
N_HEADS = 8
W_BRANCH = N_HEADS * HEAD_DIM
N_EXPERTS = 8
TOP_K = 2
MOE_CHUNK = 16
MOE_EXPERT_TILE = 512
RMS_EPS = 1e-6
QK_SCALE = 1.0 / 8.0
LANES = 128
HEADS_PER_BLOCK = LANES // HEAD_DIM
V7X_VMEM_LIMIT_BYTES = 56 * 1024 * 1024

ATT_BLOCK = 256
ATT_LANE_BLOCKS = 4
F32_EXP_UNDERFLOW = 106.0
FOX_CONSTANT_SHIFT_MAX = 30.0
QK_BOUND_MARGIN = 1.05
CUM_BLOCK = 512


def _split_bf16(x, parts):
    out = []
    for _ in range(parts - 1):
        hi = x.astype(BF16)
        out.append(hi)
        x = x - hi.astype(F32)
    out.append(x.astype(BF16))
    return out


def _dot(a, b):
    return jnp.dot(a, b, preferred_element_type=F32)


def _dot_nt(a, b):
    return lax.dot_general(a, b, (((1,), (1,)), ((), ())), preferred_element_type=F32)


def _rmsnorm(x, g):
    ms = jnp.mean(x * x, axis=-1, keepdims=True)
    return x * lax.rsqrt(ms + RMS_EPS) * g


def _sigmoid(x):
    return 1.0 / (1.0 + jnp.exp(-x))


def _compiler_params(semantics):
    return pltpu.CompilerParams(dimension_semantics=semantics,
                                vmem_limit_bytes=V7X_VMEM_LIMIT_BYTES)


def _proj_kernel(x_ref, g_ref, w_ref, gq_ref, gk_ref, sb_ref, fx_ref, f_ref):
    h = _rmsnorm(x_ref[...], g_ref[...]).astype(BF16)
    wb = W_BRANCH
    r = lax.broadcasted_iota(jnp.int32, (wb, wb), 0) // HEAD_DIM
    c = lax.broadcasted_iota(jnp.int32, (wb, wb), 1) // HEAD_DIM
    head_mean = jnp.where(r == c, 1.0 / HEAD_DIM, 0.0).astype(BF16)
    for i in range(3):
        acc = _dot(h, w_ref[:, i * wb:(i + 1) * wb])
        if i == 0:
            acc = acc * QK_SCALE
        sb_ref[:, i * wb:(i + 1) * wb] = acc.astype(BF16)
    for i, gain_ref in enumerate((gq_ref, gk_ref, None)):
        acc = _dot(h, w_ref[:, (3 + i) * wb:(4 + i) * wb])
        if gain_ref is not None:
            ms = _dot((acc * acc).astype(BF16), head_mean)
            acc = acc * lax.rsqrt(ms + RMS_EPS) * gain_ref[...]
        if i == 0:
            acc = acc * QK_SCALE
        fx_ref[:, i * wb:(i + 1) * wb] = acc.astype(BF16)
    f_ref[...] = _dot(h, w_ref[:, 6 * wb:6 * wb + LANES])


def _proj(x, g, w, gq, gk, tm):
    t, d = x.shape
    n = w.shape[1]
    full = lambda i: (0, 0)
    return pl.pallas_call(
        _proj_kernel,
        grid=(t // tm,),
        in_specs=[
            pl.BlockSpec((tm, d), lambda i: (i, 0)),
            pl.BlockSpec((1, d), full),
            pl.BlockSpec((d, n), full),
            pl.BlockSpec((1, W_BRANCH), full),
            pl.BlockSpec((1, W_BRANCH), full),
        ],
        out_specs=[
            pl.BlockSpec((tm, 3 * W_BRANCH), lambda i: (i, 0)),
            pl.BlockSpec((tm, 3 * W_BRANCH), lambda i: (i, 0)),
            pl.BlockSpec((tm, LANES), lambda i: (i, 0)),
        ],
        out_shape=[
            jax.ShapeDtypeStruct((t, 3 * W_BRANCH), BF16),
            jax.ShapeDtypeStruct((t, 3 * W_BRANCH), BF16),
            jax.ShapeDtypeStruct((t, LANES), F32),
        ],
        compiler_params=_compiler_params(("parallel",)),
        name="proj",
    )(x, g, w, gq, gk)


def _cumf_kernel(f_ref, b_ref, o_ref):
    seq = f_ref.shape[1]
    cb = min(CUM_BLOCK, seq)
    r = lax.broadcasted_iota(jnp.int32, (cb, cb), 0)
    c = lax.broadcasted_iota(jnp.int32, (cb, cb), 1)
    prefix = jnp.where(r <= c, 1.0, 0.0).astype(BF16)
    carry = jnp.zeros((N_HEADS, 1), F32)
    for i in range(seq // cb):
        v = f_ref[0, i * cb:(i + 1) * cb, :] + b_ref[...]
        log_f = jnp.minimum(v, 0.0) - jnp.log1p(jnp.exp(-jnp.abs(v)))
        log_f = log_f.T[:N_HEADS, :]
        cum = carry
        for part in _split_bf16(log_f, 3):
            cum = cum + _dot(part, prefix)
        o_ref[0, :, i * cb:(i + 1) * cb] = cum
        carry = cum[:, cb - 1:cb]


def _cumf(f_pre, b_f):
    b, s, _ = f_pre.shape
    return pl.pallas_call(
        _cumf_kernel,
        grid=(b,),
        in_specs=[
            pl.BlockSpec((1, s, LANES), lambda i: (i, 0, 0)),
            pl.BlockSpec((1, LANES), lambda i: (0, 0)),
        ],
        out_specs=pl.BlockSpec((1, N_HEADS, s), lambda i: (i, 0, 0)),
        out_shape=jax.ShapeDtypeStruct((b, N_HEADS, s), F32),
        compiler_params=_compiler_params(("parallel",)),
        name="cumf",
    )(f_pre, b_f)


def _head_queries(q_ref, half):
    q = q_ref[0, :, half * LANES:(half + 1) * LANES]
    return [q * _head_lanes(hh) for hh in range(HEADS_PER_BLOCK)]


def _head_lanes(hh):
    lane = lax.broadcasted_iota(jnp.int32, (1, LANES), 1)
    return jnp.where((lane >= hh * HEAD_DIM) & (lane < (hh + 1) * HEAD_DIM), 1.0, 0.0).astype(BF16)


class _Walk:
    def __init__(self, q_ref, k_ref, v_ref, o_ref, stages, finish, stop):
        self.q_ref, self.k_ref, self.v_ref, self.o_ref = q_ref, k_ref, v_ref, o_ref
        self.stages, self.finish, self.stop = stages, finish, stop


def _run_walks(walks):
    bk = ATT_BLOCK
    qi = pl.program_id(2)
    halves = [range(w.q_ref.shape[2] // LANES) for w in walks]
    queries = [[q for half in hs for q in _head_queries(w.q_ref, half)] for w, hs in zip(walks, halves)]

    def advance(active, blocks, masked, states):
        jobs = []
        for wi in active:
            w = walks[wi]
            start = pl.multiple_of(blocks[wi] * bk, bk)
            kv = [(w.k_ref[0, pl.ds(start, bk), half * LANES:(half + 1) * LANES],
                   w.v_ref[0, pl.ds(start, bk), half * LANES:(half + 1) * LANES]) for half in halves[wi]]
            jobs += [(wi, head, (head, q, *kv[head // HEADS_PER_BLOCK], start, masked))
                     for head, q in enumerate(queries[wi])]
        tmps = [None] * len(jobs)
        out = [list(s) for s in states]
        for s in range(max(len(walks[wi].stages) for wi in active)):
            for j, (wi, head, args) in enumerate(jobs):
                stages = walks[wi].stages
                if s < len(stages) - 1:
                    tmps[j] = stages[s](*args, tmps[j])
                elif s == len(stages) - 1:
                    out[wi][head] = stages[s](*args, tmps[j], states[wi][head])
        return tuple(tuple(s) for s in out)

    n = len(walks)
    everyone = tuple(range(n))
    states = advance(everyone, [qi] * n, True, [[None] * len(q) for q in queries])

    def phase(active, carry):
        def cond(c):
            trips, done, _ = c
            go = [jnp.logical_and(trips[wi] < qi, done[wi] == 0) for wi in active]
            return functools.reduce(jnp.logical_and, go)

        def body(c):
            trips, done, states = c
            blocks = [qi - 1 - t for t in trips]
            states = advance(active, blocks, False, states)
            trips = tuple(t + 1 if wi in active else t for wi, t in enumerate(trips))
            done = tuple(walks[wi].stop(states[wi], jnp.maximum(blocks[wi] - 1, 0)).astype(jnp.int32)
                         if wi in active else d for wi, d in enumerate(done))
            return trips, done, states

        return lax.while_loop(cond, body, carry)

    zeros = tuple(jnp.int32(0) for _ in walks)
    carry = phase(everyone, (zeros, zeros, states))
    if n > 1:
        for wi in everyone:
            carry = phase((wi,), carry)
    states = carry[2]
    lane = lax.broadcasted_iota(jnp.int32, (1, LANES), 1)
    for wi, w in enumerate(walks):
        for half in halves[wi]:
            outs = [w.finish(head, states[wi][head])
                    for head in range(half * HEADS_PER_BLOCK, (half + 1) * HEADS_PER_BLOCK)]
            w.o_ref[0, :, half * LANES:(half + 1) * LANES] = jnp.where(
                lane < HEAD_DIM, outs[0], outs[1]).astype(BF16)


def _sb_walk(q_ref, k_ref, v_ref, o_ref):
    bk = ATT_BLOCK
    row = lax.broadcasted_iota(jnp.int32, (bk, bk), 0)
    col = lax.broadcasted_iota(jnp.int32, (bk, bk), 1)
    suffix = jnp.where(row >= col, 1.0, 0.0).astype(BF16)
    below_diag = col < row

    def scores(head, q, k, v, start, masked, tmp):
        z = _dot_nt(q, k)
        neg_abs = lax.bitcast_convert_type(
            lax.bitcast_convert_type(z, jnp.uint32) | jnp.uint32(0x80000000), F32)
        sp = jnp.maximum(z, 0.0) + jnp.log(1.0 + jnp.exp(neg_abs))
        if masked:
            sp = jnp.where(below_diag, sp, 0.0)
        return z, sp.astype(BF16)

    def suffix_sums(head, q, k, v, start, masked, tmp):
        z, sp_bf16 = tmp
        return z, _dot(sp_bf16, suffix)

    def weights(head, q, k, v, start, masked, tmp, state):
        z, incl = tmp
        arg = z - incl
        if state is not None:
            arg = arg - state[0]
        w = jnp.exp(arg)
        if masked:
            w = jnp.where(below_diag, w, 0.0)
        pv = _dot(w.astype(BF16), v)
        if state is None:
            return incl[:, 0:1], pv
        return state[0] + incl[:, 0:1], state[1] + pv

    def stop(states, next_block):
        carry = functools.reduce(jnp.minimum, [state[0] for state in states])
        return jnp.min(carry) > F32_EXP_UNDERFLOW

    return _Walk(q_ref, k_ref, v_ref, o_ref, (scores, suffix_sums, weights),
                 lambda head, state: state[1], stop)


def _fox_walk(constant_shift, zmax_ref, fend_ref, q_ref, k_ref, v_ref, f_ref, fcol_ref, o_ref):
    bk = ATT_BLOCK
    n_blocks = k_ref.shape[1] // bk
    n_heads = f_ref.shape[2]
    qi = pl.program_id(2)
    row = lax.broadcasted_iota(jnp.int32, (bk, bk), 0)
    col = lax.broadcasted_iota(jnp.int32, (bk, bk), 1)
    causal = col <= row
    table = (pl.program_id(0) * (W_BRANCH // HEAD_DIM) + pl.program_id(1) * n_heads) * n_blocks

    values = {}

    def values_with_ones(hh, v):
        if (hh, id(v)) not in values:
            values[(hh, id(v))] = (v, v * _head_lanes(hh) + _head_lanes(1 - hh))
        return values[(hh, id(v))][1]

    def normalised(head, acc):
        ones_lane = (1 - head % HEADS_PER_BLOCK) * HEAD_DIM
        return acc / acc[:, ones_lane:ones_lane + 1]

    if constant_shift:
        shift = [fcol_ref[0, :, head:head + 1] - zmax_ref[0] for head in range(n_heads)]

        def weights(head, q, k, v, start, masked, tmp):
            s = _dot_nt(q, k) - f_ref[0, 0, head:head + 1, pl.ds(start, bk)] + shift[head]
            if masked:
                s = jnp.where(causal, s, -jnp.inf)
            return jnp.exp(s).astype(BF16)

        def accumulate(head, q, k, v, start, masked, tmp, state):
            pv = _dot(tmp, values_with_ones(head % HEADS_PER_BLOCK, v))
            return pv if state is None else state + pv

        def stop(states, next_block):
            last = jnp.maximum(qi - 1, 0)
            gaps = [fend_ref[table + head * n_blocks + next_block] - fend_ref[table + head * n_blocks + last]
                    for head in range(n_heads)]
            return functools.reduce(jnp.maximum, gaps) < -F32_EXP_UNDERFLOW

        return _Walk(q_ref, k_ref, v_ref, o_ref, (weights, accumulate), normalised, stop)

    def logits(head, q, k, v, start, masked, tmp):
        s = _dot_nt(q, k) - f_ref[0, 0, head:head + 1, pl.ds(start, bk)]
        if masked:
            s = jnp.where(causal, s, -jnp.inf)
        return s, jnp.max(s, axis=1, keepdims=True)

    def accumulate(head, q, k, v, start, masked, tmp, state):
        s, m_blk = tmp
        v_ones = values_with_ones(head % HEADS_PER_BLOCK, v)
        if state is None:
            return m_blk, _dot(jnp.exp(s - m_blk).astype(BF16), v_ones)
        m, acc = state
        m_new = jnp.maximum(m, m_blk)
        pv = _dot(jnp.exp(s - m_new).astype(BF16), v_ones)
        return m_new, jnp.exp(m - m_new) * acc + pv

    def stop(states, next_block):
        slack = functools.reduce(jnp.minimum, [
            state[0] - fend_ref[table + head * n_blocks + next_block]
            for head, state in enumerate(states)])
        return jnp.min(slack) > zmax_ref[0] + F32_EXP_UNDERFLOW

    return _Walk(q_ref, k_ref, v_ref, o_ref, (logits, accumulate),
                 lambda head, state: normalised(head, state[1]), stop)


def _attention_kernel(zmax_ref, fend_ref, qs_ref, ks_ref, vs_ref, qf_ref, kf_ref, vf_ref, f_ref,
                      fcol_ref, os_ref, of_ref):
    @pl.when(zmax_ref[0] <= FOX_CONSTANT_SHIFT_MAX)
    def _():
        _run_walks([_sb_walk(qs_ref, ks_ref, vs_ref, os_ref),
                    _fox_walk(True, zmax_ref, fend_ref, qf_ref, kf_ref, vf_ref, f_ref, fcol_ref, of_ref)])

    @pl.when(zmax_ref[0] > FOX_CONSTANT_SHIFT_MAX)
    def _():
        _run_walks([_sb_walk(qs_ref, ks_ref, vs_ref, os_ref)])
        _run_walks([_fox_walk(False, zmax_ref, fend_ref, qf_ref, kf_ref, vf_ref, f_ref, fcol_ref, of_ref)])


def _attention(qkv_sb, qkv_fx, cum_f, zmax):
    b, s, _ = qkv_sb.shape
    bq = min(ATT_BLOCK, s)
    width = ATT_LANE_BLOCKS * LANES
    n_groups = W_BRANCH // width
    heads = ATT_LANE_BLOCKS * HEADS_PER_BLOCK
    qkv_specs = [
        pl.BlockSpec((1, bq, width), lambda bi, g, qi: (bi, qi, g)),
        pl.BlockSpec((1, s, width), lambda bi, g, qi: (bi, 0, n_groups + g)),
        pl.BlockSpec((1, s, width), lambda bi, g, qi: (bi, 0, 2 * n_groups + g)),
    ]
    smem = pl.BlockSpec(memory_space=pltpu.SMEM)
    f_end = -cum_f[:, :, bq - 1::bq]
    by_head = cum_f.reshape(b, n_groups, heads, s)
    out_spec = pl.BlockSpec((1, bq, width), lambda bi, g, qi: (bi, qi, g))
    return pl.pallas_call(
        _attention_kernel,
        grid=(b, n_groups, s // bq),
        in_specs=[smem, smem] + qkv_specs + qkv_specs + [
            pl.BlockSpec((1, 1, heads, s), lambda bi, g, qi: (bi, g, 0, 0)),
            pl.BlockSpec((1, bq, heads), lambda bi, g, qi: (bi * n_groups + g, qi, 0))],
        out_specs=[out_spec, out_spec],
        out_shape=[jax.ShapeDtypeStruct((b, s, W_BRANCH), BF16)] * 2,
        compiler_params=_compiler_params(("parallel", "parallel", "arbitrary")),
        name="attention",
    )(zmax, f_end.reshape(-1), qkv_sb, qkv_sb, qkv_sb, qkv_fx, qkv_fx, qkv_fx, by_head,
      jnp.swapaxes(by_head, 2, 3).reshape(b * n_groups, s, heads))


def _local_rows(tm):
    return TOP_K * tm + N_EXPERTS * MOE_CHUNK


def _route_and_sort(logits, h2, xs_ref, len_ref, route_ref):
    tm = logits.shape[0]
    lt = logits.T[:N_EXPERTS, :]
    expert = lax.broadcasted_iota(jnp.int32, lt.shape, 0).astype(F32)
    m1 = jnp.max(lt, axis=0, keepdims=True)
    i1 = jnp.min(jnp.where(lt == m1, expert, float(N_EXPERTS)), axis=0, keepdims=True)
    lt2 = jnp.where(expert == i1, -jnp.inf, lt)
    m2 = jnp.max(lt2, axis=0, keepdims=True)
    i2 = jnp.min(jnp.where(lt2 == m2, expert, float(N_EXPERTS)), axis=0, keepdims=True)
    e2 = jnp.exp(m2 - m1)
    w1 = 1.0 / (1.0 + e2)
    w2 = e2 * w1
    chosen = jnp.where((expert == i1) | (expert == i2), 1.0, 0.0)
    yield
    before = (lax.broadcasted_iota(jnp.int32, (tm, tm), 0)
              < lax.broadcasted_iota(jnp.int32, (tm, tm), 1))
    rank = _dot(chosen.astype(BF16), jnp.where(before, 1.0, 0.0).astype(BF16))
    count = jnp.sum(chosen, axis=1, keepdims=True)
    padded = jnp.floor((count + (MOE_CHUNK - 1)) * (1.0 / MOE_CHUNK)) * MOE_CHUNK
    r1 = jnp.sum(jnp.where(expert == i1, rank, 0.0), axis=0, keepdims=True)
    r2 = jnp.sum(jnp.where(expert == i2, rank, 0.0), axis=0, keepdims=True)
    start = jnp.zeros((1, 1), F32)
    for e in range(N_EXPERTS):
        r1 = r1 + jnp.where(i1 == e, start, 0.0)
        r2 = r2 + jnp.where(i2 == e, start, 0.0)
        start = start + padded[e:e + 1, :]
    rows = _local_rows(tm)
    row = lax.broadcasted_iota(jnp.int32, (rows, tm), 0).astype(F32)
    place = jnp.where((row == r1) | (row == r2), 1.0, 0.0).astype(BF16)
    yield
    xs_ref[0] = _dot(place, h2).astype(BF16)
    len_ref[0] = jnp.broadcast_to(padded, (N_EXPERTS, LANES))
    field = lax.broadcasted_iota(jnp.int32, (LANES, tm), 0)
    fields = jnp.where(field == 0, r1, jnp.where(field == 1, r2, jnp.where(
        field == 2, w1, jnp.where(field == 3, w2, 0.0))))
    route_ref[...] = fields.T


def _merge_kernel(x_ref, ysb_ref, yfx_ref, gmix_ref, wgate_ref, bgate_ref, wosb_ref, wofx_ref,
                  wout_ref, gffn_ref, *rest, with_router):
    if with_router:
        wr_hi_ref, wr_lo_ref, x1_ref, xs_ref, len_ref, route_ref, h2_keep, logits_keep = rest
        step = pl.program_id(0)
        keep = lax.rem(step, 2)

        @pl.when(step == 0)
        def _():
            h2_keep[...] = jnp.zeros_like(h2_keep)
            logits_keep[...] = jnp.zeros_like(logits_keep)

        routing = _route_and_sort(logits_keep[1 - keep], h2_keep[1 - keep], xs_ref, len_ref, route_ref)
        next(routing)
    else:
        x1_ref, h2_ref = rest
    d = x_ref.shape[1]
    x = x_ref[...]
    h = _rmsnorm(x, gmix_ref[...]).astype(BF16)
    merged = None
    for i, (y_ref, wo_ref) in enumerate(((ysb_ref, wosb_ref), (yfx_ref, wofx_ref))):
        gate = _sigmoid(_dot(h, wgate_ref[:, i * d:(i + 1) * d]) + bgate_ref[:, i * d:(i + 1) * d])
        branch = gate * _dot(y_ref[...], wo_ref[...])
        merged = branch if merged is None else merged + branch
    if with_router:
        next(routing)
    x1 = x + _dot(merged.astype(BF16), wout_ref[...])
    x1_ref[...] = x1
    h2 = _rmsnorm(x1, gffn_ref[...])
    if with_router:
        h_hi, h_lo = _split_bf16(h2, 2)
        logits_keep[keep] = (_dot(h_hi, wr_hi_ref[...]) + _dot(h_lo, wr_hi_ref[...])
                             + _dot(h_hi, wr_lo_ref[...]))
        h2_keep[keep] = h_hi
        next(routing, None)
    else:
        h2_ref[...] = h2.astype(BF16)


def _merge(x, y_sb, y_fx, g_mix, w_gate, b_gate, w_o_sb, w_o_fox, w_out, g_ffn, w_router, tm):
    t, d = x.shape
    with_router = w_router is not None
    n_tiles = t // tm
    full = lambda i: (0, 0)
    tile = lambda i: (jnp.minimum(i, n_tiles - 1), 0)
    routed = lambda i: (jnp.maximum(i - 1, 0), 0)
    scratch = []
    in_specs = [
        pl.BlockSpec((tm, d), tile),
        pl.BlockSpec((tm, W_BRANCH), tile),
        pl.BlockSpec((tm, W_BRANCH), tile),
        pl.BlockSpec((1, d), full),
        pl.BlockSpec((d, 2 * d), full),
        pl.BlockSpec((1, 2 * d), full),
        pl.BlockSpec((W_BRANCH, d), full),
        pl.BlockSpec((W_BRANCH, d), full),
        pl.BlockSpec((d, d), full),
        pl.BlockSpec((1, d), full),
    ]
    args = [x, y_sb, y_fx, g_mix, w_gate, b_gate, w_o_sb, w_o_fox, w_out, g_ffn]
    out_specs = [pl.BlockSpec((tm, d), tile)]
    out_shape = [jax.ShapeDtypeStruct((t, d), F32)]
    if with_router:
        rows = _local_rows(tm)
        in_specs += [pl.BlockSpec((d, LANES), full)] * 2
        args += list(w_router)
        out_specs += [pl.BlockSpec((1, rows, d), lambda i: routed(i) + (0,)),
                      pl.BlockSpec((1, N_EXPERTS, LANES), lambda i: routed(i) + (0,)),
                      pl.BlockSpec((tm, LANES), routed)]
        out_shape += [jax.ShapeDtypeStruct((n_tiles, rows, d), BF16),
                      jax.ShapeDtypeStruct((n_tiles, N_EXPERTS, LANES), F32),
                      jax.ShapeDtypeStruct((t, LANES), F32)]
        scratch = [pltpu.VMEM((2, tm, d), BF16), pltpu.VMEM((2, tm, LANES), F32)]
    else:
        out_specs.append(pl.BlockSpec((tm, d), tile))
        out_shape.append(jax.ShapeDtypeStruct((t, d), BF16))
    return pl.pallas_call(
        functools.partial(_merge_kernel, with_router=with_router),
        grid=(n_tiles + (1 if with_router else 0),),
        in_specs=in_specs,
        out_specs=out_specs,
        out_shape=out_shape,
        scratch_shapes=scratch,
        compiler_params=_compiler_params(("arbitrary",) if with_router else ("parallel",)),
        name="merge_router" if with_router else "merge",
    )(*args)


def _swiglu_chunk(h, wg, wu, wd):
    g = _dot(h, wg)
    u = _dot(h, wu)
    return _dot((g * _sigmoid(g) * u).astype(BF16), wd)


def _ffn_kernel(h_ref, x_ref, wg_ref, wu_ref, wd_ref, o_ref):
    @pl.when(pl.program_id(1) == 0)
    def _():
        o_ref[...] = x_ref[...]

    o_ref[...] += _swiglu_chunk(h_ref[...], wg_ref[...], wu_ref[...], wd_ref[...])


def _ffn(h, x, w_gu, w_dn, tm, tf):
    t, d = x.shape
    d_ff = w_dn.shape[0]
    nf = d_ff // tf
    return pl.pallas_call(
        _ffn_kernel,
        grid=(t // tm, nf),
        in_specs=[
            pl.BlockSpec((tm, d), lambda i, j: (i, 0)),
            pl.BlockSpec((tm, d), lambda i, j: (i, 0)),
            pl.BlockSpec((d, tf), lambda i, j: (0, j)),
            pl.BlockSpec((d, tf), lambda i, j: (0, nf + j)),
            pl.BlockSpec((tf, d), lambda i, j: (j, 0)),
        ],
        out_specs=pl.BlockSpec((tm, d), lambda i, j: (i, 0)),
        out_shape=jax.ShapeDtypeStruct((t, d), F32),
        compiler_params=_compiler_params(("parallel", "arbitrary")),
        name="ffn",
    )(h, x, w_gu, w_gu, w_dn)


def _expert_kernel(expert_ref, valid_ref, src_ref, xs_hbm, wg_ref, wu_ref, wd_ref, o_ref, buf, sems):
    j = pl.program_id(0)
    rows = buf.shape[1]
    chunks = rows // MOE_CHUNK

    def fetches(step, slot):
        return [pltpu.make_async_copy(
            xs_hbm.at[src_ref[step * chunks + c]],
            buf.at[slot, pl.ds(c * MOE_CHUNK, MOE_CHUNK), :], sems.at[slot]) for c in range(chunks)]

    slot = lax.rem(j, 2)

    @pl.when(j == 0)
    def _():
        for copy in fetches(0, 0):
            copy.start()

    @pl.when(j + 1 < pl.num_programs(0))
    def _():
        for copy in fetches(j + 1, 1 - slot):
            copy.start()

    for copy in fetches(j, slot):
        copy.wait()

    @pl.when(valid_ref[j] != 0)
    def _():
        o_ref[...] = _swiglu_chunk(buf[slot], wg_ref[0], wu_ref[0], wd_ref[0]).astype(BF16)

    @pl.when(valid_ref[j] == 0)
    def _():
        o_ref[...] = jnp.zeros_like(o_ref)


def _experts(xs, tile_expert, tile_valid, chunk_src, w_gu, w_dn, n_rows):
    d = xs.shape[2]
    d_ff = w_dn.shape[1]
    te = MOE_EXPERT_TILE
    grid_spec = pltpu.PrefetchScalarGridSpec(
        num_scalar_prefetch=3,
        grid=(n_rows // te,),
        in_specs=[
            pl.BlockSpec(memory_space=pl.ANY),
            pl.BlockSpec((1, d, d_ff), lambda j, ex, va, sr: (ex[j], 0, 0)),
            pl.BlockSpec((1, d, d_ff), lambda j, ex, va, sr: (ex[j], 0, 1)),
            pl.BlockSpec((1, d_ff, d), lambda j, ex, va, sr: (ex[j], 0, 0)),
        ],
        out_specs=pl.BlockSpec((te, d), lambda j, ex, va, sr: (j, 0)),
        scratch_shapes=[pltpu.VMEM((2, te, d), BF16), pltpu.SemaphoreType.DMA((2,))],
    )
    return pl.pallas_call(
        _expert_kernel,
        grid_spec=grid_spec,
        out_shape=jax.ShapeDtypeStruct((n_rows, d), BF16),
        compiler_params=_compiler_params(("arbitrary",)),
        name="experts",
    )(tile_expert, tile_valid, chunk_src, xs, w_gu, w_gu, w_dn)


def _combine_kernel(goff_ref, count_ref, loff_ref, x_ref, route_ref, ys_hbm, o_ref, buf, sems):
    i = pl.program_id(0)
    slot = lax.rem(i, 2)

    def for_each_chunk(step, which, act):
        for e in range(N_EXPERTS):
            g = step * N_EXPERTS + e

            @pl.loop(0, count_ref[g])
            def _(c):
                dst = pl.multiple_of((loff_ref[g] + c) * MOE_CHUNK, MOE_CHUNK)
                act(pltpu.make_async_copy(ys_hbm.at[goff_ref[g] + c],
                                          buf.at[which, pl.ds(dst, MOE_CHUNK), :], sems.at[which]))

    @pl.when(i == 0)
    def _():
        buf[...] = jnp.zeros_like(buf)
        for_each_chunk(0, 0, lambda copy: copy.start())

    @pl.when(i + 1 < pl.num_programs(0))
    def _():
        for_each_chunk(i + 1, 1 - slot, lambda copy: copy.start())

    for_each_chunk(i, slot, lambda copy: copy.wait())

    route = route_ref[...]
    y = buf[slot]
    col = lax.broadcasted_iota(jnp.int32, (route.shape[0], buf.shape[1]), 1).astype(F32)
    out = x_ref[...]
    for k in range(TOP_K):
        pick = jnp.where(col == route[:, k:k + 1], 1.0, 0.0).astype(BF16)
        out = out + route[:, TOP_K + k:TOP_K + k + 1] * _dot(pick, y)
    o_ref[...] = out


def _combine(x, route, ys, goff, count, loff, tm):
    t, d = x.shape
    grid_spec = pltpu.PrefetchScalarGridSpec(
        num_scalar_prefetch=3,
        grid=(t // tm,),
        in_specs=[
            pl.BlockSpec((tm, d), lambda i, *_: (i, 0)),
            pl.BlockSpec((tm, LANES), lambda i, *_: (i, 0)),
            pl.BlockSpec(memory_space=pl.ANY),
        ],
        out_specs=pl.BlockSpec((tm, d), lambda i, *_: (i, 0)),
        scratch_shapes=[pltpu.VMEM((2, _local_rows(tm), d), BF16), pltpu.SemaphoreType.DMA((2,))],
    )
    return pl.pallas_call(
        _combine_kernel,
        grid_spec=grid_spec,
        out_shape=jax.ShapeDtypeStruct((t, d), F32),
        compiler_params=_compiler_params(("arbitrary",)),
        name="combine",
    )(goff, count, loff, x, route, ys)


def _moe(x1, xs_local, padded_len, route, w_gu, w_dn, tm):
    t, d = x1.shape
    n_tiles, local_rows, _ = xs_local.shape
    te = MOE_EXPERT_TILE
    n_rows = -(-(TOP_K * t + n_tiles * N_EXPERTS * (MOE_CHUNK - 1) + N_EXPERTS * (te - 1)) // te) * te
    length = padded_len[:, :, 0].astype(jnp.int32)
    count = length // MOE_CHUNK
    used = jnp.sum(length, axis=0)
    span = -(-used // te) * te
    base = jnp.cumsum(span) - span
    goff = (base[None, :] + jnp.cumsum(length, axis=0) - length) // MOE_CHUNK
    loff = (jnp.cumsum(length, axis=1) - length) // MOE_CHUNK
    seg_start = goff.T.reshape(-1)
    seg_count = count.T.reshape(-1)
    seg_src = (jnp.arange(n_tiles, dtype=jnp.int32)[:, None] * (local_rows // MOE_CHUNK) + loff).T.reshape(-1)
    chunk = jnp.arange(n_rows // MOE_CHUNK, dtype=jnp.int32)[:, None]
    within = chunk - seg_start[None, :]
    owner = (within >= 0) & (within < seg_count[None, :])
    chunk_src = jnp.sum(jnp.where(owner, seg_src[None, :] + within, 0), axis=1).astype(jnp.int32)
    tile_row = jnp.arange(n_rows // te, dtype=jnp.int32)[:, None] * te
    tile_expert = (jnp.sum(base[None, :] <= tile_row, axis=1) - 1).astype(jnp.int32)
    tile_valid = jnp.any((base[None, :] <= tile_row) & (tile_row < (base + used)[None, :]),
                         axis=1).astype(jnp.int32)
    ys = _experts(xs_local.reshape(-1, MOE_CHUNK, d), tile_expert, tile_valid, chunk_src, w_gu, w_dn, n_rows)
    return _combine(x1, route, ys.reshape(-1, MOE_CHUNK, d), goff.reshape(-1), count.reshape(-1),
                    loff.reshape(-1), tm)


def _pad_lanes(w):
    return jnp.pad(w, ((0, 0), (0, LANES - w.shape[1])))


def kernel(x, g_mix, w_in, b_f, b_gate, g_q, g_k, w_o_sb, w_o_fox, w_out, g_ffn, w_gu_dense,
           w_dn_dense, w_router, w_gu_exp, w_dn_exp):
    batch, seq, d = x.shape
    depth = w_in.shape[0]
    t = batch * seq
    tm_proj = min(512, t)
    tm_ffn = min(1024, t)
    n_qkv = 6 * W_BRANCH
    xt = x.reshape(t, d)
    for l in range(depth):
        w_l = w_in[l]
        w_proj = jnp.concatenate(
            [w_l[:, :n_qkv], _pad_lanes(w_l[:, n_qkv:n_qkv + N_HEADS])], axis=1).astype(BF16)
        w_gate = w_l[:, n_qkv + N_HEADS:].astype(BF16)
        row = lambda v: v.reshape(1, -1)
        gq = jnp.tile(g_q[l], N_HEADS).reshape(1, W_BRANCH)
        gk = jnp.tile(g_k[l], N_HEADS).reshape(1, W_BRANCH)
        qkv_sb, qkv_fx, f_pre = _proj(xt, row(g_mix[l]), w_proj, gq, gk, tm_proj)
        cum_f = _cumf(f_pre.reshape(batch, seq, LANES), _pad_lanes(row(b_f[l])))
        zmax = (QK_BOUND_MARGIN * 8.0 * jnp.max(jnp.abs(g_q[l])) * jnp.max(jnp.abs(g_k[l]))).reshape(1)
        y_sb, y_fx = _attention(qkv_sb.reshape(batch, seq, 3 * W_BRANCH),
                                qkv_fx.reshape(batch, seq, 3 * W_BRANCH), cum_f, zmax)
        moe_layer = l % 2 == 1
        router = None
        if moe_layer:
            wr = _pad_lanes(w_router[l // 2])
            router = _split_bf16(wr, 2)
        outs = _merge(xt, y_sb.reshape(t, W_BRANCH), y_fx.reshape(t, W_BRANCH), row(g_mix[l]),
                      w_gate, row(b_gate[l]), w_o_sb[l].astype(BF16), w_o_fox[l].astype(BF16),
                      w_out[l].astype(BF16), row(g_ffn[l]), router, tm_proj)
        if moe_layer:
            x1, xs_local, padded_len, route = outs
            xt = _moe(x1, xs_local, padded_len, route, w_gu_exp[l // 2].astype(BF16),
                      w_dn_exp[l // 2].astype(BF16), tm_proj)
        else:
            x1, h2 = outs
            xt = _ffn(h2, x1, w_gu_dense[l // 2].astype(BF16), w_dn_dense[l // 2].astype(BF16),
                      tm_proj, w_dn_dense.shape[1] // 2)
    return xt.reshape(batch, seq, d)
```

```python
import functools

import jax
import jax.numpy as jnp
from jax import lax
from jax.experimental import pallas as pl
from jax.experimental.pallas import tpu as pltpu

F32 = jnp.float32
BF16 = jnp.bfloat16

HEAD_DIM = 64
N_HEADS = 8
W_BRANCH = N_HEADS * HEAD_DIM
N_EXPERTS = 8
TOP_K = 2
MOE_CHUNK = 16
MOE_EXPERT_TILE = 512
RMS_EPS = 1e-6
QK_SCALE = 1.0 / 8.0
LANES = 128
HEADS_PER_BLOCK = LANES // HEAD_DIM
V7X_VMEM_LIMIT_BYTES = 56 * 1024 * 1024

ATT_BLOCK = 256
ATT_LANE_BLOCKS = 4
F32_EXP_UNDERFLOW = 106.0
FOX_CONSTANT_SHIFT_MAX = 30.0
QK_BOUND_MARGIN = 1.05
CUM_BLOCK = 512


def _split_bf16(x, parts):
    out = []
    for _ in range(parts - 1):
        hi = x.astype(BF16)
        out.append(hi)
        x = x - hi.astype(F32)
    out.append(x.astype(BF16))
    return out


def _dot(a, b):
    return jnp.dot(a, b, preferred_element_type=F32)


def _dot_nt(a, b):
    return lax.dot_general(a, b, (((1,), (1,)), ((), ())), preferred_element_type=F32)


def _rmsnorm(x, g):
    ms = jnp.mean(x * x, axis=-1, keepdims=True)
    return x * lax.rsqrt(ms + RMS_EPS) * g


def _sigmoid(x):
    return 1.0 / (1.0 + jnp.exp(-x))


def _compiler_params(semantics):
    return pltpu.CompilerParams(dimension_semantics=semantics,
                                vmem_limit_bytes=V7X_VMEM_LIMIT_BYTES)


def _proj_kernel(x_ref, g_ref, w_ref, gq_ref, gk_ref, sb_ref, fx_ref, f_ref):
    h = _rmsnorm(x_ref[...], g_ref[...]).astype(BF16)
    wb = W_BRANCH
    r = lax.broadcasted_iota(jnp.int32, (wb, wb), 0) // HEAD_DIM
    c = lax.broadcasted_iota(jnp.int32, (wb, wb), 1) // HEAD_DIM
    head_mean = jnp.where(r == c, 1.0 / HEAD_DIM, 0.0).astype(BF16)
    for i in range(3):
        acc = _dot(h, w_ref[:, i * wb:(i + 1) * wb])
        if i == 0:
            acc = acc * QK_SCALE
        sb_ref[:, i * wb:(i + 1) * wb] = acc.astype(BF16)
    for i, gain_ref in enumerate((gq_ref, gk_ref, None)):
        acc = _dot(h, w_ref[:, (3 + i) * wb:(4 + i) * wb])
        if gain_ref is not None:
            ms = _dot((acc * acc).astype(BF16), head_mean)
            acc = acc * lax.rsqrt(ms + RMS_EPS) * gain_ref[...]
        if i == 0:
            acc = acc * QK_SCALE
        fx_ref[:, i * wb:(i + 1) * wb] = acc.astype(BF16)
    f_ref[...] = _dot(h, w_ref[:, 6 * wb:6 * wb + LANES])


def _proj(x, g, w, gq, gk, tm):
    t, d = x.shape
    n = w.shape[1]
    full = lambda i: (0, 0)
    return pl.pallas_call(
        _proj_kernel,
        grid=(t // tm,),
        in_specs=[
            pl.BlockSpec((tm, d), lambda i: (i, 0)),
            pl.BlockSpec((1, d), full),
            pl.BlockSpec((d, n), full),
            pl.BlockSpec((1, W_BRANCH), full),
            pl.BlockSpec((1, W_BRANCH), full),
        ],
        out_specs=[
            pl.BlockSpec((tm, 3 * W_BRANCH), lambda i: (i, 0)),
            pl.BlockSpec((tm, 3 * W_BRANCH), lambda i: (i, 0)),
            pl.BlockSpec((tm, LANES), lambda i: (i, 0)),
        ],
        out_shape=[
            jax.ShapeDtypeStruct((t, 3 * W_BRANCH), BF16),
            jax.ShapeDtypeStruct((t, 3 * W_BRANCH), BF16),
            jax.ShapeDtypeStruct((t, LANES), F32),
        ],
        compiler_params=_compiler_params(("parallel",)),
        name="proj",
    )(x, g, w, gq, gk)


def _cumf_kernel(f_ref, b_ref, o_ref):
    seq = f_ref.shape[1]
    cb = min(CUM_BLOCK, seq)
    r = lax.broadcasted_iota(jnp.int32, (cb, cb), 0)
    c = lax.broadcasted_iota(jnp.int32, (cb, cb), 1)
    prefix = jnp.where(r <= c, 1.0, 0.0).astype(BF16)
    carry = jnp.zeros((N_HEADS, 1), F32)
    for i in range(seq // cb):
        v = f_ref[0, i * cb:(i + 1) * cb, :] + b_ref[...]
        log_f = jnp.minimum(v, 0.0) - jnp.log1p(jnp.exp(-jnp.abs(v)))
        log_f = log_f.T[:N_HEADS, :]
        cum = carry
        for part in _split_bf16(log_f, 3):
            cum = cum + _dot(part, prefix)
        o_ref[0, :, i * cb:(i + 1) * cb] = cum
        carry = cum[:, cb - 1:cb]


def _cumf(f_pre, b_f):
    b, s, _ = f_pre.shape
    return pl.pallas_call(
        _cumf_kernel,
        grid=(b,),
        in_specs=[
            pl.BlockSpec((1, s, LANES), lambda i: (i, 0, 0)),
            pl.BlockSpec((1, LANES), lambda i: (0, 0)),
        ],
        out_specs=pl.BlockSpec((1, N_HEADS, s), lambda i: (i, 0, 0)),
        out_shape=jax.ShapeDtypeStruct((b, N_HEADS, s), F32),
        compiler_params=_compiler_params(("parallel",)),
        name="cumf",
    )(f_pre, b_f)


def _head_queries(q_ref, half):
    q = q_ref[0, :, half * LANES:(half + 1) * LANES].astype(F32)
    lane = lax.broadcasted_iota(jnp.int32, (1, LANES), 1)
    return [jnp.where((lane >= hh * HEAD_DIM) & (lane < (hh + 1) * HEAD_DIM), q, 0.0).astype(BF16)
            for hh in range(HEADS_PER_BLOCK)]


def _run_chains(q_ref, k_ref, v_ref, o_ref, stages, finish, stop):
    bk = ATT_BLOCK
    halves = range(q_ref.shape[2] // LANES)
    qi = pl.program_id(2)
    queries = [q for half in halves for q in _head_queries(q_ref, half)]

    def advance(blocks, masked, states):
        args = []
        for kb in blocks:
            start = pl.multiple_of(kb * bk, bk)
            kv = [(k_ref[0, pl.ds(start, bk), half * LANES:(half + 1) * LANES],
                   v_ref[0, pl.ds(start, bk), half * LANES:(half + 1) * LANES]) for half in halves]
            args += [(head, q, *kv[head // HEADS_PER_BLOCK], start, masked)
                     for head, q in enumerate(queries)]
        tmps = [None] * len(args)
        for stage in stages[:-1]:
            tmps = [stage(*a, tmp) for a, tmp in zip(args, tmps)]
        states = list(states)
        for a, tmp in zip(args, tmps):
            states[a[0]] = stages[-1](*a, tmp, states[a[0]])
        return tuple(states)

    states = advance([qi], True, [None] * len(queries))
    if callable(stop):
        def body(carry):
            i, _, states = carry
            kb = qi - 1 - i
            states = advance([kb], False, states)
            return i + 1, stop(states, jnp.maximum(kb - 1, 0)).astype(jnp.int32), states

        _, _, states = lax.while_loop(lambda c: jnp.logical_and(c[0] < qi, c[1] == 0), body,
                                      (jnp.int32(0), jnp.int32(0), states))
    else:
        n_blocks = stop
        states = lax.fori_loop(
            0, n_blocks // 2,
            lambda i, s: advance([qi - 1 - 2 * i, qi - 2 - 2 * i], False, s), states)
        states = lax.cond(n_blocks % 2 == 1,
                          lambda s: advance([jnp.maximum(qi - n_blocks, 0)], False, s),
                          lambda s: s, states)
    lane = lax.broadcasted_iota(jnp.int32, (1, LANES), 1)
    for half in halves:
        outs = [finish(head, states[head])
                for head in range(half * HEADS_PER_BLOCK, (half + 1) * HEADS_PER_BLOCK)]
        o_ref[0, :, half * LANES:(half + 1) * LANES] = jnp.where(
            lane < HEAD_DIM, outs[0], outs[1]).astype(BF16)


def _sb_kernel(q_ref, k_ref, v_ref, o_ref):
    bk = ATT_BLOCK
    row = lax.broadcasted_iota(jnp.int32, (bk, bk), 0)
    col = lax.broadcasted_iota(jnp.int32, (bk, bk), 1)
    suffix = jnp.where(row >= col, 1.0, 0.0).astype(BF16)
    below_diag = col < row

    def scores(head, q, k, v, start, masked, tmp):
        z = _dot_nt(q, k)
        neg_abs = lax.bitcast_convert_type(
            lax.bitcast_convert_type(z, jnp.uint32) | jnp.uint32(0x80000000), F32)
        sp = jnp.maximum(z, 0.0) + jnp.log(1.0 + jnp.exp(neg_abs))
        if masked:
            sp = jnp.where(below_diag, sp, 0.0)
        return z, sp.astype(BF16)

    def suffix_sums(head, q, k, v, start, masked, tmp):
        z, sp_bf16 = tmp
        return z, _dot(sp_bf16, suffix)

    def weights(head, q, k, v, start, masked, tmp, state):
        z, incl = tmp
        arg = z - incl
        if state is not None:
            arg = arg - state[0]
        w = jnp.exp(arg)
        if masked:
            w = jnp.where(below_diag, w, 0.0)
        pv = _dot(w.astype(BF16), v)
        if state is None:
            return incl[:, 0:1], pv
        return state[0] + incl[:, 0:1], state[1] + pv

    def stop(states, next_block):
        carry = functools.reduce(jnp.minimum, [state[0] for state in states])
        return jnp.min(carry) > F32_EXP_UNDERFLOW

    _run_chains(q_ref, k_ref, v_ref, o_ref, (scores, suffix_sums, weights),
                lambda head, state: state[1], stop)


def _fox_kernel(zmax_ref, fend_ref, q_ref, k_ref, v_ref, f_ref, fcol_ref, o_ref):
    bk = ATT_BLOCK
    n_blocks = k_ref.shape[1] // bk
    n_heads = f_ref.shape[2]
    qi = pl.program_id(2)
    row = lax.broadcasted_iota(jnp.int32, (bk, bk), 0)
    col = lax.broadcasted_iota(jnp.int32, (bk, bk), 1)
    causal = col <= row
    lane = lax.broadcasted_iota(jnp.int32, (1, LANES), 1)
    table = (pl.program_id(0) * (W_BRANCH // HEAD_DIM) + pl.program_id(1) * n_heads) * n_blocks

    def ones_lanes_cache():
        values = {}

        def values_with_ones(hh, v):
            if (hh, id(v)) not in values:
                own = (lane >= hh * HEAD_DIM) & (lane < (hh + 1) * HEAD_DIM)
                values[(hh, id(v))] = (v, jnp.where(own, v.astype(F32), 1.0).astype(BF16))
            return values[(hh, id(v))][1]

        return values_with_ones

    def normalised(head, acc):
        ones_lane = (1 - head % HEADS_PER_BLOCK) * HEAD_DIM
        return acc / acc[:, ones_lane:ones_lane + 1]

    @pl.when(zmax_ref[0] <= FOX_CONSTANT_SHIFT_MAX)
    def _():
        values_with_ones = ones_lanes_cache()
        shift = [fcol_ref[0, :, head:head + 1] - zmax_ref[0] for head in range(n_heads)]

        def weights(head, q, k, v, start, masked, tmp):
            s = _dot_nt(q, k) - f_ref[0, 0, head:head + 1, pl.ds(start, bk)] + shift[head]
            if masked:
                s = jnp.where(causal, s, -jnp.inf)
            return jnp.exp(s).astype(BF16)

        def accumulate(head, q, k, v, start, masked, tmp, state):
            pv = _dot(tmp, values_with_ones(head % HEADS_PER_BLOCK, v))
            return pv if state is None else state + pv

        def contributes(block):
            last = jnp.maximum(qi - 1, 0)
            gaps = [fend_ref[table + head * n_blocks + block] - fend_ref[table + head * n_blocks + last]
                    for head in range(n_heads)]
            return functools.reduce(jnp.maximum, gaps) >= -F32_EXP_UNDERFLOW

        needed = lax.while_loop(
            lambda n: jnp.logical_and(n < qi, contributes(jnp.maximum(qi - 1 - n, 0))),
            lambda n: n + 1, jnp.int32(0))
        _run_chains(q_ref, k_ref, v_ref, o_ref, (weights, accumulate), normalised, needed)

    @pl.when(zmax_ref[0] > FOX_CONSTANT_SHIFT_MAX)
    def _():
        values_with_ones = ones_lanes_cache()

        def logits(head, q, k, v, start, masked, tmp):
            s = _dot_nt(q, k) - f_ref[0, 0, head:head + 1, pl.ds(start, bk)]
            if masked:
                s = jnp.where(causal, s, -jnp.inf)
            return s, jnp.max(s, axis=1, keepdims=True)

        def accumulate(head, q, k, v, start, masked, tmp, state):
            s, m_blk = tmp
            v_ones = values_with_ones(head % HEADS_PER_BLOCK, v)
            if state is None:
                return m_blk, _dot(jnp.exp(s - m_blk).astype(BF16), v_ones)
            m, acc = state
            m_new = jnp.maximum(m, m_blk)
            pv = _dot(jnp.exp(s - m_new).astype(BF16), v_ones)
            return m_new, jnp.exp(m - m_new) * acc + pv

        def stop(states, next_block):
            slack = functools.reduce(jnp.minimum, [
                state[0] - fend_ref[table + head * n_blocks + next_block]
                for head, state in enumerate(states)])
            return jnp.min(slack) > zmax_ref[0] + F32_EXP_UNDERFLOW

        _run_chains(q_ref, k_ref, v_ref, o_ref, (logits, accumulate),
                    lambda head, state: normalised(head, state[1]), stop)


def _attention(kernel, qkv, cum_f=None, zmax=None):
    b, s, _ = qkv.shape
    bq = min(ATT_BLOCK, s)
    width = ATT_LANE_BLOCKS * LANES
    n_groups = W_BRANCH // width
    in_specs = [
        pl.BlockSpec((1, bq, width), lambda bi, g, qi: (bi, qi, g)),
        pl.BlockSpec((1, s, width), lambda bi, g, qi: (bi, 0, n_groups + g)),
        pl.BlockSpec((1, s, width), lambda bi, g, qi: (bi, 0, 2 * n_groups + g)),
    ]
    args = [qkv, qkv, qkv]
    if cum_f is not None:
        heads = ATT_LANE_BLOCKS * HEADS_PER_BLOCK
        smem = pl.BlockSpec(memory_space=pltpu.SMEM)
        in_specs = [smem, smem] + in_specs + [
            pl.BlockSpec((1, 1, heads, s), lambda bi, g, qi: (bi, g, 0, 0)),
            pl.BlockSpec((1, bq, heads), lambda bi, g, qi: (bi * n_groups + g, qi, 0))]
        f_end = -cum_f[:, :, bq - 1::bq]
        by_head = cum_f.reshape(b, n_groups, heads, s)
        args = [zmax, f_end.reshape(-1)] + args + [
            by_head, jnp.swapaxes(by_head, 2, 3).reshape(b * n_groups, s, heads)]
    return pl.pallas_call(
        kernel,
        grid=(b, n_groups, s // bq),
        in_specs=in_specs,
        out_specs=pl.BlockSpec((1, bq, width), lambda bi, g, qi: (bi, qi, g)),
        out_shape=jax.ShapeDtypeStruct((b, s, W_BRANCH), BF16),
        compiler_params=_compiler_params(("parallel", "parallel", "arbitrary")),
        name=kernel.__name__.strip("_"),
    )(*args)


def _local_rows(tm):
    return TOP_K * tm + N_EXPERTS * MOE_CHUNK


def _route_and_sort(logits, h2, xs_ref, len_ref, route_ref):
    tm = logits.shape[0]
    lt = logits.T[:N_EXPERTS, :]
    expert = lax.broadcasted_iota(jnp.int32, lt.shape, 0).astype(F32)
    m1 = jnp.max(lt, axis=0, keepdims=True)
    i1 = jnp.min(jnp.where(lt == m1, expert, float(N_EXPERTS)), axis=0, keepdims=True)
    lt2 = jnp.where(expert == i1, -jnp.inf, lt)
    m2 = jnp.max(lt2, axis=0, keepdims=True)
    i2 = jnp.min(jnp.where(lt2 == m2, expert, float(N_EXPERTS)), axis=0, keepdims=True)
    e2 = jnp.exp(m2 - m1)
    w1 = 1.0 / (1.0 + e2)
    w2 = e2 * w1
    chosen = jnp.where((expert == i1) | (expert == i2), 1.0, 0.0)
    yield
    before = (lax.broadcasted_iota(jnp.int32, (tm, tm), 0)
              < lax.broadcasted_iota(jnp.int32, (tm, tm), 1))
    rank = _dot(chosen.astype(BF16), jnp.where(before, 1.0, 0.0).astype(BF16))
    count = jnp.sum(chosen, axis=1, keepdims=True)
    padded = jnp.floor((count + (MOE_CHUNK - 1)) * (1.0 / MOE_CHUNK)) * MOE_CHUNK
    r1 = jnp.sum(jnp.where(expert == i1, rank, 0.0), axis=0, keepdims=True)
    r2 = jnp.sum(jnp.where(expert == i2, rank, 0.0), axis=0, keepdims=True)
    start = jnp.zeros((1, 1), F32)
    for e in range(N_EXPERTS):
        r1 = r1 + jnp.where(i1 == e, start, 0.0)
        r2 = r2 + jnp.where(i2 == e, start, 0.0)
        start = start + padded[e:e + 1, :]
    rows = _local_rows(tm)
    row = lax.broadcasted_iota(jnp.int32, (rows, tm), 0).astype(F32)
    place = jnp.where((row == r1) | (row == r2), 1.0, 0.0).astype(BF16)
    yield
    xs_ref[0] = _dot(place, h2).astype(BF16)
    len_ref[0] = jnp.broadcast_to(padded, (N_EXPERTS, LANES))
    field = lax.broadcasted_iota(jnp.int32, (LANES, tm), 0)
    fields = jnp.where(field == 0, r1, jnp.where(field == 1, r2, jnp.where(
        field == 2, w1, jnp.where(field == 3, w2, 0.0))))
    route_ref[...] = fields.T


def _merge_kernel(x_ref, ysb_ref, yfx_ref, gmix_ref, wgate_ref, bgate_ref, wosb_ref, wofx_ref,
                  wout_ref, gffn_ref, *rest, with_router):
    if with_router:
        wr_hi_ref, wr_lo_ref, x1_ref, xs_ref, len_ref, route_ref, h2_keep, logits_keep = rest
        step = pl.program_id(0)
        keep = lax.rem(step, 2)

        @pl.when(step == 0)
        def _():
            h2_keep[...] = jnp.zeros_like(h2_keep)
            logits_keep[...] = jnp.zeros_like(logits_keep)

        routing = _route_and_sort(logits_keep[1 - keep], h2_keep[1 - keep], xs_ref, len_ref, route_ref)
        next(routing)
    else:
        x1_ref, h2_ref = rest
    d = x_ref.shape[1]
    x = x_ref[...]
    h = _rmsnorm(x, gmix_ref[...]).astype(BF16)
    merged = None
    for i, (y_ref, wo_ref) in enumerate(((ysb_ref, wosb_ref), (yfx_ref, wofx_ref))):
        gate = _sigmoid(_dot(h, wgate_ref[:, i * d:(i + 1) * d]) + bgate_ref[:, i * d:(i + 1) * d])
        branch = gate * _dot(y_ref[...], wo_ref[...])
        merged = branch if merged is None else merged + branch
    if with_router:
        next(routing)
    x1 = x + _dot(merged.astype(BF16), wout_ref[...])
    x1_ref[...] = x1
    h2 = _rmsnorm(x1, gffn_ref[...])
    if with_router:
        h_hi, h_lo = _split_bf16(h2, 2)
        logits_keep[keep] = (_dot(h_hi, wr_hi_ref[...]) + _dot(h_lo, wr_hi_ref[...])
                             + _dot(h_hi, wr_lo_ref[...]))
        h2_keep[keep] = h_hi
        next(routing, None)
    else:
        h2_ref[...] = h2.astype(BF16)


def _merge(x, y_sb, y_fx, g_mix, w_gate, b_gate, w_o_sb, w_o_fox, w_out, g_ffn, w_router, tm):
    t, d = x.shape
    with_router = w_router is not None
    n_tiles = t // tm
    full = lambda i: (0, 0)
    tile = lambda i: (jnp.minimum(i, n_tiles - 1), 0)
    routed = lambda i: (jnp.maximum(i - 1, 0), 0)
    scratch = []
    in_specs = [
        pl.BlockSpec((tm, d), tile),
        pl.BlockSpec((tm, W_BRANCH), tile),
        pl.BlockSpec((tm, W_BRANCH), tile),
        pl.BlockSpec((1, d), full),
        pl.BlockSpec((d, 2 * d), full),
        pl.BlockSpec((1, 2 * d), full),
        pl.BlockSpec((W_BRANCH, d), full),
        pl.BlockSpec((W_BRANCH, d), full),
        pl.BlockSpec((d, d), full),
        pl.BlockSpec((1, d), full),
    ]
    args = [x, y_sb, y_fx, g_mix, w_gate, b_gate, w_o_sb, w_o_fox, w_out, g_ffn]
    out_specs = [pl.BlockSpec((tm, d), tile)]
    out_shape = [jax.ShapeDtypeStruct((t, d), F32)]
    if with_router:
        rows = _local_rows(tm)
        in_specs += [pl.BlockSpec((d, LANES), full)] * 2
        args += list(w_router)
        out_specs += [pl.BlockSpec((1, rows, d), lambda i: routed(i) + (0,)),
                      pl.BlockSpec((1, N_EXPERTS, LANES), lambda i: routed(i) + (0,)),
                      pl.BlockSpec((tm, LANES), routed)]
        out_shape += [jax.ShapeDtypeStruct((n_tiles, rows, d), BF16),
                      jax.ShapeDtypeStruct((n_tiles, N_EXPERTS, LANES), F32),
                      jax.ShapeDtypeStruct((t, LANES), F32)]
        scratch = [pltpu.VMEM((2, tm, d), BF16), pltpu.VMEM((2, tm, LANES), F32)]
    else:
        out_specs.append(pl.BlockSpec((tm, d), tile))
        out_shape.append(jax.ShapeDtypeStruct((t, d), BF16))
    return pl.pallas_call(
        functools.partial(_merge_kernel, with_router=with_router),
        grid=(n_tiles + (1 if with_router else 0),),
        in_specs=in_specs,
        out_specs=out_specs,
        out_shape=out_shape,
        scratch_shapes=scratch,
        compiler_params=_compiler_params(("arbitrary",) if with_router else ("parallel",)),
        name="merge_router" if with_router else "merge",
    )(*args)


def _swiglu_chunk(h, wg, wu, wd):
    g = _dot(h, wg)
    u = _dot(h, wu)
    return _dot((g * _sigmoid(g) * u).astype(BF16), wd)


def _ffn_kernel(h_ref, x_ref, wg_ref, wu_ref, wd_ref, o_ref):
    @pl.when(pl.program_id(1) == 0)
    def _():
        o_ref[...] = x_ref[...]

    o_ref[...] += _swiglu_chunk(h_ref[...], wg_ref[...], wu_ref[...], wd_ref[...])


def _ffn(h, x, w_gu, w_dn, tm, tf):
    t, d = x.shape
    d_ff = w_dn.shape[0]
    nf = d_ff // tf
    return pl.pallas_call(
        _ffn_kernel,
        grid=(t // tm, nf),
        in_specs=[
            pl.BlockSpec((tm, d), lambda i, j: (i, 0)),
            pl.BlockSpec((tm, d), lambda i, j: (i, 0)),
            pl.BlockSpec((d, tf), lambda i, j: (0, j)),
            pl.BlockSpec((d, tf), lambda i, j: (0, nf + j)),
            pl.BlockSpec((tf, d), lambda i, j: (j, 0)),
        ],
        out_specs=pl.BlockSpec((tm, d), lambda i, j: (i, 0)),
        out_shape=jax.ShapeDtypeStruct((t, d), F32),
        compiler_params=_compiler_params(("parallel", "arbitrary")),
        name="ffn",
    )(h, x, w_gu, w_gu, w_dn)


def _expert_kernel(expert_ref, valid_ref, src_ref, xs_hbm, wg_ref, wu_ref, wd_ref, o_ref, buf, sems):
    j = pl.program_id(0)
    rows = buf.shape[1]
    chunks = rows // MOE_CHUNK

    def fetches(step, slot):
        return [pltpu.make_async_copy(
            xs_hbm.at[src_ref[step * chunks + c]],
            buf.at[slot, pl.ds(c * MOE_CHUNK, MOE_CHUNK), :], sems.at[slot]) for c in range(chunks)]

    slot = lax.rem(j, 2)

    @pl.when(j == 0)
    def _():
        for copy in fetches(0, 0):
            copy.start()

    @pl.when(j + 1 < pl.num_programs(0))
    def _():
        for copy in fetches(j + 1, 1 - slot):
            copy.start()

    for copy in fetches(j, slot):
        copy.wait()

    @pl.when(valid_ref[j] != 0)
    def _():
        o_ref[...] = _swiglu_chunk(buf[slot], wg_ref[0], wu_ref[0], wd_ref[0]).astype(BF16)

    @pl.when(valid_ref[j] == 0)
    def _():
        o_ref[...] = jnp.zeros_like(o_ref)


def _experts(xs, tile_expert, tile_valid, chunk_src, w_gu, w_dn, n_rows):
    d = xs.shape[2]
    d_ff = w_dn.shape[1]
    te = MOE_EXPERT_TILE
    grid_spec = pltpu.PrefetchScalarGridSpec(
        num_scalar_prefetch=3,
        grid=(n_rows // te,),
        in_specs=[
            pl.BlockSpec(memory_space=pl.ANY),
            pl.BlockSpec((1, d, d_ff), lambda j, ex, va, sr: (ex[j], 0, 0)),
            pl.BlockSpec((1, d, d_ff), lambda j, ex, va, sr: (ex[j], 0, 1)),
            pl.BlockSpec((1, d_ff, d), lambda j, ex, va, sr: (ex[j], 0, 0)),
        ],
        out_specs=pl.BlockSpec((te, d), lambda j, ex, va, sr: (j, 0)),
        scratch_shapes=[pltpu.VMEM((2, te, d), BF16), pltpu.SemaphoreType.DMA((2,))],
    )
    return pl.pallas_call(
        _expert_kernel,
        grid_spec=grid_spec,
        out_shape=jax.ShapeDtypeStruct((n_rows, d), BF16),
        compiler_params=_compiler_params(("arbitrary",)),
        name="experts",
    )(tile_expert, tile_valid, chunk_src, xs, w_gu, w_gu, w_dn)


def _combine_kernel(goff_ref, count_ref, loff_ref, x_ref, route_ref, ys_hbm, o_ref, buf, sems):
    i = pl.program_id(0)
    slot = lax.rem(i, 2)

    def for_each_chunk(step, which, act):
        for e in range(N_EXPERTS):
            g = step * N_EXPERTS + e

            @pl.loop(0, count_ref[g])
            def _(c):
                dst = pl.multiple_of((loff_ref[g] + c) * MOE_CHUNK, MOE_CHUNK)
                act(pltpu.make_async_copy(ys_hbm.at[goff_ref[g] + c],
                                          buf.at[which, pl.ds(dst, MOE_CHUNK), :], sems.at[which]))

    @pl.when(i == 0)
    def _():
        buf[...] = jnp.zeros_like(buf)
        for_each_chunk(0, 0, lambda copy: copy.start())

    @pl.when(i + 1 < pl.num_programs(0))
    def _():
        for_each_chunk(i + 1, 1 - slot, lambda copy: copy.start())

    for_each_chunk(i, slot, lambda copy: copy.wait())

    route = route_ref[...]
    y = buf[slot]
    col = lax.broadcasted_iota(jnp.int32, (route.shape[0], buf.shape[1]), 1).astype(F32)
    out = x_ref[...]
    for k in range(TOP_K):
        pick = jnp.where(col == route[:, k:k + 1], 1.0, 0.0).astype(BF16)
        out = out + route[:, TOP_K + k:TOP_K + k + 1] * _dot(pick, y)
    o_ref[...] = out


def _combine(x, route, ys, goff, count, loff, tm):
    t, d = x.shape
    grid_spec = pltpu.PrefetchScalarGridSpec(
        num_scalar_prefetch=3,
        grid=(t // tm,),
        in_specs=[
            pl.BlockSpec((tm, d), lambda i, *_: (i, 0)),
            pl.BlockSpec((tm, LANES), lambda i, *_: (i, 0)),
            pl.BlockSpec(memory_space=pl.ANY),
        ],
        out_specs=pl.BlockSpec((tm, d), lambda i, *_: (i, 0)),
        scratch_shapes=[pltpu.VMEM((2, _local_rows(tm), d), BF16), pltpu.SemaphoreType.DMA((2,))],
    )
    return pl.pallas_call(
        _combine_kernel,
        grid_spec=grid_spec,
        out_shape=jax.ShapeDtypeStruct((t, d), F32),
        compiler_params=_compiler_params(("arbitrary",)),
        name="combine",
    )(goff, count, loff, x, route, ys)


def _moe(x1, xs_local, padded_len, route, w_gu, w_dn, tm):
    t, d = x1.shape
    n_tiles, local_rows, _ = xs_local.shape
    te = MOE_EXPERT_TILE
    n_rows = -(-(TOP_K * t + n_tiles * N_EXPERTS * (MOE_CHUNK - 1) + N_EXPERTS * (te - 1)) // te) * te
    length = padded_len[:, :, 0].astype(jnp.int32)
    count = length // MOE_CHUNK
    used = jnp.sum(length, axis=0)
    span = -(-used // te) * te
    base = jnp.cumsum(span) - span
    goff = (base[None, :] + jnp.cumsum(length, axis=0) - length) // MOE_CHUNK
    loff = (jnp.cumsum(length, axis=1) - length) // MOE_CHUNK
    seg_start = goff.T.reshape(-1)
    seg_count = count.T.reshape(-1)
    seg_src = (jnp.arange(n_tiles, dtype=jnp.int32)[:, None] * (local_rows // MOE_CHUNK) + loff).T.reshape(-1)
    chunk = jnp.arange(n_rows // MOE_CHUNK, dtype=jnp.int32)[:, None]
    within = chunk - seg_start[None, :]
    owner = (within >= 0) & (within < seg_count[None, :])
    chunk_src = jnp.sum(jnp.where(owner, seg_src[None, :] + within, 0), axis=1).astype(jnp.int32)
    tile_row = jnp.arange(n_rows // te, dtype=jnp.int32)[:, None] * te
    tile_expert = (jnp.sum(base[None, :] <= tile_row, axis=1) - 1).astype(jnp.int32)
    tile_valid = jnp.any((base[None, :] <= tile_row) & (tile_row < (base + used)[None, :]),
                         axis=1).astype(jnp.int32)
    ys = _experts(xs_local.reshape(-1, MOE_CHUNK, d), tile_expert, tile_valid, chunk_src, w_gu, w_dn, n_rows)
    return _combine(x1, route, ys.reshape(-1, MOE_CHUNK, d), goff.reshape(-1), count.reshape(-1),
                    loff.reshape(-1), tm)


def _pad_lanes(w):
    return jnp.pad(w, ((0, 0), (0, LANES - w.shape[1])))


def kernel(x, g_mix, w_in, b_f, b_gate, g_q, g_k, w_o_sb, w_o_fox, w_out, g_ffn, w_gu_dense,
           w_dn_dense, w_router, w_gu_exp, w_dn_exp):
    batch, seq, d = x.shape
    depth = w_in.shape[0]
    t = batch * seq
    tm_proj = min(512, t)
    tm_ffn = min(1024, t)
    n_qkv = 6 * W_BRANCH
    xt = x.reshape(t, d)
    for l in range(depth):
        w_l = w_in[l]
        w_proj = jnp.concatenate(
            [w_l[:, :n_qkv], _pad_lanes(w_l[:, n_qkv:n_qkv + N_HEADS])], axis=1).astype(BF16)
        w_gate = w_l[:, n_qkv + N_HEADS:].astype(BF16)
        row = lambda v: v.reshape(1, -1)
        gq = jnp.tile(g_q[l], N_HEADS).reshape(1, W_BRANCH)
        gk = jnp.tile(g_k[l], N_HEADS).reshape(1, W_BRANCH)
        qkv_sb, qkv_fx, f_pre = _proj(xt, row(g_mix[l]), w_proj, gq, gk, tm_proj)
        cum_f = _cumf(f_pre.reshape(batch, seq, LANES), _pad_lanes(row(b_f[l])))
        y_sb = _attention(_sb_kernel, qkv_sb.reshape(batch, seq, 3 * W_BRANCH))
        zmax = (QK_BOUND_MARGIN * 8.0 * jnp.max(jnp.abs(g_q[l])) * jnp.max(jnp.abs(g_k[l]))).reshape(1)
        y_fx = _attention(_fox_kernel, qkv_fx.reshape(batch, seq, 3 * W_BRANCH), cum_f, zmax)
        moe_layer = l % 2 == 1
        router = None
        if moe_layer:
            wr = _pad_lanes(w_router[l // 2])
            router = _split_bf16(wr, 2)
        outs = _merge(xt, y_sb.reshape(t, W_BRANCH), y_fx.reshape(t, W_BRANCH), row(g_mix[l]),
                      w_gate, row(b_gate[l]), w_o_sb[l].astype(BF16), w_o_fox[l].astype(BF16),
                      w_out[l].astype(BF16), row(g_ffn[l]), router, tm_proj)
        if moe_layer:
            x1, xs_local, padded_len, route = outs
            xt = _moe(x1, xs_local, padded_len, route, w_gu_exp[l // 2].astype(BF16),
                      w_dn_exp[l // 2].astype(BF16), tm_proj)
        else:
            x1, h2 = outs
            xt = _ffn(h2, x1, w_gu_dense[l // 2].astype(BF16), w_dn_dense[l // 2].astype(BF16),
                      tm_proj, w_dn_dense.shape[1] // 2)
    return xt.reshape(batch, seq, d)
```

```python
import functools

import jax
import jax.numpy as jnp
from jax import lax
from jax.experimental import pallas as pl
from jax.experimental.pallas import tpu as pltpu

F32 = jnp.float32
BF16 = jnp.bfloat16

HEAD_DIM = 64
N_HEADS = 8
W_BRANCH = N_HEADS * HEAD_DIM
N_EXPERTS = 8
TOP_K = 2
TOKEN_TILE = 512
MOE_CHUNK = 16
MOE_EXPERT_TILE = 512
RMS_EPS = 1e-6
QK_SCALE = 1.0 / 8.0
LANES = 128
HEADS_PER_BLOCK = LANES // HEAD_DIM
V7X_VMEM_LIMIT_BYTES = 56 * 1024 * 1024

ATT_BLOCK = 256
ATT_LANE_BLOCKS = 4
F32_EXP_UNDERFLOW = 106.0
FOX_CONSTANT_SHIFT_MAX = 30.0
QK_BOUND_MARGIN = 1.05
CUM_BLOCK = 512


def _split_bf16(x, parts):
    out = []
    for _ in range(parts - 1):
        hi = x.astype(BF16)
        out.append(hi)
        x = x - hi.astype(F32)
    out.append(x.astype(BF16))
    return out


def _dot(a, b):
    return jnp.dot(a, b, preferred_element_type=F32)


def _dot_nt(a, b):
    return lax.dot_general(a, b, (((1,), (1,)), ((), ())), preferred_element_type=F32)


def _rmsnorm(x, g):
    ms = jnp.mean(x * x, axis=-1, keepdims=True)
    return x * lax.rsqrt(ms + RMS_EPS) * g


def _sigmoid(x):
    return 1.0 / (1.0 + jnp.exp(-x))


def _compiler_params(semantics):
    return pltpu.CompilerParams(dimension_semantics=semantics,
                                vmem_limit_bytes=V7X_VMEM_LIMIT_BYTES)


def _proj_kernel(x_ref, g_ref, w_ref, gq_ref, gk_ref, sb_ref, fx_ref, f_ref):
    h = _rmsnorm(x_ref[...], g_ref[...]).astype(BF16)
    wb = W_BRANCH
    r = lax.broadcasted_iota(jnp.int32, (wb, wb), 0) // HEAD_DIM
    c = lax.broadcasted_iota(jnp.int32, (wb, wb), 1) // HEAD_DIM
    head_mean = jnp.where(r == c, 1.0 / HEAD_DIM, 0.0).astype(BF16)
    for i in range(3):
        acc = _dot(h, w_ref[:, i * wb:(i + 1) * wb])
        if i == 0:
            acc = acc * QK_SCALE
        sb_ref[:, i * wb:(i + 1) * wb] = acc.astype(BF16)
    for i, gain_ref in enumerate((gq_ref, gk_ref, None)):
        acc = _dot(h, w_ref[:, (3 + i) * wb:(4 + i) * wb])
        if gain_ref is not None:
            ms = _dot((acc * acc).astype(BF16), head_mean)
            acc = acc * lax.rsqrt(ms + RMS_EPS) * gain_ref[...]
        if i == 0:
            acc = acc * QK_SCALE
        fx_ref[:, i * wb:(i + 1) * wb] = acc.astype(BF16)
    f_ref[...] = _dot(h, w_ref[:, 6 * wb:6 * wb + LANES])


def _proj(x, g, w, gq, gk, tm):
    t, d = x.shape
    n = w.shape[1]
    full = lambda i: (0, 0)
    return pl.pallas_call(
        _proj_kernel,
        grid=(t // tm,),
        in_specs=[
            pl.BlockSpec((tm, d), lambda i: (i, 0)),
            pl.BlockSpec((1, d), full),
            pl.BlockSpec((d, n), full),
            pl.BlockSpec((1, W_BRANCH), full),
            pl.BlockSpec((1, W_BRANCH), full),
        ],
        out_specs=[
            pl.BlockSpec((tm, 3 * W_BRANCH), lambda i: (i, 0)),
            pl.BlockSpec((tm, 3 * W_BRANCH), lambda i: (i, 0)),
            pl.BlockSpec((tm, LANES), lambda i: (i, 0)),
        ],
        out_shape=[
            jax.ShapeDtypeStruct((t, 3 * W_BRANCH), BF16),
            jax.ShapeDtypeStruct((t, 3 * W_BRANCH), BF16),
            jax.ShapeDtypeStruct((t, LANES), F32),
        ],
        compiler_params=_compiler_params(("parallel",)),
        name="proj",
    )(x, g, w, gq, gk)


def _cumf_kernel(f_ref, b_ref, o_ref):
    seq = f_ref.shape[1]
    cb = min(CUM_BLOCK, seq)
    r = lax.broadcasted_iota(jnp.int32, (cb, cb), 0)
    c = lax.broadcasted_iota(jnp.int32, (cb, cb), 1)
    prefix = jnp.where(r <= c, 1.0, 0.0).astype(BF16)
    carry = jnp.zeros((N_HEADS, 1), F32)
    for i in range(seq // cb):
        v = f_ref[0, i * cb:(i + 1) * cb, :] + b_ref[...]
        log_f = jnp.minimum(v, 0.0) - jnp.log1p(jnp.exp(-jnp.abs(v)))
        log_f = log_f.T[:N_HEADS, :]
        cum = carry
        for part in _split_bf16(log_f, 3):
            cum = cum + _dot(part, prefix)
        o_ref[0, :, i * cb:(i + 1) * cb] = cum
        carry = cum[:, cb - 1:cb]


def _cumf(f_pre, b_f):
    b, s, _ = f_pre.shape
    return pl.pallas_call(
        _cumf_kernel,
        grid=(b,),
        in_specs=[
            pl.BlockSpec((1, s, LANES), lambda i: (i, 0, 0)),
            pl.BlockSpec((1, LANES), lambda i: (0, 0)),
        ],
        out_specs=pl.BlockSpec((1, N_HEADS, s), lambda i: (i, 0, 0)),
        out_shape=jax.ShapeDtypeStruct((b, N_HEADS, s), F32),
        compiler_params=_compiler_params(("parallel",)),
        name="cumf",
    )(f_pre, b_f)


def _head_queries(q_ref, half):
    q = q_ref[0, :, half * LANES:(half + 1) * LANES].astype(F32)
    lane = lax.broadcasted_iota(jnp.int32, (1, LANES), 1)
    return [jnp.where((lane >= hh * HEAD_DIM) & (lane < (hh + 1) * HEAD_DIM), q, 0.0).astype(BF16)
            for hh in range(HEADS_PER_BLOCK)]


def _run_chains(q_ref, k_ref, v_ref, o_ref, stages, finish, stop):
    bk = ATT_BLOCK
    halves = range(q_ref.shape[2] // LANES)
    qi = pl.program_id(2)
    queries = [q for half in halves for q in _head_queries(q_ref, half)]

    def advance(kb, masked, states):
        start = pl.multiple_of(kb * bk, bk)
        kv = [(k_ref[0, pl.ds(start, bk), half * LANES:(half + 1) * LANES],
               v_ref[0, pl.ds(start, bk), half * LANES:(half + 1) * LANES]) for half in halves]
        args = [(head, q, *kv[head // HEADS_PER_BLOCK], start, masked)
                for head, q in enumerate(queries)]
        tmps = [None] * len(args)
        for stage in stages[:-1]:
            tmps = [stage(*a, tmp) for a, tmp in zip(args, tmps)]
        return tuple(stages[-1](*a, tmp, state) for a, tmp, state in zip(args, tmps, states))

    def body(carry):
        i, _, states = carry
        kb = qi - 1 - i
        states = advance(kb, False, states)
        return i + 1, stop(states, jnp.maximum(kb - 1, 0)).astype(jnp.int32), states

    states = advance(qi, True, [None] * len(queries))
    _, _, states = lax.while_loop(lambda c: jnp.logical_and(c[0] < qi, c[1] == 0), body,
                                  (jnp.int32(0), jnp.int32(0), states))
    lane = lax.broadcasted_iota(jnp.int32, (1, LANES), 1)
    for half in halves:
        outs = [finish(head, states[head])
                for head in range(half * HEADS_PER_BLOCK, (half + 1) * HEADS_PER_BLOCK)]
        o_ref[0, :, half * LANES:(half + 1) * LANES] = jnp.where(
            lane < HEAD_DIM, outs[0], outs[1]).astype(BF16)


def _sb_kernel(q_ref, k_ref, v_ref, o_ref):
    bk = ATT_BLOCK
    row = lax.broadcasted_iota(jnp.int32, (bk, bk), 0)
    col = lax.broadcasted_iota(jnp.int32, (bk, bk), 1)
    suffix = jnp.where(row >= col, 1.0, 0.0).astype(BF16)
    below_diag = col < row

    def scores(head, q, k, v, start, masked, tmp):
        z = _dot_nt(q, k)
        neg_abs = lax.bitcast_convert_type(
            lax.bitcast_convert_type(z, jnp.uint32) | jnp.uint32(0x80000000), F32)
        sp = jnp.maximum(z, 0.0) + jnp.log(1.0 + jnp.exp(neg_abs))
        if masked:
            sp = jnp.where(below_diag, sp, 0.0)
        return z, sp.astype(BF16)

    def suffix_sums(head, q, k, v, start, masked, tmp):
        z, sp_bf16 = tmp
        return z, _dot(sp_bf16, suffix)

    def weights(head, q, k, v, start, masked, tmp, state):
        z, incl = tmp
        arg = z - incl
        if state is not None:
            arg = arg - state[0]
        w = jnp.exp(arg)
        if masked:
            w = jnp.where(below_diag, w, 0.0)
        pv = _dot(w.astype(BF16), v)
        if state is None:
            return incl[:, 0:1], pv
        return state[0] + incl[:, 0:1], state[1] + pv

    def stop(states, next_block):
        carry = functools.reduce(jnp.minimum, [state[0] for state in states])
        return jnp.min(carry) > F32_EXP_UNDERFLOW

    _run_chains(q_ref, k_ref, v_ref, o_ref, (scores, suffix_sums, weights),
                lambda head, state: state[1], stop)


def _fox_kernel(zmax_ref, fend_ref, q_ref, k_ref, v_ref, f_ref, fcol_ref, o_ref):
    bk = ATT_BLOCK
    n_blocks = k_ref.shape[1] // bk
    n_heads = f_ref.shape[2]
    qi = pl.program_id(2)
    row = lax.broadcasted_iota(jnp.int32, (bk, bk), 0)
    col = lax.broadcasted_iota(jnp.int32, (bk, bk), 1)
    causal = col <= row
    lane = lax.broadcasted_iota(jnp.int32, (1, LANES), 1)
    table = (pl.program_id(0) * (W_BRANCH // HEAD_DIM) + pl.program_id(1) * n_heads) * n_blocks

    def ones_lanes_cache():
        values = {}

        def values_with_ones(hh, v):
            if (hh, id(v)) not in values:
                own = (lane >= hh * HEAD_DIM) & (lane < (hh + 1) * HEAD_DIM)
                values[(hh, id(v))] = (v, jnp.where(own, v.astype(F32), 1.0).astype(BF16))
            return values[(hh, id(v))][1]

        return values_with_ones

    def normalised(head, acc):
        ones_lane = (1 - head % HEADS_PER_BLOCK) * HEAD_DIM
        return acc / acc[:, ones_lane:ones_lane + 1]

    @pl.when(zmax_ref[0] <= FOX_CONSTANT_SHIFT_MAX)
    def _():
        values_with_ones = ones_lanes_cache()
        shift = [fcol_ref[0, :, head:head + 1] - zmax_ref[0] for head in range(n_heads)]

        def weights(head, q, k, v, start, masked, tmp):
            s = _dot_nt(q, k) - f_ref[0, 0, head:head + 1, pl.ds(start, bk)] + shift[head]
            if masked:
                s = jnp.where(causal, s, -jnp.inf)
            return jnp.exp(s).astype(BF16)

        def accumulate(head, q, k, v, start, masked, tmp, state):
            pv = _dot(tmp, values_with_ones(head % HEADS_PER_BLOCK, v))
            return pv if state is None else state + pv

        def stop(states, next_block):
            last = jnp.maximum(qi - 1, 0)
            gaps = [fend_ref[table + head * n_blocks + next_block] - fend_ref[table + head * n_blocks + last]
                    for head in range(n_heads)]
            return functools.reduce(jnp.maximum, gaps) < -F32_EXP_UNDERFLOW

        _run_chains(q_ref, k_ref, v_ref, o_ref, (weights, accumulate), normalised, stop)

    @pl.when(zmax_ref[0] > FOX_CONSTANT_SHIFT_MAX)
    def _():
        values_with_ones = ones_lanes_cache()

        def logits(head, q, k, v, start, masked, tmp):
            s = _dot_nt(q, k) - f_ref[0, 0, head:head + 1, pl.ds(start, bk)]
            if masked:
                s = jnp.where(causal, s, -jnp.inf)
            return s, jnp.max(s, axis=1, keepdims=True)

        def accumulate(head, q, k, v, start, masked, tmp, state):
            s, m_blk = tmp
            v_ones = values_with_ones(head % HEADS_PER_BLOCK, v)
            if state is None:
                return m_blk, _dot(jnp.exp(s - m_blk).astype(BF16), v_ones)
            m, acc = state
            m_new = jnp.maximum(m, m_blk)
            pv = _dot(jnp.exp(s - m_new).astype(BF16), v_ones)
            return m_new, jnp.exp(m - m_new) * acc + pv

        def stop(states, next_block):
            slack = functools.reduce(jnp.minimum, [
                state[0] - fend_ref[table + head * n_blocks + next_block]
                for head, state in enumerate(states)])
            return jnp.min(slack) > zmax_ref[0] + F32_EXP_UNDERFLOW

        _run_chains(q_ref, k_ref, v_ref, o_ref, (logits, accumulate),
                    lambda head, state: normalised(head, state[1]), stop)


def _attention(kernel, qkv, cum_f=None, zmax=None):
    b, s, _ = qkv.shape
    bq = min(ATT_BLOCK, s)
    width = ATT_LANE_BLOCKS * LANES
    n_groups = W_BRANCH // width
    in_specs = [
        pl.BlockSpec((1, bq, width), lambda bi, g, qi: (bi, qi, g)),
        pl.BlockSpec((1, s, width), lambda bi, g, qi: (bi, 0, n_groups + g)),
        pl.BlockSpec((1, s, width), lambda bi, g, qi: (bi, 0, 2 * n_groups + g)),
    ]
    args = [qkv, qkv, qkv]
    if cum_f is not None:
        heads = ATT_LANE_BLOCKS * HEADS_PER_BLOCK
        smem = pl.BlockSpec(memory_space=pltpu.SMEM)
        in_specs = [smem, smem] + in_specs + [
            pl.BlockSpec((1, 1, heads, s), lambda bi, g, qi: (bi, g, 0, 0)),
            pl.BlockSpec((1, bq, heads), lambda bi, g, qi: (bi * n_groups + g, qi, 0))]
        f_end = -cum_f[:, :, bq - 1::bq]
        by_head = cum_f.reshape(b, n_groups, heads, s)
        args = [zmax, f_end.reshape(-1)] + args + [
            by_head, jnp.swapaxes(by_head, 2, 3).reshape(b * n_groups, s, heads)]
    return pl.pallas_call(
        kernel,
        grid=(b, n_groups, s // bq),
        in_specs=in_specs,
        out_specs=pl.BlockSpec((1, bq, width), lambda bi, g, qi: (bi, qi, g)),
        out_shape=jax.ShapeDtypeStruct((b, s, W_BRANCH), BF16),
        compiler_params=_compiler_params(("parallel", "parallel", "arbitrary")),
        name=kernel.__name__.strip("_"),
    )(*args)


def _local_rows(tm):
    return TOP_K * tm + N_EXPERTS * MOE_CHUNK


def _route_and_sort(logits, h2, xs_ref, len_ref, route_ref):
    tm = logits.shape[0]
    lt = logits.T[:N_EXPERTS, :]
    expert = lax.broadcasted_iota(jnp.int32, lt.shape, 0).astype(F32)
    m1 = jnp.max(lt, axis=0, keepdims=True)
    i1 = jnp.min(jnp.where(lt == m1, expert, float(N_EXPERTS)), axis=0, keepdims=True)
    lt2 = jnp.where(expert == i1, -jnp.inf, lt)
    m2 = jnp.max(lt2, axis=0, keepdims=True)
    i2 = jnp.min(jnp.where(lt2 == m2, expert, float(N_EXPERTS)), axis=0, keepdims=True)
    e2 = jnp.exp(m2 - m1)
    w1 = 1.0 / (1.0 + e2)
    w2 = e2 * w1
    chosen = jnp.where((expert == i1) | (expert == i2), 1.0, 0.0)
    yield
    before = (lax.broadcasted_iota(jnp.int32, (tm, tm), 0)
              < lax.broadcasted_iota(jnp.int32, (tm, tm), 1))
    rank = _dot(chosen.astype(BF16), jnp.where(before, 1.0, 0.0).astype(BF16))
    count = jnp.sum(chosen, axis=1, keepdims=True)
    padded = jnp.floor((count + (MOE_CHUNK - 1)) * (1.0 / MOE_CHUNK)) * MOE_CHUNK
    r1 = jnp.sum(jnp.where(expert == i1, rank, 0.0), axis=0, keepdims=True)
    r2 = jnp.sum(jnp.where(expert == i2, rank, 0.0), axis=0, keepdims=True)
    start = jnp.zeros((1, 1), F32)
    for e in range(N_EXPERTS):
        r1 = r1 + jnp.where(i1 == e, start, 0.0)
        r2 = r2 + jnp.where(i2 == e, start, 0.0)
        start = start + padded[e:e + 1, :]
    rows = _local_rows(tm)
    row = lax.broadcasted_iota(jnp.int32, (rows, tm), 0).astype(F32)
    place = jnp.where((row == r1) | (row == r2), 1.0, 0.0).astype(BF16)
    yield
    xs_ref[0] = _dot(place, h2).astype(BF16)
    len_ref[0] = jnp.broadcast_to(padded, (N_EXPERTS, LANES))
    field = lax.broadcasted_iota(jnp.int32, (LANES, tm), 0)
    fields = jnp.where(field == 0, r1, jnp.where(field == 1, r2, jnp.where(
        field == 2, w1, jnp.where(field == 3, w2, 0.0))))
    route_ref[...] = fields.T


def _merge_kernel(x_ref, ysb_ref, yfx_ref, gmix_ref, wgate_ref, bgate_ref, wosb_ref, wofx_ref,
                  wout_ref, gffn_ref, *rest, with_router):
    if with_router:
        wr_hi_ref, wr_lo_ref, x1_ref, xs_ref, len_ref, route_ref, h2_keep, logits_keep = rest
        step = pl.program_id(0)
        keep = lax.rem(step, 2)

        @pl.when(step == 0)
        def _():
            h2_keep[...] = jnp.zeros_like(h2_keep)
            logits_keep[...] = jnp.zeros_like(logits_keep)

        routing = _route_and_sort(logits_keep[1 - keep], h2_keep[1 - keep], xs_ref, len_ref, route_ref)
        next(routing)
    else:
        x1_ref, h2_ref = rest
    d = x_ref.shape[1]
    x = x_ref[...]
    h = _rmsnorm(x, gmix_ref[...]).astype(BF16)
    merged = None
    for i, (y_ref, wo_ref) in enumerate(((ysb_ref, wosb_ref), (yfx_ref, wofx_ref))):
        gate = _sigmoid(_dot(h, wgate_ref[:, i * d:(i + 1) * d]) + bgate_ref[:, i * d:(i + 1) * d])
        branch = gate * _dot(y_ref[...], wo_ref[...])
        merged = branch if merged is None else merged + branch
    if with_router:
        next(routing)
    x1 = x + _dot(merged.astype(BF16), wout_ref[...])
    x1_ref[...] = x1
    h2 = _rmsnorm(x1, gffn_ref[...])
    if with_router:
        h_hi, h_lo = _split_bf16(h2, 2)
        logits_keep[keep] = (_dot(h_hi, wr_hi_ref[...]) + _dot(h_lo, wr_hi_ref[...])
                             + _dot(h_hi, wr_lo_ref[...]))
        h2_keep[keep] = h_hi
        next(routing, None)
    else:
        h2_ref[...] = h2.astype(BF16)


def _merge(x, y_sb, y_fx, g_mix, w_gate, b_gate, w_o_sb, w_o_fox, w_out, g_ffn, w_router, tm):
    t, d = x.shape
    with_router = w_router is not None
    n_tiles = t // tm
    full = lambda i: (0, 0)
    tile = lambda i: (jnp.minimum(i, n_tiles - 1), 0)
    routed = lambda i: (jnp.maximum(i - 1, 0), 0)
    scratch = []
    in_specs = [
        pl.BlockSpec((tm, d), tile),
        pl.BlockSpec((tm, W_BRANCH), tile),
        pl.BlockSpec((tm, W_BRANCH), tile),
        pl.BlockSpec((1, d), full),
        pl.BlockSpec((d, 2 * d), full),
        pl.BlockSpec((1, 2 * d), full),
        pl.BlockSpec((W_BRANCH, d), full),
        pl.BlockSpec((W_BRANCH, d), full),
        pl.BlockSpec((d, d), full),
        pl.BlockSpec((1, d), full),
    ]
    args = [x, y_sb, y_fx, g_mix, w_gate, b_gate, w_o_sb, w_o_fox, w_out, g_ffn]
    out_specs = [pl.BlockSpec((tm, d), tile)]
    out_shape = [jax.ShapeDtypeStruct((t, d), F32)]
    if with_router:
        rows = _local_rows(tm)
        in_specs += [pl.BlockSpec((d, LANES), full)] * 2
        args += list(w_router)
        out_specs += [pl.BlockSpec((1, rows, d), lambda i: routed(i) + (0,)),
                      pl.BlockSpec((1, N_EXPERTS, LANES), lambda i: routed(i) + (0,)),
                      pl.BlockSpec((tm, LANES), routed)]
        out_shape += [jax.ShapeDtypeStruct((n_tiles, rows, d), BF16),
                      jax.ShapeDtypeStruct((n_tiles, N_EXPERTS, LANES), F32),
                      jax.ShapeDtypeStruct((t, LANES), F32)]
        scratch = [pltpu.VMEM((2, tm, d), BF16), pltpu.VMEM((2, tm, LANES), F32)]
    else:
        out_specs.append(pl.BlockSpec((tm, d), tile))
        out_shape.append(jax.ShapeDtypeStruct((t, d), BF16))
    return pl.pallas_call(
        functools.partial(_merge_kernel, with_router=with_router),
        grid=(n_tiles + (1 if with_router else 0),),
        in_specs=in_specs,
        out_specs=out_specs,
        out_shape=out_shape,
        scratch_shapes=scratch,
        compiler_params=_compiler_params(("arbitrary",) if with_router else ("parallel",)),
        name="merge_router" if with_router else "merge",
    )(*args)


def _swiglu_chunk(h, wg, wu, wd):
    g = _dot(h, wg)
    u = _dot(h, wu)
    return _dot((g * _sigmoid(g) * u).astype(BF16), wd)


def _ffn_kernel(h_ref, x_ref, wg_ref, wu_ref, wd_ref, o_ref):
    @pl.when(pl.program_id(1) == 0)
    def _():
        o_ref[...] = x_ref[...]

    o_ref[...] += _swiglu_chunk(h_ref[...], wg_ref[...], wu_ref[...], wd_ref[...])


def _ffn(h, x, w_gu, w_dn, tm, tf):
    t, d = x.shape
    d_ff = w_dn.shape[0]
    nf = d_ff // tf
    return pl.pallas_call(
        _ffn_kernel,
        grid=(t // tm, nf),
        in_specs=[
            pl.BlockSpec((tm, d), lambda i, j: (i, 0)),
            pl.BlockSpec((tm, d), lambda i, j: (i, 0)),
            pl.BlockSpec((d, tf), lambda i, j: (0, j)),
            pl.BlockSpec((d, tf), lambda i, j: (0, nf + j)),
            pl.BlockSpec((tf, d), lambda i, j: (j, 0)),
        ],
        out_specs=pl.BlockSpec((tm, d), lambda i, j: (i, 0)),
        out_shape=jax.ShapeDtypeStruct((t, d), F32),
        compiler_params=_compiler_params(("parallel", "arbitrary")),
        name="ffn",
    )(h, x, w_gu, w_gu, w_dn)


def _expert_kernel(expert_ref, valid_ref, src_ref, xs_hbm, wg_ref, wu_ref, wd_ref, o_ref, buf, sems):
    j = pl.program_id(0)
    rows = buf.shape[1]
    chunks = rows // MOE_CHUNK

    def fetches(step, slot):
        return [pltpu.make_async_copy(
            xs_hbm.at[src_ref[step * chunks + c]],
            buf.at[slot, pl.ds(c * MOE_CHUNK, MOE_CHUNK), :], sems.at[slot]) for c in range(chunks)]

    slot = lax.rem(j, 2)

    @pl.when(j == 0)
    def _():
        for copy in fetches(0, 0):
            copy.start()

    @pl.when(j + 1 < pl.num_programs(0))
    def _():
        for copy in fetches(j + 1, 1 - slot):
            copy.start()

    for copy in fetches(j, slot):
        copy.wait()

    @pl.when(valid_ref[j] != 0)
    def _():
        o_ref[...] = _swiglu_chunk(buf[slot], wg_ref[0], wu_ref[0], wd_ref[0]).astype(BF16)

    @pl.when(valid_ref[j] == 0)
    def _():
        o_ref[...] = jnp.zeros_like(o_ref)


def _experts(xs, tile_expert, tile_valid, chunk_src, w_gu, w_dn, n_rows):
    d = xs.shape[2]
    d_ff = w_dn.shape[1]
    te = MOE_EXPERT_TILE
    grid_spec = pltpu.PrefetchScalarGridSpec(
        num_scalar_prefetch=3,
        grid=(n_rows // te,),
        in_specs=[
            pl.BlockSpec(memory_space=pl.ANY),
            pl.BlockSpec((1, d, d_ff), lambda j, ex, va, sr: (ex[j], 0, 0)),
            pl.BlockSpec((1, d, d_ff), lambda j, ex, va, sr: (ex[j], 0, 1)),
            pl.BlockSpec((1, d_ff, d), lambda j, ex, va, sr: (ex[j], 0, 0)),
        ],
        out_specs=pl.BlockSpec((te, d), lambda j, ex, va, sr: (j, 0)),
        scratch_shapes=[pltpu.VMEM((2, te, d), BF16), pltpu.SemaphoreType.DMA((2,))],
    )
    return pl.pallas_call(
        _expert_kernel,
        grid_spec=grid_spec,
        out_shape=jax.ShapeDtypeStruct((n_rows, d), BF16),
        compiler_params=_compiler_params(("arbitrary",)),
        name="experts",
    )(tile_expert, tile_valid, chunk_src, xs, w_gu, w_gu, w_dn)


def _combine_kernel(goff_ref, count_ref, loff_ref, x_ref, route_ref, ys_hbm, o_ref, buf, sems):
    i = pl.program_id(0)
    slot = lax.rem(i, 2)

    def for_each_chunk(step, which, act):
        for e in range(N_EXPERTS):
            g = step * N_EXPERTS + e

            @pl.loop(0, count_ref[g])
            def _(c):
                dst = pl.multiple_of((loff_ref[g] + c) * MOE_CHUNK, MOE_CHUNK)
                act(pltpu.make_async_copy(ys_hbm.at[goff_ref[g] + c],
                                          buf.at[which, pl.ds(dst, MOE_CHUNK), :], sems.at[which]))

    @pl.when(i == 0)
    def _():
        buf[...] = jnp.zeros_like(buf)
        for_each_chunk(0, 0, lambda copy: copy.start())

    @pl.when(i + 1 < pl.num_programs(0))
    def _():
        for_each_chunk(i + 1, 1 - slot, lambda copy: copy.start())

    for_each_chunk(i, slot, lambda copy: copy.wait())

    route = route_ref[...]
    y = buf[slot]
    col = lax.broadcasted_iota(jnp.int32, (route.shape[0], buf.shape[1]), 1).astype(F32)
    out = x_ref[...]
    for k in range(TOP_K):
        pick = jnp.where(col == route[:, k:k + 1], 1.0, 0.0).astype(BF16)
        out = out + route[:, TOP_K + k:TOP_K + k + 1] * _dot(pick, y)
    o_ref[...] = out


def _combine(x, route, ys, goff, count, loff, tm):
    t, d = x.shape
    grid_spec = pltpu.PrefetchScalarGridSpec(
        num_scalar_prefetch=3,
        grid=(t // tm,),
        in_specs=[
            pl.BlockSpec((tm, d), lambda i, *_: (i, 0)),
            pl.BlockSpec((tm, LANES), lambda i, *_: (i, 0)),
            pl.BlockSpec(memory_space=pl.ANY),
        ],
        out_specs=pl.BlockSpec((tm, d), lambda i, *_: (i, 0)),
        scratch_shapes=[pltpu.VMEM((2, _local_rows(tm), d), BF16), pltpu.SemaphoreType.DMA((2,))],
    )
    return pl.pallas_call(
        _combine_kernel,
        grid_spec=grid_spec,
        out_shape=jax.ShapeDtypeStruct((t, d), F32),
        compiler_params=_compiler_params(("arbitrary",)),
        name="combine",
    )(goff, count, loff, x, route, ys)


def _moe(x1, xs_local, padded_len, route, w_gu, w_dn, tm):
    t, d = x1.shape
    n_tiles, local_rows, _ = xs_local.shape
    te = MOE_EXPERT_TILE
    n_rows = -(-(TOP_K * t + n_tiles * N_EXPERTS * (MOE_CHUNK - 1) + N_EXPERTS * (te - 1)) // te) * te
    length = padded_len[:, :, 0].astype(jnp.int32)
    count = length // MOE_CHUNK
    used = jnp.sum(length, axis=0)
    span = -(-used // te) * te
    base = jnp.cumsum(span) - span
    goff = (base[None, :] + jnp.cumsum(length, axis=0) - length) // MOE_CHUNK
    loff = (jnp.cumsum(length, axis=1) - length) // MOE_CHUNK
    seg_start = goff.T.reshape(-1)
    seg_count = count.T.reshape(-1)
    seg_src = (jnp.arange(n_tiles, dtype=jnp.int32)[:, None] * (local_rows // MOE_CHUNK) + loff).T.reshape(-1)
    chunk = jnp.arange(n_rows // MOE_CHUNK, dtype=jnp.int32)[:, None]
    within = chunk - seg_start[None, :]
    owner = (within >= 0) & (within < seg_count[None, :])
    chunk_src = jnp.sum(jnp.where(owner, seg_src[None, :] + within, 0), axis=1).astype(jnp.int32)
    tile_row = jnp.arange(n_rows // te, dtype=jnp.int32)[:, None] * te
    tile_expert = (jnp.sum(base[None, :] <= tile_row, axis=1) - 1).astype(jnp.int32)
    tile_valid = jnp.any((base[None, :] <= tile_row) & (tile_row < (base + used)[None, :]),
                         axis=1).astype(jnp.int32)
    ys = _experts(xs_local.reshape(-1, MOE_CHUNK, d), tile_expert, tile_valid, chunk_src, w_gu, w_dn, n_rows)
    return _combine(x1, route, ys.reshape(-1, MOE_CHUNK, d), goff.reshape(-1), count.reshape(-1),
                    loff.reshape(-1), tm)


def _pad_lanes(w):
    return jnp.pad(w, ((0, 0), (0, LANES - w.shape[1])))


def kernel(x, g_mix, w_in, b_f, b_gate, g_q, g_k, w_o_sb, w_o_fox, w_out, g_ffn, w_gu_dense,
           w_dn_dense, w_router, w_gu_exp, w_dn_exp):
    batch, seq, d = x.shape
    depth = w_in.shape[0]
    t = batch * seq
    tm = min(TOKEN_TILE, t)
    n_qkv = 6 * W_BRANCH
    xt = x.reshape(t, d)
    for l in range(depth):
        w_l = w_in[l]
        w_proj = jnp.concatenate(
            [w_l[:, :n_qkv], _pad_lanes(w_l[:, n_qkv:n_qkv + N_HEADS])], axis=1).astype(BF16)
        w_gate = w_l[:, n_qkv + N_HEADS:].astype(BF16)
        row = lambda v: v.reshape(1, -1)
        gq = jnp.tile(g_q[l], N_HEADS).reshape(1, W_BRANCH)
        gk = jnp.tile(g_k[l], N_HEADS).reshape(1, W_BRANCH)
        qkv_sb, qkv_fx, f_pre = _proj(xt, row(g_mix[l]), w_proj, gq, gk, tm)
        cum_f = _cumf(f_pre.reshape(batch, seq, LANES), _pad_lanes(row(b_f[l])))
        y_sb = _attention(_sb_kernel, qkv_sb.reshape(batch, seq, 3 * W_BRANCH))
        zmax = (QK_BOUND_MARGIN * 8.0 * jnp.max(jnp.abs(g_q[l])) * jnp.max(jnp.abs(g_k[l]))).reshape(1)
        y_fx = _attention(_fox_kernel, qkv_fx.reshape(batch, seq, 3 * W_BRANCH), cum_f, zmax)
        moe_layer = l % 2 == 1
        router = None
        if moe_layer:
            wr = _pad_lanes(w_router[l // 2])
            router = _split_bf16(wr, 2)
        outs = _merge(xt, y_sb.reshape(t, W_BRANCH), y_fx.reshape(t, W_BRANCH), row(g_mix[l]),
                      w_gate, row(b_gate[l]), w_o_sb[l].astype(BF16), w_o_fox[l].astype(BF16),
                      w_out[l].astype(BF16), row(g_ffn[l]), router, tm)
        if moe_layer:
            x1, xs_local, padded_len, route = outs
            xt = _moe(x1, xs_local, padded_len, route, w_gu_exp[l // 2].astype(BF16),
                      w_dn_exp[l // 2].astype(BF16), tm)
        else:
            x1, h2 = outs
            xt = _ffn(h2, x1, w_gu_dense[l // 2].astype(BF16), w_dn_dense[l // 2].astype(BF16),
                      tm, w_dn_dense.shape[1] // 2)
    return xt.reshape(batch, seq, d)
```

```python
import functools

import jax
import jax.numpy as jnp
from jax import lax
from jax.experimental import pallas as pl
from jax.experimental.pallas import tpu as pltpu

F32 = jnp.float32
BF16 = jnp.bfloat16

HEAD_DIM = 64
N_HEADS = 8
W_BRANCH = N_HEADS * HEAD_DIM
N_EXPERTS = 8
TOP_K = 2
TOKEN_TILE = 512
MOE_CHUNK = 16
MOE_EXPERT_TILE = 512
RMS_EPS = 1e-6
QK_SCALE = 1.0 / 8.0
LANES = 128
HEADS_PER_BLOCK = LANES // HEAD_DIM
V7X_VMEM_LIMIT_BYTES = 56 * 1024 * 1024

ATT_BLOCK = 256
ATT_LANE_BLOCKS = 4
F32_EXP_UNDERFLOW = 106.0
FOX_CONSTANT_SHIFT_MAX = 30.0
QK_BOUND_MARGIN = 1.05
CUM_BLOCK = 512


def _split_bf16(x, parts):
    out = []
    for _ in range(parts - 1):
        hi = x.astype(BF16)
        out.append(hi)
        x = x - hi.astype(F32)
    out.append(x.astype(BF16))
    return out


def _dot(a, b):
    return jnp.dot(a, b, preferred_element_type=F32)


def _dot_nt(a, b):
    return lax.dot_general(a, b, (((1,), (1,)), ((), ())), preferred_element_type=F32)


def _rmsnorm(x, g):
    ms = jnp.mean(x * x, axis=-1, keepdims=True)
    return x * lax.rsqrt(ms + RMS_EPS) * g


def _sigmoid(x):
    return 1.0 / (1.0 + jnp.exp(-x))


def _compiler_params(semantics):
    return pltpu.CompilerParams(dimension_semantics=semantics,
                                vmem_limit_bytes=V7X_VMEM_LIMIT_BYTES)


def _proj_kernel(x_ref, g_ref, w_ref, gq_ref, gk_ref, sb_ref, fx_ref, f_ref):
    h = _rmsnorm(x_ref[...], g_ref[...]).astype(BF16)
    wb = W_BRANCH
    r = lax.broadcasted_iota(jnp.int32, (wb, wb), 0) // HEAD_DIM
    c = lax.broadcasted_iota(jnp.int32, (wb, wb), 1) // HEAD_DIM
    head_mean = jnp.where(r == c, 1.0 / HEAD_DIM, 0.0).astype(BF16)
    for i in range(3):
        acc = _dot(h, w_ref[:, i * wb:(i + 1) * wb])
        if i == 0:
            acc = acc * QK_SCALE
        sb_ref[:, i * wb:(i + 1) * wb] = acc.astype(BF16)
    for i, gain_ref in enumerate((gq_ref, gk_ref, None)):
        acc = _dot(h, w_ref[:, (3 + i) * wb:(4 + i) * wb])
        if gain_ref is not None:
            ms = _dot((acc * acc).astype(BF16), head_mean)
            acc = acc * lax.rsqrt(ms + RMS_EPS) * gain_ref[...]
        if i == 0:
            acc = acc * QK_SCALE
        fx_ref[:, i * wb:(i + 1) * wb] = acc.astype(BF16)
    f_ref[...] = _dot(h, w_ref[:, 6 * wb:6 * wb + LANES])


def _proj(x, g, w, gq, gk, tm):
    t, d = x.shape
    n = w.shape[1]
    full = lambda i: (0, 0)
    return pl.pallas_call(
        _proj_kernel,
        grid=(t // tm,),
        in_specs=[
            pl.BlockSpec((tm, d), lambda i: (i, 0)),
            pl.BlockSpec((1, d), full),
            pl.BlockSpec((d, n), full),
            pl.BlockSpec((1, W_BRANCH), full),
            pl.BlockSpec((1, W_BRANCH), full),
        ],
        out_specs=[
            pl.BlockSpec((tm, 3 * W_BRANCH), lambda i: (i, 0)),
            pl.BlockSpec((tm, 3 * W_BRANCH), lambda i: (i, 0)),
            pl.BlockSpec((tm, LANES), lambda i: (i, 0)),
        ],
        out_shape=[
            jax.ShapeDtypeStruct((t, 3 * W_BRANCH), BF16),
            jax.ShapeDtypeStruct((t, 3 * W_BRANCH), BF16),
            jax.ShapeDtypeStruct((t, LANES), F32),
        ],
        compiler_params=_compiler_params(("parallel",)),
        name="proj",
    )(x, g, w, gq, gk)


def _cumf_kernel(f_ref, b_ref, o_ref):
    seq = f_ref.shape[1]
    cb = min(CUM_BLOCK, seq)
    r = lax.broadcasted_iota(jnp.int32, (cb, cb), 0)
    c = lax.broadcasted_iota(jnp.int32, (cb, cb), 1)
    prefix = jnp.where(r <= c, 1.0, 0.0).astype(BF16)
    carry = jnp.zeros((N_HEADS, 1), F32)
    for i in range(seq // cb):
        v = f_ref[0, i * cb:(i + 1) * cb, :] + b_ref[...]
        log_f = jnp.minimum(v, 0.0) - jnp.log1p(jnp.exp(-jnp.abs(v)))
        log_f = log_f.T[:N_HEADS, :]
        cum = carry
        for part in _split_bf16(log_f, 3):
            cum = cum + _dot(part, prefix)
        o_ref[0, :, i * cb:(i + 1) * cb] = cum
        carry = cum[:, cb - 1:cb]


def _cumf(f_pre, b_f):
    b, s, _ = f_pre.shape
    return pl.pallas_call(
        _cumf_kernel,
        grid=(b,),
        in_specs=[
            pl.BlockSpec((1, s, LANES), lambda i: (i, 0, 0)),
            pl.BlockSpec((1, LANES), lambda i: (0, 0)),
        ],
        out_specs=pl.BlockSpec((1, N_HEADS, s), lambda i: (i, 0, 0)),
        out_shape=jax.ShapeDtypeStruct((b, N_HEADS, s), F32),
        compiler_params=_compiler_params(("parallel",)),
        name="cumf",
    )(f_pre, b_f)


def _head_queries(q_ref, half):
    q = q_ref[0, :, half * LANES:(half + 1) * LANES].astype(F32)
    lane = lax.broadcasted_iota(jnp.int32, (1, LANES), 1)
    return [jnp.where((lane >= hh * HEAD_DIM) & (lane < (hh + 1) * HEAD_DIM), q, 0.0).astype(BF16)
            for hh in range(HEADS_PER_BLOCK)]


def _run_chains(q_ref, k_ref, v_ref, o_ref, stages, finish, stop):
    bk = ATT_BLOCK
    halves = range(q_ref.shape[2] // LANES)
    qi = pl.program_id(2)
    queries = [q for half in halves for q in _head_queries(q_ref, half)]

    def advance(blocks, masked, states):
        args = []
        for kb in blocks:
            start = pl.multiple_of(kb * bk, bk)
            kv = [(k_ref[0, pl.ds(start, bk), half * LANES:(half + 1) * LANES],
                   v_ref[0, pl.ds(start, bk), half * LANES:(half + 1) * LANES]) for half in halves]
            args += [(head, q, *kv[head // HEADS_PER_BLOCK], start, masked)
                     for head, q in enumerate(queries)]
        tmps = [None] * len(args)
        for stage in stages[:-1]:
            tmps = [stage(*a, tmp) for a, tmp in zip(args, tmps)]
        states = list(states)
        for a, tmp in zip(args, tmps):
            states[a[0]] = stages[-1](*a, tmp, states[a[0]])
        return tuple(states)

    states = advance([qi], True, [None] * len(queries))
    if callable(stop):
        def body(carry):
            i, _, states = carry
            kb = qi - 1 - i
            states = advance([kb], False, states)
            return i + 1, stop(states, jnp.maximum(kb - 1, 0)).astype(jnp.int32), states

        _, _, states = lax.while_loop(lambda c: jnp.logical_and(c[0] < qi, c[1] == 0), body,
                                      (jnp.int32(0), jnp.int32(0), states))
    else:
        n_blocks = stop
        states = lax.fori_loop(
            0, n_blocks // 2,
            lambda i, s: advance([qi - 1 - 2 * i, qi - 2 - 2 * i], False, s), states)
        states = lax.cond(n_blocks % 2 == 1,
                          lambda s: advance([jnp.maximum(qi - n_blocks, 0)], False, s),
                          lambda s: s, states)
    lane = lax.broadcasted_iota(jnp.int32, (1, LANES), 1)
    for half in halves:
        outs = [finish(head, states[head])
                for head in range(half * HEADS_PER_BLOCK, (half + 1) * HEADS_PER_BLOCK)]
        o_ref[0, :, half * LANES:(half + 1) * LANES] = jnp.where(
            lane < HEAD_DIM, outs[0], outs[1]).astype(BF16)


def _sb_kernel(q_ref, k_ref, v_ref, o_ref):
    bk = ATT_BLOCK
    row = lax.broadcasted_iota(jnp.int32, (bk, bk), 0)
    col = lax.broadcasted_iota(jnp.int32, (bk, bk), 1)
    suffix = jnp.where(row >= col, 1.0, 0.0).astype(BF16)
    below_diag = col < row

    def scores(head, q, k, v, start, masked, tmp):
        z = _dot_nt(q, k)
        neg_abs = lax.bitcast_convert_type(
            lax.bitcast_convert_type(z, jnp.uint32) | jnp.uint32(0x80000000), F32)
        sp = jnp.maximum(z, 0.0) + jnp.log(1.0 + jnp.exp(neg_abs))
        if masked:
            sp = jnp.where(below_diag, sp, 0.0)
        return z, sp.astype(BF16)

    def suffix_sums(head, q, k, v, start, masked, tmp):
        z, sp_bf16 = tmp
        return z, _dot(sp_bf16, suffix)

    def weights(head, q, k, v, start, masked, tmp, state):
        z, incl = tmp
        arg = z - incl
        if state is not None:
            arg = arg - state[0]
        w = jnp.exp(arg)
        if masked:
            w = jnp.where(below_diag, w, 0.0)
        pv = _dot(w.astype(BF16), v)
        if state is None:
            return incl[:, 0:1], pv
        return state[0] + incl[:, 0:1], state[1] + pv

    def stop(states, next_block):
        carry = functools.reduce(jnp.minimum, [state[0] for state in states])
        return jnp.min(carry) > F32_EXP_UNDERFLOW

    _run_chains(q_ref, k_ref, v_ref, o_ref, (scores, suffix_sums, weights),
                lambda head, state: state[1], stop)


def _fox_kernel(zmax_ref, fend_ref, q_ref, k_ref, v_ref, f_ref, fcol_ref, o_ref):
    bk = ATT_BLOCK
    n_blocks = k_ref.shape[1] // bk
    n_heads = f_ref.shape[2]
    qi = pl.program_id(2)
    row = lax.broadcasted_iota(jnp.int32, (bk, bk), 0)
    col = lax.broadcasted_iota(jnp.int32, (bk, bk), 1)
    causal = col <= row
    lane = lax.broadcasted_iota(jnp.int32, (1, LANES), 1)
    table = (pl.program_id(0) * (W_BRANCH // HEAD_DIM) + pl.program_id(1) * n_heads) * n_blocks

    def ones_lanes_cache():
        values = {}

        def values_with_ones(hh, v):
            if (hh, id(v)) not in values:
                own = (lane >= hh * HEAD_DIM) & (lane < (hh + 1) * HEAD_DIM)
                values[(hh, id(v))] = (v, jnp.where(own, v.astype(F32), 1.0).astype(BF16))
            return values[(hh, id(v))][1]

        return values_with_ones

    def normalised(head, acc):
        ones_lane = (1 - head % HEADS_PER_BLOCK) * HEAD_DIM
        return acc / acc[:, ones_lane:ones_lane + 1]

    @pl.when(zmax_ref[0] <= FOX_CONSTANT_SHIFT_MAX)
    def _():
        values_with_ones = ones_lanes_cache()
        shift = [fcol_ref[0, :, head:head + 1] - zmax_ref[0] for head in range(n_heads)]

        def weights(head, q, k, v, start, masked, tmp):
            s = _dot_nt(q, k) - f_ref[0, 0, head:head + 1, pl.ds(start, bk)] + shift[head]
            if masked:
                s = jnp.where(causal, s, -jnp.inf)
            return jnp.exp(s).astype(BF16)

        def accumulate(head, q, k, v, start, masked, tmp, state):
            pv = _dot(tmp, values_with_ones(head % HEADS_PER_BLOCK, v))
            return pv if state is None else state + pv

        def contributes(block):
            last = jnp.maximum(qi - 1, 0)
            gaps = [fend_ref[table + head * n_blocks + block] - fend_ref[table + head * n_blocks + last]
                    for head in range(n_heads)]
            return functools.reduce(jnp.maximum, gaps) >= -F32_EXP_UNDERFLOW

        needed = lax.while_loop(
            lambda n: jnp.logical_and(n < qi, contributes(jnp.maximum(qi - 1 - n, 0))),
            lambda n: n + 1, jnp.int32(0))
        _run_chains(q_ref, k_ref, v_ref, o_ref, (weights, accumulate), normalised, needed)

    @pl.when(zmax_ref[0] > FOX_CONSTANT_SHIFT_MAX)
    def _():
        values_with_ones = ones_lanes_cache()

        def logits(head, q, k, v, start, masked, tmp):
            s = _dot_nt(q, k) - f_ref[0, 0, head:head + 1, pl.ds(start, bk)]
            if masked:
                s = jnp.where(causal, s, -jnp.inf)
            return s, jnp.max(s, axis=1, keepdims=True)

        def accumulate(head, q, k, v, start, masked, tmp, state):
            s, m_blk = tmp
            v_ones = values_with_ones(head % HEADS_PER_BLOCK, v)
            if state is None:
                return m_blk, _dot(jnp.exp(s - m_blk).astype(BF16), v_ones)
            m, acc = state
            m_new = jnp.maximum(m, m_blk)
            pv = _dot(jnp.exp(s - m_new).astype(BF16), v_ones)
            return m_new, jnp.exp(m - m_new) * acc + pv

        def stop(states, next_block):
            slack = functools.reduce(jnp.minimum, [
                state[0] - fend_ref[table + head * n_blocks + next_block]
                for head, state in enumerate(states)])
            return jnp.min(slack) > zmax_ref[0] + F32_EXP_UNDERFLOW

        _run_chains(q_ref, k_ref, v_ref, o_ref, (logits, accumulate),
                    lambda head, state: normalised(head, state[1]), stop)


def _attention(kernel, qkv, cum_f=None, zmax=None):
    b, s, _ = qkv.shape
    bq = min(ATT_BLOCK, s)
    width = ATT_LANE_BLOCKS * LANES
    n_groups = W_BRANCH // width
    in_specs = [
        pl.BlockSpec((1, bq, width), lambda bi, g, qi: (bi, qi, g)),
        pl.BlockSpec((1, s, width), lambda bi, g, qi: (bi, 0, n_groups + g)),
        pl.BlockSpec((1, s, width), lambda bi, g, qi: (bi, 0, 2 * n_groups + g)),
    ]
    args = [qkv, qkv, qkv]
    if cum_f is not None:
        heads = ATT_LANE_BLOCKS * HEADS_PER_BLOCK
        smem = pl.BlockSpec(memory_space=pltpu.SMEM)
        in_specs = [smem, smem] + in_specs + [
            pl.BlockSpec((1, 1, heads, s), lambda bi, g, qi: (bi, g, 0, 0)),
            pl.BlockSpec((1, bq, heads), lambda bi, g, qi: (bi * n_groups + g, qi, 0))]
        f_end = -cum_f[:, :, bq - 1::bq]
        by_head = cum_f.reshape(b, n_groups, heads, s)
        args = [zmax, f_end.reshape(-1)] + args + [
            by_head, jnp.swapaxes(by_head, 2, 3).reshape(b * n_groups, s, heads)]
    return pl.pallas_call(
        kernel,
        grid=(b, n_groups, s // bq),
        in_specs=in_specs,
        out_specs=pl.BlockSpec((1, bq, width), lambda bi, g, qi: (bi, qi, g)),
        out_shape=jax.ShapeDtypeStruct((b, s, W_BRANCH), BF16),
        compiler_params=_compiler_params(("parallel", "parallel", "arbitrary")),
        name=kernel.__name__.strip("_"),
    )(*args)


def _local_rows(tm):
    return TOP_K * tm + N_EXPERTS * MOE_CHUNK


def _route_and_sort(logits, h2, xs_ref, len_ref, route_ref):
    tm = logits.shape[0]
    lt = logits.T[:N_EXPERTS, :]
    expert = lax.broadcasted_iota(jnp.int32, lt.shape, 0).astype(F32)
    m1 = jnp.max(lt, axis=0, keepdims=True)
    i1 = jnp.min(jnp.where(lt == m1, expert, float(N_EXPERTS)), axis=0, keepdims=True)
    lt2 = jnp.where(expert == i1, -jnp.inf, lt)
    m2 = jnp.max(lt2, axis=0, keepdims=True)
    i2 = jnp.min(jnp.where(lt2 == m2, expert, float(N_EXPERTS)), axis=0, keepdims=True)
    e2 = jnp.exp(m2 - m1)
    w1 = 1.0 / (1.0 + e2)
    w2 = e2 * w1
    chosen = jnp.where((expert == i1) | (expert == i2), 1.0, 0.0)
    yield
    before = (lax.broadcasted_iota(jnp.int32, (tm, tm), 0)
              < lax.broadcasted_iota(jnp.int32, (tm, tm), 1))
    rank = _dot(chosen.astype(BF16), jnp.where(before, 1.0, 0.0).astype(BF16))
    count = jnp.sum(chosen, axis=1, keepdims=True)
    padded = jnp.floor((count + (MOE_CHUNK - 1)) * (1.0 / MOE_CHUNK)) * MOE_CHUNK
    r1 = jnp.sum(jnp.where(expert == i1, rank, 0.0), axis=0, keepdims=True)
    r2 = jnp.sum(jnp.where(expert == i2, rank, 0.0), axis=0, keepdims=True)
    start = jnp.zeros((1, 1), F32)
    for e in range(N_EXPERTS):
        r1 = r1 + jnp.where(i1 == e, start, 0.0)
        r2 = r2 + jnp.where(i2 == e, start, 0.0)
        start = start + padded[e:e + 1, :]
    rows = _local_rows(tm)
    row = lax.broadcasted_iota(jnp.int32, (rows, tm), 0).astype(F32)
    place = jnp.where((row == r1) | (row == r2), 1.0, 0.0).astype(BF16)
    yield
    xs_ref[0] = _dot(place, h2).astype(BF16)
    len_ref[0] = jnp.broadcast_to(padded, (N_EXPERTS, LANES))
    field = lax.broadcasted_iota(jnp.int32, (LANES, tm), 0)
    fields = jnp.where(field == 0, r1, jnp.where(field == 1, r2, jnp.where(
        field == 2, w1, jnp.where(field == 3, w2, 0.0))))
    route_ref[...] = fields.T


def _merge_kernel(x_ref, ysb_ref, yfx_ref, gmix_ref, wgate_ref, bgate_ref, wosb_ref, wofx_ref,
                  wout_ref, gffn_ref, *rest, with_router):
    if with_router:
        wr_hi_ref, wr_lo_ref, x1_ref, xs_ref, len_ref, route_ref, h2_keep, logits_keep = rest
        step = pl.program_id(0)
        keep = lax.rem(step, 2)

        @pl.when(step == 0)
        def _():
            h2_keep[...] = jnp.zeros_like(h2_keep)
            logits_keep[...] = jnp.zeros_like(logits_keep)

        routing = _route_and_sort(logits_keep[1 - keep], h2_keep[1 - keep], xs_ref, len_ref, route_ref)
        next(routing)
    else:
        x1_ref, h2_ref = rest
    d = x_ref.shape[1]
    x = x_ref[...]
    h = _rmsnorm(x, gmix_ref[...]).astype(BF16)
    merged = None
    for i, (y_ref, wo_ref) in enumerate(((ysb_ref, wosb_ref), (yfx_ref, wofx_ref))):
        gate = _sigmoid(_dot(h, wgate_ref[:, i * d:(i + 1) * d]) + bgate_ref[:, i * d:(i + 1) * d])
        branch = gate * _dot(y_ref[...], wo_ref[...])
        merged = branch if merged is None else merged + branch
    if with_router:
        next(routing)
    x1 = x + _dot(merged.astype(BF16), wout_ref[...])
    x1_ref[...] = x1
    h2 = _rmsnorm(x1, gffn_ref[...])
    if with_router:
        h_hi, h_lo = _split_bf16(h2, 2)
        logits_keep[keep] = (_dot(h_hi, wr_hi_ref[...]) + _dot(h_lo, wr_hi_ref[...])
                             + _dot(h_hi, wr_lo_ref[...]))
        h2_keep[keep] = h_hi
        next(routing, None)
    else:
        h2_ref[...] = h2.astype(BF16)


def _merge(x, y_sb, y_fx, g_mix, w_gate, b_gate, w_o_sb, w_o_fox, w_out, g_ffn, w_router, tm):
    t, d = x.shape
    with_router = w_router is not None
    n_tiles = t // tm
    full = lambda i: (0, 0)
    tile = lambda i: (jnp.minimum(i, n_tiles - 1), 0)
    routed = lambda i: (jnp.maximum(i - 1, 0), 0)
    scratch = []
    in_specs = [
        pl.BlockSpec((tm, d), tile),
        pl.BlockSpec((tm, W_BRANCH), tile),
        pl.BlockSpec((tm, W_BRANCH), tile),
        pl.BlockSpec((1, d), full),
        pl.BlockSpec((d, 2 * d), full),
        pl.BlockSpec((1, 2 * d), full),
        pl.BlockSpec((W_BRANCH, d), full),
        pl.BlockSpec((W_BRANCH, d), full),
        pl.BlockSpec((d, d), full),
        pl.BlockSpec((1, d), full),
    ]
    args = [x, y_sb, y_fx, g_mix, w_gate, b_gate, w_o_sb, w_o_fox, w_out, g_ffn]
    out_specs = [pl.BlockSpec((tm, d), tile)]
    out_shape = [jax.ShapeDtypeStruct((t, d), F32)]
    if with_router:
        rows = _local_rows(tm)
        in_specs += [pl.BlockSpec((d, LANES), full)] * 2
        args += list(w_router)
        out_specs += [pl.BlockSpec((1, rows, d), lambda i: routed(i) + (0,)),
                      pl.BlockSpec((1, N_EXPERTS, LANES), lambda i: routed(i) + (0,)),
                      pl.BlockSpec((tm, LANES), routed)]
        out_shape += [jax.ShapeDtypeStruct((n_tiles, rows, d), BF16),
                      jax.ShapeDtypeStruct((n_tiles, N_EXPERTS, LANES), F32),
                      jax.ShapeDtypeStruct((t, LANES), F32)]
        scratch = [pltpu.VMEM((2, tm, d), BF16), pltpu.VMEM((2, tm, LANES), F32)]
    else:
        out_specs.append(pl.BlockSpec((tm, d), tile))
        out_shape.append(jax.ShapeDtypeStruct((t, d), BF16))
    return pl.pallas_call(
        functools.partial(_merge_kernel, with_router=with_router),
        grid=(n_tiles + (1 if with_router else 0),),
        in_specs=in_specs,
        out_specs=out_specs,
        out_shape=out_shape,
        scratch_shapes=scratch,
        compiler_params=_compiler_params(("arbitrary",) if with_router else ("parallel",)),
        name="merge_router" if with_router else "merge",
    )(*args)


def _swiglu_chunk(h, wg, wu, wd):
    g = _dot(h, wg)
    u = _dot(h, wu)
    return _dot((g * _sigmoid(g) * u).astype(BF16), wd)


def _ffn_kernel(h_ref, x_ref, wg_ref, wu_ref, wd_ref, o_ref):
    @pl.when(pl.program_id(1) == 0)
    def _():
        o_ref[...] = x_ref[...]

    o_ref[...] += _swiglu_chunk(h_ref[...], wg_ref[...], wu_ref[...], wd_ref[...])


def _ffn(h, x, w_gu, w_dn, tm, tf):
    t, d = x.shape
    d_ff = w_dn.shape[0]
    nf = d_ff // tf
    return pl.pallas_call(
        _ffn_kernel,
        grid=(t // tm, nf),
        in_specs=[
            pl.BlockSpec((tm, d), lambda i, j: (i, 0)),
            pl.BlockSpec((tm, d), lambda i, j: (i, 0)),
            pl.BlockSpec((d, tf), lambda i, j: (0, j)),
            pl.BlockSpec((d, tf), lambda i, j: (0, nf + j)),
            pl.BlockSpec((tf, d), lambda i, j: (j, 0)),
        ],
        out_specs=pl.BlockSpec((tm, d), lambda i, j: (i, 0)),
        out_shape=jax.ShapeDtypeStruct((t, d), F32),
        compiler_params=_compiler_params(("parallel", "arbitrary")),
        name="ffn",
    )(h, x, w_gu, w_gu, w_dn)


def _expert_kernel(expert_ref, valid_ref, src_ref, xs_hbm, wg_ref, wu_ref, wd_ref, o_ref, buf, sems):
    j = pl.program_id(0)
    rows = buf.shape[1]
    chunks = rows // MOE_CHUNK

    def fetches(step, slot):
        return [pltpu.make_async_copy(
            xs_hbm.at[src_ref[step * chunks + c]],
            buf.at[slot, pl.ds(c * MOE_CHUNK, MOE_CHUNK), :], sems.at[slot]) for c in range(chunks)]

    slot = lax.rem(j, 2)

    @pl.when(j == 0)
    def _():
        for copy in fetches(0, 0):
            copy.start()

    @pl.when(j + 1 < pl.num_programs(0))
    def _():
        for copy in fetches(j + 1, 1 - slot):
            copy.start()

    for copy in fetches(j, slot):
        copy.wait()

    @pl.when(valid_ref[j] != 0)
    def _():
        o_ref[...] = _swiglu_chunk(buf[slot], wg_ref[0], wu_ref[0], wd_ref[0]).astype(BF16)

    @pl.when(valid_ref[j] == 0)
    def _():
        o_ref[...] = jnp.zeros_like(o_ref)


def _experts(xs, tile_expert, tile_valid, chunk_src, w_gu, w_dn, n_rows):
    d = xs.shape[2]
    d_ff = w_dn.shape[1]
    te = MOE_EXPERT_TILE
    grid_spec = pltpu.PrefetchScalarGridSpec(
        num_scalar_prefetch=3,
        grid=(n_rows // te,),
        in_specs=[
            pl.BlockSpec(memory_space=pl.ANY),
            pl.BlockSpec((1, d, d_ff), lambda j, ex, va, sr: (ex[j], 0, 0)),
            pl.BlockSpec((1, d, d_ff), lambda j, ex, va, sr: (ex[j], 0, 1)),
            pl.BlockSpec((1, d_ff, d), lambda j, ex, va, sr: (ex[j], 0, 0)),
        ],
        out_specs=pl.BlockSpec((te, d), lambda j, ex, va, sr: (j, 0)),
        scratch_shapes=[pltpu.VMEM((2, te, d), BF16), pltpu.SemaphoreType.DMA((2,))],
    )
    return pl.pallas_call(
        _expert_kernel,
        grid_spec=grid_spec,
        out_shape=jax.ShapeDtypeStruct((n_rows, d), BF16),
        compiler_params=_compiler_params(("arbitrary",)),
        name="experts",
    )(tile_expert, tile_valid, chunk_src, xs, w_gu, w_gu, w_dn)


def _combine_kernel(goff_ref, count_ref, loff_ref, x_ref, route_ref, ys_hbm, o_ref, buf, sems):
    i = pl.program_id(0)
    slot = lax.rem(i, 2)

    def for_each_chunk(step, which, act):
        for e in range(N_EXPERTS):
            g = step * N_EXPERTS + e

            @pl.loop(0, count_ref[g])
            def _(c):
                dst = pl.multiple_of((loff_ref[g] + c) * MOE_CHUNK, MOE_CHUNK)
                act(pltpu.make_async_copy(ys_hbm.at[goff_ref[g] + c],
                                          buf.at[which, pl.ds(dst, MOE_CHUNK), :], sems.at[which]))

    @pl.when(i == 0)
    def _():
        buf[...] = jnp.zeros_like(buf)
        for_each_chunk(0, 0, lambda copy: copy.start())

    @pl.when(i + 1 < pl.num_programs(0))
    def _():
        for_each_chunk(i + 1, 1 - slot, lambda copy: copy.start())

    for_each_chunk(i, slot, lambda copy: copy.wait())

    route = route_ref[...]
    y = buf[slot]
    col = lax.broadcasted_iota(jnp.int32, (route.shape[0], buf.shape[1]), 1).astype(F32)
    out = x_ref[...]
    for k in range(TOP_K):
        pick = jnp.where(col == route[:, k:k + 1], 1.0, 0.0).astype(BF16)
        out = out + route[:, TOP_K + k:TOP_K + k + 1] * _dot(pick, y)
    o_ref[...] = out


def _combine(x, route, ys, goff, count, loff, tm):
    t, d = x.shape
    grid_spec = pltpu.PrefetchScalarGridSpec(
        num_scalar_prefetch=3,
        grid=(t // tm,),
        in_specs=[
            pl.BlockSpec((tm, d), lambda i, *_: (i, 0)),
            pl.BlockSpec((tm, LANES), lambda i, *_: (i, 0)),
            pl.BlockSpec(memory_space=pl.ANY),
        ],
        out_specs=pl.BlockSpec((tm, d), lambda i, *_: (i, 0)),
        scratch_shapes=[pltpu.VMEM((2, _local_rows(tm), d), BF16), pltpu.SemaphoreType.DMA((2,))],
    )
    return pl.pallas_call(
        _combine_kernel,
        grid_spec=grid_spec,
        out_shape=jax.ShapeDtypeStruct((t, d), F32),
        compiler_params=_compiler_params(("arbitrary",)),
        name="combine",
    )(goff, count, loff, x, route, ys)


def _moe(x1, xs_local, padded_len, route, w_gu, w_dn, tm):
    t, d = x1.shape
    n_tiles, local_rows, _ = xs_local.shape
    te = MOE_EXPERT_TILE
    n_rows = -(-(TOP_K * t + n_tiles * N_EXPERTS * (MOE_CHUNK - 1) + N_EXPERTS * (te - 1)) // te) * te
    length = padded_len[:, :, 0].astype(jnp.int32)
    count = length // MOE_CHUNK
    used = jnp.sum(length, axis=0)
    span = -(-used // te) * te
    base = jnp.cumsum(span) - span
    goff = (base[None, :] + jnp.cumsum(length, axis=0) - length) // MOE_CHUNK
    loff = (jnp.cumsum(length, axis=1) - length) // MOE_CHUNK
    seg_start = goff.T.reshape(-1)
    seg_count = count.T.reshape(-1)
    seg_src = (jnp.arange(n_tiles, dtype=jnp.int32)[:, None] * (local_rows // MOE_CHUNK) + loff).T.reshape(-1)
    chunk = jnp.arange(n_rows // MOE_CHUNK, dtype=jnp.int32)[:, None]
    within = chunk - seg_start[None, :]
    owner = (within >= 0) & (within < seg_count[None, :])
    chunk_src = jnp.sum(jnp.where(owner, seg_src[None, :] + within, 0), axis=1).astype(jnp.int32)
    tile_row = jnp.arange(n_rows // te, dtype=jnp.int32)[:, None] * te
    tile_expert = (jnp.sum(base[None, :] <= tile_row, axis=1) - 1).astype(jnp.int32)
    tile_valid = jnp.any((base[None, :] <= tile_row) & (tile_row < (base + used)[None, :]),
                         axis=1).astype(jnp.int32)
    ys = _experts(xs_local.reshape(-1, MOE_CHUNK, d), tile_expert, tile_valid, chunk_src, w_gu, w_dn, n_rows)
    return _combine(x1, route, ys.reshape(-1, MOE_CHUNK, d), goff.reshape(-1), count.reshape(-1),
                    loff.reshape(-1), tm)


def _pad_lanes(w):
    return jnp.pad(w, ((0, 0), (0, LANES - w.shape[1])))


def kernel(x, g_mix, w_in, b_f, b_gate, g_q, g_k, w_o_sb, w_o_fox, w_out, g_ffn, w_gu_dense,
           w_dn_dense, w_router, w_gu_exp, w_dn_exp):
    batch, seq, d = x.shape
    depth = w_in.shape[0]
    t = batch * seq
    tm = min(TOKEN_TILE, t)
    n_qkv = 6 * W_BRANCH
    xt = x.reshape(t, d)
    for l in range(depth):
        w_l = w_in[l]
        w_proj = jnp.concatenate(
            [w_l[:, :n_qkv], _pad_lanes(w_l[:, n_qkv:n_qkv + N_HEADS])], axis=1).astype(BF16)
        w_gate = w_l[:, n_qkv + N_HEADS:].astype(BF16)
        row = lambda v: v.reshape(1, -1)
        gq = jnp.tile(g_q[l], N_HEADS).reshape(1, W_BRANCH)
        gk = jnp.tile(g_k[l], N_HEADS).reshape(1, W_BRANCH)
        qkv_sb, qkv_fx, f_pre = _proj(xt, row(g_mix[l]), w_proj, gq, gk, tm)
        cum_f = _cumf(f_pre.reshape(batch, seq, LANES), _pad_lanes(row(b_f[l])))
        y_sb = _attention(_sb_kernel, qkv_sb.reshape(batch, seq, 3 * W_BRANCH))
        zmax = (QK_BOUND_MARGIN * 8.0 * jnp.max(jnp.abs(g_q[l])) * jnp.max(jnp.abs(g_k[l]))).reshape(1)
        y_fx = _attention(_fox_kernel, qkv_fx.reshape(batch, seq, 3 * W_BRANCH), cum_f, zmax)
        moe_layer = l % 2 == 1
        router = None
        if moe_layer:
            wr = _pad_lanes(w_router[l // 2])
            router = _split_bf16(wr, 2)
        outs = _merge(xt, y_sb.reshape(t, W_BRANCH), y_fx.reshape(t, W_BRANCH), row(g_mix[l]),
                      w_gate, row(b_gate[l]), w_o_sb[l].astype(BF16), w_o_fox[l].astype(BF16),
                      w_out[l].astype(BF16), row(g_ffn[l]), router, tm)
        if moe_layer:
            x1, xs_local, padded_len, route = outs
            xt = _moe(x1, xs_local, padded_len, route, w_gu_exp[l // 2].astype(BF16),
                      w_dn_exp[l // 2].astype(BF16), tm)
        else:
            x1, h2 = outs
            xt = _ffn(h2, x1, w_gu_dense[l // 2].astype(BF16), w_dn_dense[l // 2].astype(BF16),
                      min(2 * tm, t), w_dn_dense.shape[1] // 2)
    return xt.reshape(batch, seq, d)
```
